```python
import math
import jax
import jax.numpy as jnp
from jax import lax
import numpy as np

D_MODEL = 1024
BATCH = 8
SEQ = 4096
DEPTH = 4

Q_BLOCK = 128
LN_EPS = 1e-5
RMS_EPS = 1e-5
N_BUCKETS = 32
MAX_DISTANCE = 2048
N_BIAS_COLS = 16
A_HEADS = 4
A_QK = 64
A_V = 128
B_PAIRS = ((128, 1), (512, 4), (2048, 16))
B_HEADS = 4
B_DIM = 64
C_HEADS = 8
C_KV_GROUPS = 2
C_DIM = 64
CMP_LEN = 32
CMP_STRIDE = 16
CMP_HIDDEN = 256
SLC_BLOCK = 64
SLC_TOP_N = 16
SLC_Q_CHUNK = 32
WIN_SIZE = 512
D_HEADS = 8
D_Q_LORA = 384
D_KV_LORA = 256
D_NOPE = 64
D_ROPE = 32
D_V = 64
ROPE_THETA = 10000.0
D_FF = 2816
CONV_W = 3

EVEN_SPLITS = (2 * A_HEADS * A_QK, 2 * A_HEADS * A_QK, A_HEADS * A_V, len(B_PAIRS) * 3 * B_HEADS * B_DIM)
EVEN_IN = sum(EVEN_SPLITS)
EVEN_OUT = A_HEADS * A_V + B_HEADS * B_DIM
ODD_SPLITS = (C_HEADS * C_DIM, 3 * 2 * C_KV_GROUPS * C_DIM, 3 * C_HEADS, D_Q_LORA, D_KV_LORA, D_ROPE)
ODD_IN = sum(ODD_SPLITS)
ODD_OUT = C_HEADS * C_DIM + D_HEADS * D_V

kernel_name = 'hybrid_diff_dilated_nsa_mla_convffn'


def split_cols(h, sizes):
    outs, off = [], 0
    for s in sizes:
        outs.append(h[..., off:off + s])
        off += s
    return outs


def layer_norm(x, g, b):
    xf = x.astype(jnp.float32)
    mu = jnp.mean(xf, -1, keepdims=True)
    var = jnp.mean(jnp.square(xf - mu), -1, keepdims=True)
    return ((xf - mu) * lax.rsqrt(var + LN_EPS) * g + b).astype(x.dtype)


def rms_norm(x, g):
    xf = x.astype(jnp.float32)
    return (xf * lax.rsqrt(jnp.mean(xf * xf, -1, keepdims=True) + RMS_EPS) * g).astype(x.dtype)


def t5_bucket(dist):
    max_exact = N_BUCKETS // 2
    n = jnp.maximum(dist, 0)
    nf = jnp.maximum(n, 1).astype(jnp.float32)
    large = max_exact + (jnp.log(nf / max_exact) / math.log(MAX_DISTANCE / max_exact) * (N_BUCKETS - max_exact)).astype(jnp.int32)
    return jnp.where(n < max_exact, n, jnp.minimum(large, N_BUCKETS - 1))


def rope(x, pos):
    half = x.shape[-1] // 2
    inv = ROPE_THETA ** (-jnp.arange(half, dtype=jnp.float32) / half)
    ang = pos.astype(jnp.float32)[:, None] * inv
    ang = ang.reshape(ang.shape[:1] + (1,) * (x.ndim - 3) + (half,))
    cos, sin = jnp.cos(ang), jnp.sin(ang)
    xf = x.astype(jnp.float32)
    x1, x2 = xf[..., :half], xf[..., half:]
    return jnp.concatenate([x1 * cos - x2 * sin, x1 * sin + x2 * cos], -1).astype(x.dtype)


def banded_attention(q, k, v, max_dist, bias_tbl, dist_scale):
    N, L, G, Hg, D = q.shape
    blk = Q_BLOCK
    n_prev = -(-max_dist // blk)
    nb = -(-L // blk)
    pad = nb * blk - L
    qb = jnp.pad(q, ((0, 0), (0, pad), (0, 0), (0, 0), (0, 0))).reshape(N, nb, blk, G, Hg, D)
    kv_pad = ((0, 0), (n_prev * blk, pad), (0, 0), (0, 0))
    kb = jnp.pad(k, kv_pad).reshape(N, nb + n_prev, blk, G, D)
    vb = jnp.pad(v, kv_pad).reshape(N, nb + n_prev, blk, G, v.shape[-1])
    kw = jnp.concatenate([kb[:, j:j + nb] for j in range(n_prev + 1)], axis=2)
    vw = jnp.concatenate([vb[:, j:j + nb] for j in range(n_prev + 1)], axis=2)
    n_keys = (n_prev + 1) * blk
    kj = jnp.arange(n_keys)
    dist = jnp.arange(blk)[:, None] + n_prev * blk - kj[None, :]
    kpos = (jnp.arange(nb) * blk)[:, None, None] - n_prev * blk + kj[None, None, :]
    valid = (dist >= 0) & (dist <= max_dist) & (kpos >= 0)
    bias = bias_tbl[t5_bucket(dist * dist_scale)].transpose(2, 3, 0, 1)
    s = jnp.einsum('nbqghd,nbkgd->nbghqk', qb, kw).astype(jnp.float32) * (D ** -0.5) + bias
    s = jnp.where(valid[:, None, None], s, -jnp.inf)
    m = jnp.max(s, -1, keepdims=True)
    p = jnp.exp(s - m)
    den = jnp.sum(p, -1, keepdims=True)
    o = jnp.einsum('nbghqk,nbkgd->nbqghd', (p / den).astype(v.dtype), vw)
    lse = (m + jnp.log(den))[..., 0].transpose(0, 1, 4, 2, 3)
    o = o.reshape(N, nb * blk, G, Hg, v.shape[-1])[:, :L]
    lse = lse.reshape(N, nb * blk, G, Hg)[:, :L]
    return o, lse


def diff_attention(q, k, v, lam, bias_tbl):
    B, S, _, H, Dk = q.shape
    nb = S // Q_BLOCK
    kpos = jnp.arange(S)
    qb = q.reshape(B, nb, Q_BLOCK, 2, H, Dk).swapaxes(0, 1)

    def step(args):
        qi, start = args
        dist = (start + jnp.arange(Q_BLOCK))[:, None] - kpos[None, :]
        bias = bias_tbl[t5_bucket(dist)].transpose(2, 0, 1)
        s = jnp.einsum('bqmhd,bkmhd->bmhqk', qi, k).astype(jnp.float32) * (Dk ** -0.5) + bias
        s = jnp.where(dist >= 0, s, -jnp.inf)
        p = jax.nn.softmax(s, axis=-1)
        w = p[:, 0] - lam * p[:, 1]
        return jnp.einsum('bhqk,bkhd->bqhd', w.astype(v.dtype), v)

    o = lax.map(step, (qb, jnp.arange(nb) * Q_BLOCK))
    return o.swapaxes(0, 1).reshape(B, S, H, v.shape[-1])


def to_residue(t, d):
    B, S, H, D = t.shape
    return t.reshape(B, S // d, d, H, D).transpose(0, 2, 1, 3, 4).reshape(B * d, S // d, H, D)


def dilated_mixture(hb, bias_table):
    B, S = hb.shape[:2]
    outs, lses = [], []
    for i, (window, d) in enumerate(B_PAIRS):
        M = S // d
        col = A_HEADS + i * B_HEADS
        tbl = bias_table[:, col:col + B_HEADS][:, :, None]
        o, lse = banded_attention(to_residue(hb[:, :, i, 0], d)[:, :, :, None], to_residue(hb[:, :, i, 1], d),
                                  to_residue(hb[:, :, i, 2], d), window // d, tbl, d)
        outs.append(o.reshape(B, d, M, B_HEADS, B_DIM).transpose(0, 2, 1, 3, 4).reshape(B, S, B_HEADS, B_DIM))
        lses.append(lse.reshape(B, d, M, B_HEADS).transpose(0, 2, 1, 3).reshape(B, S, B_HEADS))
    wts = jax.nn.softmax(jnp.stack(lses), axis=0)
    return jnp.einsum('pbsh,pbshd->bshd', wts.astype(outs[0].dtype), jnp.stack(outs))


def nsa_compress(t, pe, w1, w2):
    B, S, G, D = t.shape
    ch = t.reshape(B, S // CMP_STRIDE, CMP_STRIDE, G, D)
    blocks = jnp.concatenate([ch[:, :-1], ch[:, 1:]], axis=2)
    blocks = blocks + pe[:, None, :]
    flat = blocks.transpose(0, 1, 3, 2, 4).reshape(B, blocks.shape[1], G, CMP_LEN * D)
    return jax.nn.gelu(flat @ w1) @ w2


def nsa_compressed(q, k, v, pe, w1, w2, tbl):
    B, S, G, Hg, D = q.shape
    kc = nsa_compress(k, pe[0], w1[0], w2[0])
    vc = nsa_compress(v, pe[1], w1[1], w2[1])
    ncb = kc.shape[1]
    kend = jnp.arange(ncb) * CMP_STRIDE + CMP_LEN - 1
    dist = jnp.arange(S)[:, None] - kend[None, :]
    bias = tbl[t5_bucket(dist)].transpose(2, 3, 0, 1)
    s = jnp.einsum('bqghd,bcgd->bghqc', q, kc).astype(jnp.float32) * (D ** -0.5) + bias
    s = jnp.where(dist >= 0, s, -jnp.inf)
    m = jnp.max(s, -1, keepdims=True)
    p = jnp.exp(s - jnp.where(jnp.isfinite(m), m, 0.0))
    den = jnp.sum(p, -1, keepdims=True)
    p = p / jnp.where(den > 0, den, 1.0)
    o = jnp.einsum('bghqc,bcgd->bqghd', p.astype(v.dtype), vc)
    n_sel = S // SLC_BLOCK
    r = SLC_BLOCK // CMP_STRIDE
    pg = jnp.sum(p, axis=2)
    pp = jnp.pad(pg, ((0, 0), (0, 0), (0, 0), (1, r * (n_sel + 1) - 1 - ncb))).reshape(B, G, S, n_sel + 1, r)
    score = 0.5 * pp[..., :-1, 0] + jnp.sum(pp[..., :-1, 1:], -1) + 0.5 * pp[..., 1:, 0]
    return o, score


def nsa_selected(q, k, v, slc_score, tbl):
    B, S, G, Hg, D = q.shape
    n_sel = S // SLC_BLOCK
    top_n = min(SLC_TOP_N, n_sel)
    qblk = jnp.arange(S) // SLC_BLOCK
    jb = jnp.arange(n_sel)[None, :]
    allowed = jb <= qblk[:, None]
    forced = (jb == 0) | (jb == qblk[:, None]) | (jb == qblk[:, None] - 1)
    score = jnp.where(forced, jnp.inf, jnp.where(allowed, slc_score.astype(jnp.float32), -jnp.inf))
    _, idx = lax.top_k(score, top_n)
    kb = k.reshape(B, n_sel, SLC_BLOCK, G, D).transpose(0, 3, 1, 2, 4)
    vb = v.reshape(B, n_sel, SLC_BLOCK, G, v.shape[-1]).transpose(0, 3, 1, 2, 4)
    nq = S // SLC_Q_CHUNK
    idx_c = idx.transpose(0, 2, 1, 3).reshape(B, nq, SLC_Q_CHUNK, G, top_n).swapaxes(0, 1)
    q_c = q.reshape(B, nq, SLC_Q_CHUNK, G, Hg, D).swapaxes(0, 1)
    bi = jnp.arange(B)[:, None, None, None]
    gi = jnp.arange(G)[None, None, :, None]
    tok = jnp.arange(SLC_BLOCK)

    def step(args):
        qi, ii, start = args
        kg = kb[bi, gi, ii]
        vg = vb[bi, gi, ii]
        qp = start + jnp.arange(SLC_Q_CHUNK)
        dist = qp[None, :, None, None, None] - (ii[..., None] * SLC_BLOCK + tok)
        bias = tbl[t5_bucket(dist), gi[..., None]]
        s = jnp.einsum('bqghd,bqgnkd->bqghnk', qi, kg).astype(jnp.float32) * (D ** -0.5) + jnp.moveaxis(bias, -1, 3)
        s = jnp.where((dist >= 0)[:, :, :, None], s, -jnp.inf)
        p = jax.nn.softmax(s.reshape(s.shape[:4] + (-1,)), axis=-1).astype(v.dtype)
        return jnp.einsum('bqghk,bqgkd->bqghd', p, vg.reshape(vg.shape[:3] + (-1, vg.shape[-1])))

    o = lax.map(step, (q_c, idx_c, jnp.arange(nq) * SLC_Q_CHUNK))
    return o.swapaxes(0, 1).reshape(B, S, G, Hg, v.shape[-1])


def mla_attention(q_nope, q_rope, k_nope, k_rope, v):
    B, S, H, _ = q_nope.shape
    nb = S // Q_BLOCK
    kpos = jnp.arange(S)
    scale = (D_NOPE + D_ROPE) ** -0.5

    def blocks(t):
        return t.reshape((B, nb, Q_BLOCK) + t.shape[2:]).swapaxes(0, 1)

    def step(args):
        qn, qr, start = args
        s = (jnp.einsum('bqhd,bkhd->bhqk', qn, k_nope) + jnp.einsum('bqhr,bkr->bhqk', qr, k_rope)).astype(jnp.float32) * scale
        s = jnp.where((start + jnp.arange(Q_BLOCK))[:, None] >= kpos[None, :], s, -jnp.inf)
        p = jax.nn.softmax(s, axis=-1).astype(v.dtype)
        return jnp.einsum('bhqk,bkhd->bqhd', p, v)

    o = lax.map(step, (blocks(q_nope), blocks(q_rope), jnp.arange(nb) * Q_BLOCK))
    return o.swapaxes(0, 1).reshape(B, S, H, v.shape[-1])


def even_mixer(x, w_in, w_out, lam_p, subln_g, bias_table, layer_idx):
    B, S, _ = x.shape
    qa, ka, va, hb = split_cols(x @ w_in, EVEN_SPLITS)
    qa = qa.reshape(B, S, 2, A_HEADS, A_QK)
    ka = ka.reshape(B, S, 2, A_HEADS, A_QK)
    va = va.reshape(B, S, A_HEADS, A_V)
    lam_init = 0.8 - 0.6 * math.exp(-0.3 * layer_idx)
    lp = lam_p.astype(jnp.float32)
    lam = jnp.exp(jnp.sum(lp[0] * lp[1])) - jnp.exp(jnp.sum(lp[2] * lp[3])) + lam_init
    o_a = diff_attention(qa, ka, va, lam, bias_table[:, :A_HEADS])
    o_a = rms_norm(o_a, subln_g) * (1.0 - lam_init)
    o_b = dilated_mixture(hb.reshape(B, S, len(B_PAIRS), 3, B_HEADS, B_DIM), bias_table)
    y = jnp.concatenate([o_a.reshape(B, S, -1), o_b.reshape(B, S, -1)], axis=-1)
    return y @ w_out


def odd_mixer(x, w_in, w_out, cmp_pe, cmp_w1, cmp_w2, q_norm_g, kv_norm_g, w_uq, w_uk, w_uv, bias_table):
    B, S, _ = x.shape
    G, Hg = C_KV_GROUPS, C_HEADS // C_KV_GROUPS
    q_c, kv_c, gate_c, cq, ckv, k_rope = split_cols(x @ w_in, ODD_SPLITS)
    q = q_c.reshape(B, S, G, Hg, C_DIM)
    kv = kv_c.reshape(B, S, 3, 2, G, C_DIM)
    gates = jax.nn.sigmoid(gate_c.astype(jnp.float32)).reshape(B, S, 3, G, Hg, 1).astype(x.dtype)
    tbl_c = bias_table[:, :C_HEADS].reshape(N_BUCKETS, G, Hg)
    o_cmp, slc_score = nsa_compressed(q, kv[:, :, 0, 0], kv[:, :, 0, 1], cmp_pe, cmp_w1, cmp_w2, tbl_c)
    o_slc = nsa_selected(q, kv[:, :, 1, 0], kv[:, :, 1, 1], slc_score, tbl_c)
    o_win, _ = banded_attention(q, kv[:, :, 2, 0], kv[:, :, 2, 1], WIN_SIZE - 1, tbl_c, 1)
    o_c = gates[:, :, 0] * o_cmp + gates[:, :, 1] * o_slc + gates[:, :, 2] * o_win
    pos = jnp.arange(S)
    qd = (rms_norm(cq, q_norm_g) @ w_uq).reshape(B, S, D_HEADS, D_NOPE + D_ROPE)
    c = rms_norm(ckv, kv_norm_g)
    k_nope = (c @ w_uk).reshape(B, S, D_HEADS, D_NOPE)
    v = (c @ w_uv).reshape(B, S, D_HEADS, D_V)
    o_d = mla_attention(qd[..., :D_NOPE], rope(qd[..., D_NOPE:], pos), k_nope, rope(k_rope, pos), v)
    y = jnp.concatenate([o_c.reshape(B, S, -1), o_d.reshape(B, S, -1)], axis=-1)
    return y @ w_out


def conv_ffn(x, w_up, conv_w, conv_b, w_down):
    a, g = jnp.split(x @ w_up, 2, axis=-1)
    g = lax.conv_general_dilated(g, conv_w[:, None, :].astype(g.dtype), window_strides=(1,),
                                 padding=((CONV_W - 1, 0),), dimension_numbers=('NWC', 'WIO', 'NWC'),
                                 feature_group_count=D_FF) + conv_b
    return (jax.nn.gelu(g) * a) @ w_down


def setup_inputs(seed: int = 0) -> dict:
    key = jax.random.key(seed)
    ks = jax.random.split(key, 22)
    f32 = jnp.float32
    n_ev, n_od = (DEPTH + 1) // 2, DEPTH // 2
    beta = (8 * DEPTH) ** -0.25

    def w(k, shape, fan_in, gain=1.0):
        return jax.random.normal(k, shape, f32) * (gain * fan_in ** -0.5)

    def near_one(k, shape):
        return 1.0 + 0.02 * jax.random.normal(k, shape, f32)

    def small(k, shape, std):
        return std * jax.random.normal(k, shape, f32)

    return {
        'x': jax.random.normal(ks[0], (BATCH, SEQ, D_MODEL), f32),
        'rel_bias': small(ks[1], (N_BUCKETS, N_BIAS_COLS), 0.2),
        'ev_w_in': w(ks[2], (n_ev, D_MODEL, EVEN_IN), D_MODEL),
        'ev_w_out': w(ks[3], (n_ev, EVEN_OUT, D_MODEL), EVEN_OUT, beta),
        'ev_lambda': small(ks[4], (n_ev, 4, A_QK), 0.1),
        'ev_subln': near_one(ks[5], (n_ev, A_V)),
        'od_w_in': w(ks[6], (n_od, D_MODEL, ODD_IN), D_MODEL),
        'od_w_out': w(ks[7], (n_od, ODD_OUT, D_MODEL), ODD_OUT, beta),
        'od_cmp_pe': small(ks[8], (n_od, 2, CMP_LEN, C_DIM), 0.1),
        'od_cmp_w1': w(ks[9], (n_od, 2, CMP_LEN * C_DIM, CMP_HIDDEN), CMP_LEN * C_DIM),
        'od_cmp_w2': w(ks[10], (n_od, 2, CMP_HIDDEN, C_DIM), CMP_HIDDEN),
        'od_q_norm': near_one(ks[11], (n_od, D_Q_LORA)),
        'od_kv_norm': near_one(ks[12], (n_od, D_KV_LORA)),
        'od_w_uq': w(ks[13], (n_od, D_Q_LORA, D_HEADS * (D_NOPE + D_ROPE)), D_Q_LORA),
        'od_w_uk': w(ks[14], (n_od, D_KV_LORA, D_HEADS * D_NOPE), D_KV_LORA),
        'od_w_uv': w(ks[15], (n_od, D_KV_LORA, D_HEADS * D_V), D_KV_LORA),
        'ffn_w_up': w(ks[16], (DEPTH, D_MODEL, 2 * D_FF), D_MODEL),
        'ffn_conv_w': w(ks[17], (DEPTH, CONV_W, D_FF), CONV_W),
        'ffn_conv_b': small(ks[18], (DEPTH, D_FF), 0.02),
        'ffn_w_down': w(ks[19], (DEPTH, D_FF, D_MODEL), D_FF, beta),
        'ln_g': near_one(ks[20], (DEPTH, 2, D_MODEL)),
        'ln_b': small(ks[21], (DEPTH, 2, D_MODEL), 0.02),
    }


def reference(x, rel_bias, ev_w_in, ev_w_out, ev_lambda, ev_subln, od_w_in, od_w_out, od_cmp_pe, od_cmp_w1,
              od_cmp_w2, od_q_norm, od_kv_norm, od_w_uq, od_w_uk, od_w_uv, ffn_w_up, ffn_conv_w, ffn_conv_b,
              ffn_w_down, ln_g, ln_b):
    alpha = (2 * DEPTH) ** 0.25
    for l in range(DEPTH):
        i = l // 2
        if l % 2 == 0:
            y = even_mixer(x, ev_w_in[i], ev_w_out[i], ev_lambda[i], ev_subln[i], rel_bias, l)
        else:
            y = odd_mixer(x, od_w_in[i], od_w_out[i], od_cmp_pe[i], od_cmp_w1[i], od_cmp_w2[i], od_q_norm[i],
                          od_kv_norm[i], od_w_uq[i], od_w_uk[i], od_w_uv[i], rel_bias)
        x = layer_norm(alpha * x + y, ln_g[l, 0], ln_b[l, 0])
        y = conv_ffn(x, ffn_w_up[l], ffn_conv_w[l], ffn_conv_b[l], ffn_w_down[l])
        x = layer_norm(alpha * x + y, ln_g[l, 1], ln_b[l, 1])
    return x
```

```python
import functools
import math

import numpy as np
import jax
import jax.numpy as jnp
from jax import lax
from jax.experimental import pallas as pl
from jax.experimental.pallas import tpu as pltpu

F32 = jnp.float32
BF16 = jnp.bfloat16

D_MODEL = 1024
DEPTH = 4
LN_EPS = 1e-5
RMS_EPS = 1e-5
N_BUCKETS = 32
MAX_DISTANCE = 2048
A_HEADS, A_QK, A_V = 4, 64, 128
B_PAIRS = ((128, 1), (512, 4), (2048, 16))
B_HEADS, B_DIM = 4, 64
C_HEADS, C_KV_GROUPS, C_DIM = 8, 2, 64
CMP_LEN, CMP_STRIDE, CMP_HIDDEN = 32, 16, 256
SLC_BLOCK, SLC_TOP_N = 64, 16
SLC_SHIFT = 6
WIN_SIZE = 512
D_HEADS, D_Q_LORA, D_KV_LORA, D_NOPE, D_ROPE, D_V = 8, 384, 256, 64, 32, 64
ROPE_THETA = 10000.0
D_FF = 2816
EVEN_IN = 2 * A_HEADS * A_QK * 2 + A_HEADS * A_V + len(B_PAIRS) * 3 * B_HEADS * B_DIM
ALPHA = (2 * DEPTH) ** 0.25

LANES = 128
VMEM_LIMIT = 56 * 1024 * 1024
T_ATT = 256
T_DIL = 128
TM = 512
FF_CHUNK = 256
NEG_INF = float("-inf")
M_INIT = -1e30


def _cparams(sem):
    return pltpu.CompilerParams(dimension_semantics=sem, vmem_limit_bytes=VMEM_LIMIT)


def _dot(a, b):
    return jnp.dot(a, b, preferred_element_type=F32)


def _dot_nt(a, b):
    return lax.dot_general(a, b, (((1,), (1,)), ((), ())), preferred_element_type=F32)


def _bucket_np(dist):
    n = np.maximum(dist, 0)
    nf = np.maximum(n, 1).astype(np.float32)
    max_exact = N_BUCKETS // 2
    large = max_exact + (np.log(nf / max_exact) / math.log(MAX_DISTANCE / max_exact)
                         * (N_BUCKETS - max_exact)).astype(np.int32)
    return np.where(n < max_exact, n, np.minimum(large, N_BUCKETS - 1)).astype(np.int32)


def _toeplitz_idx(n_delta, t, max_dist, dist_scale=1):
    r = np.arange(t)[:, None]
    c = np.arange(t)[None, :]
    out = []
    for delta in range(n_delta):
        dist = t * delta + r - c
        ok = (dist >= 0) & (dist <= max_dist)
        out.append(np.where(ok, _bucket_np(dist * dist_scale), -1))
    return np.stack(out).astype(np.int32)


def _dilated_idx(d):
    t = T_DIL
    r = np.arange(t)[:, None]
    c = np.arange(2 * t)[None, :]
    dist0 = r - c
    ok0 = (dist0 >= 0) & (c < t)
    dist1 = t + r - c
    ok1 = (dist1 >= 0) & (dist1 <= t)
    return np.stack([np.where(ok0, _bucket_np(dist0 * d), -1),
                     np.where(ok1, _bucket_np(dist1 * d), -1)]).astype(np.int32)


def _cmp_idx(seq):
    ncb = seq // CMP_STRIDE - 1
    q = np.arange(seq)[:, None]
    c = np.arange(seq // CMP_STRIDE)[None, :]
    dist = q - (c * CMP_STRIDE + CMP_LEN - 1)
    ok = (dist >= 0) & (c < ncb)
    return np.where(ok, _bucket_np(dist), -1).astype(np.int32).reshape(seq // T_ATT, T_ATT, seq // CMP_STRIDE)


def _bias_kernel(tbl_ref, idx_ref, o_ref):
    h = pl.program_id(0)
    idx = idx_ref[0]
    out = jnp.full(idx.shape, NEG_INF, F32)
    for b in range(N_BUCKETS):
        out = jnp.where(idx == b, tbl_ref[h, b], out)
    o_ref[0, 0] = out


def _bias_table(tbl, idx):
    n_h = tbl.shape[0]
    n, r, c = idx.shape
    return pl.pallas_call(
        _bias_kernel,
        grid=(n_h, n),
        in_specs=[pl.BlockSpec(memory_space=pltpu.SMEM),
                  pl.BlockSpec((1, r, c), lambda h, i: (i, 0, 0))],
        out_specs=pl.BlockSpec((1, 1, r, c), lambda h, i: (h, i, 0, 0)),
        out_shape=jax.ShapeDtypeStruct((n_h, n, r, c), F32),
        compiler_params=_cparams(("arbitrary", "arbitrary")),
        name="bias_table",
    )(tbl, jnp.asarray(idx))


def _inproj_kernel(x_ref, w_ref, *o_refs, widths, chunk):
    xb = x_ref[...].astype(BF16)
    off = 0
    for o_ref, width in zip(o_refs, widths):
        for c0 in range(0, width, chunk):
            o_ref[:, c0:c0 + chunk] = _dot(xb, w_ref[:, off + c0:off + c0 + chunk]).astype(o_ref.dtype)
        off += width


def _inproj(x, w, widths, dtypes, chunk):
    m = x.shape[0]
    n = w.shape[1]
    assert sum(widths) == n and all(wd % chunk == 0 for wd in widths)
    return pl.pallas_call(
        functools.partial(_inproj_kernel, widths=widths, chunk=chunk),
        grid=(m // TM,),
        in_specs=[pl.BlockSpec((TM, D_MODEL), lambda i: (i, 0)),
                  pl.BlockSpec((D_MODEL, n), lambda i: (0, 0))],
        out_specs=[pl.BlockSpec((TM, wd), lambda i: (i, 0)) for wd in widths],
        out_shape=[jax.ShapeDtypeStruct((m, wd), dt) for wd, dt in zip(widths, dtypes)],
        compiler_params=_cparams(("arbitrary",)),
        name="in_proj",
    )(x, w)


def _flash(q, k_ref, k_cols, v_ref, v_cols, j_lo, j_hi, tile_fn, t_k):
    t_q = q.shape[0]
    d_v = v_cols.stop - v_cols.start

    def body(j, carry):
        m, l, acc = carry
        rows = pl.ds(pl.multiple_of(j * t_k, t_k), t_k)
        s = tile_fn(j, _dot_nt(q, k_ref[0, rows, k_cols]))
        m_new = jnp.maximum(m, jnp.max(s, axis=-1, keepdims=True))
        alpha = jnp.exp(m - m_new)
        p = jnp.exp(s - m_new)
        l = alpha * l + jnp.sum(p, axis=-1, keepdims=True)
        acc = alpha * acc + _dot(p.astype(BF16), v_ref[0, rows, v_cols])
        return m_new, l, acc

    init = (jnp.full((t_q, 1), M_INIT, F32), jnp.zeros((t_q, 1), F32), jnp.zeros((t_q, d_v), F32))
    _, l, acc = lax.fori_loop(j_lo, j_hi, body, init)
    return acc / l


def _lane_lo(rows):
    return lax.broadcasted_iota(jnp.int32, (rows, LANES), 1) < (LANES // 2)


def _diff_kernel(lam_ref, g_ref, q0_ref, q1_ref, k0_ref, k1_ref, v_ref, tb_ref, o_ref, *, lam_init):
    qi = pl.program_id(2)
    lp = lam_ref[...]
    lam = (jnp.exp(jnp.sum(lp[0:1] * lp[1:2], axis=-1, keepdims=True))
           - jnp.exp(jnp.sum(lp[2:3] * lp[3:4], axis=-1, keepdims=True)) + lam_init)
    lo = _lane_lo(T_ATT)
    full = slice(0, LANES)
    for half in range(2):
        keep = lo if half == 0 else jnp.logical_not(lo)
        v_cols = slice(half * A_V, (half + 1) * A_V)

        def tile_fn(j, s, half=half):
            return s + tb_ref[half, qi - j]

        outs = []
        for q_ref, k_ref in ((q0_ref, k0_ref), (q1_ref, k1_ref)):
            q = q_ref[0]
            qm = jnp.where(keep, q * (A_QK ** -0.5), jnp.zeros_like(q))
            outs.append(_flash(qm, k_ref, full, v_ref, v_cols, 0, qi + 1, tile_fn, T_ATT))
        d = outs[0] - lam * outs[1]
        ms = jnp.mean(d * d, axis=-1, keepdims=True)
        o_ref[0, :, v_cols] = (d * lax.rsqrt(ms + RMS_EPS) * g_ref[...] * (1.0 - lam_init)).astype(o_ref.dtype)


def _diff_attention(h3, tb, lam_p, subln_g, lam_init):
    b, s, _ = h3.shape
    t = T_ATT
    n_d = s // t
    return pl.pallas_call(
        functools.partial(_diff_kernel, lam_init=lam_init),
        grid=(2, b, s // t),
        in_specs=[pl.BlockSpec((4, A_QK), lambda hp, bi, qi: (0, 0)),
                  pl.BlockSpec((1, A_V), lambda hp, bi, qi: (0, 0)),
                  pl.BlockSpec((1, t, LANES), lambda hp, bi, qi: (bi, qi, hp)),
                  pl.BlockSpec((1, t, LANES), lambda hp, bi, qi: (bi, qi, 2 + hp)),
                  pl.BlockSpec((1, s, LANES), lambda hp, bi, qi: (bi, 0, 4 + hp)),
                  pl.BlockSpec((1, s, LANES), lambda hp, bi, qi: (bi, 0, 6 + hp)),
                  pl.BlockSpec((1, s, 2 * A_V), lambda hp, bi, qi: (bi, 0, 4 + hp)),
                  pl.BlockSpec((2, n_d, t, t), lambda hp, bi, qi: (hp, 0, 0, 0))],
        out_specs=pl.BlockSpec((1, t, 2 * A_V), lambda hp, bi, qi: (bi, qi, hp)),
        out_shape=jax.ShapeDtypeStruct((b, s, A_HEADS * A_V), BF16),
        compiler_params=_cparams(("arbitrary", "arbitrary", "arbitrary")),
        name="diff_attention",
    )(lam_p, subln_g.reshape(1, A_V), h3, h3, h3, h3, h3, tb)


def _dilated_kernel(q_ref, k_ref, v_ref, tb_ref, o_ref, lse_ref, *, rows_per_step):
    ci = pl.program_id(2)
    t = T_DIL
    lo = _lane_lo(t)
    scale = B_DIM ** -0.5

    def block(bi_local, carry):
        bi = ci * (rows_per_step // t) + bi_local
        var = jnp.minimum(bi, 1)
        krows = pl.ds(pl.multiple_of(jnp.maximum(bi - 1, 0) * t, t), 2 * t)
        qrows = pl.ds(pl.multiple_of(bi_local * t, t), t)
        for pair in range(2):
            cols = slice(pair * LANES, (pair + 1) * LANES)
            q = q_ref[0, qrows, cols]
            kk = k_ref[0, krows, cols]
            vv = v_ref[0, krows, cols]
            o_half, l_half = [], []
            for half in range(2):
                keep = lo if half == 0 else jnp.logical_not(lo)
                qm = jnp.where(keep, q * scale, jnp.zeros_like(q))
                s = _dot_nt(qm, kk) + tb_ref[2 * pair + half, var]
                m = jnp.max(s, axis=-1, keepdims=True)
                p = jnp.exp(s - m)
                l = jnp.sum(p, axis=-1, keepdims=True)
                o_half.append(_dot(p.astype(BF16), vv) / l)
                l_half.append(m + jnp.log(l))
            o_ref[0, qrows, cols] = jnp.where(lo, o_half[0], o_half[1])
            lse_ref[0, qrows, cols] = jnp.where(lo, l_half[0], l_half[1])
        return carry

    lax.fori_loop(0, rows_per_step // t, block, 0)


def _dilated_attention(h3, tb, pair_idx, d):
    b, s, n_cols = h3.shape
    l_res = s // d
    hd = B_HEADS * B_DIM
    rows = min(l_res, 1024)
    hv = h3.reshape(b, l_res, d * n_cols)
    cpb = n_cols // hd
    base = (2 * A_HEADS * A_QK * 2 + A_HEADS * A_V) // hd + 3 * pair_idx
    o, lse = pl.pallas_call(
        functools.partial(_dilated_kernel, rows_per_step=rows),
        grid=(b, d, l_res // rows),
        in_specs=[pl.BlockSpec((1, rows, hd), lambda bi, r, c: (bi, c, r * cpb + base)),
                  pl.BlockSpec((1, l_res, hd), lambda bi, r, c: (bi, 0, r * cpb + base + 1)),
                  pl.BlockSpec((1, l_res, hd), lambda bi, r, c: (bi, 0, r * cpb + base + 2)),
                  pl.BlockSpec((B_HEADS, 2, T_DIL, 2 * T_DIL), lambda bi, r, c: (0, 0, 0, 0))],
        out_specs=[pl.BlockSpec((1, rows, hd), lambda bi, r, c: (bi, c, r)),
                   pl.BlockSpec((1, rows, hd), lambda bi, r, c: (bi, c, r))],
        out_shape=[jax.ShapeDtypeStruct((b, l_res, d * hd), F32)] * 2,
        compiler_params=_cparams(("arbitrary", "arbitrary", "arbitrary")),
        name="dilated_attention",
    )(hv, hv, hv, tb)
    return o.reshape(b * s, hd), lse.reshape(b * s, hd)


def _residual_ln(x, y, g, beta):
    z = ALPHA * x + y
    mu = jnp.mean(z, axis=-1, keepdims=True)
    zc = z - mu
    var = jnp.mean(zc * zc, axis=-1, keepdims=True)
    return zc * lax.rsqrt(var + LN_EPS) * g + beta


def _even_out_kernel(oa_ref, o0_ref, o1_ref, o2_ref, l0_ref, l1_ref, l2_ref, x_ref, w_ref, g_ref, b_ref, out_ref):
    l0, l1, l2 = l0_ref[...], l1_ref[...], l2_ref[...]
    mx = jnp.maximum(jnp.maximum(l0, l1), l2)
    e0, e1, e2 = jnp.exp(l0 - mx), jnp.exp(l1 - mx), jnp.exp(l2 - mx)
    den = e0 + e1 + e2
    ob = (e0 / den) * o0_ref[...] + (e1 / den) * o1_ref[...] + (e2 / den) * o2_ref[...]
    n_a = A_HEADS * A_V
    y = _dot(oa_ref[...], w_ref[0:n_a, :]) + _dot(ob.astype(BF16), w_ref[n_a:, :])
    out_ref[...] = _residual_ln(x_ref[...], y, g_ref[...], b_ref[...])


def _even_out(oa, obs, lses, x, w_out, g, beta):
    m = x.shape[0]
    hd = B_HEADS * B_DIM
    row = lambda width: pl.BlockSpec((TM, width), lambda i: (i, 0))
    const = lambda shape: pl.BlockSpec(shape, lambda i: (0, 0))
    return pl.pallas_call(
        _even_out_kernel,
        grid=(m // TM,),
        in_specs=[row(A_HEADS * A_V)] + [row(hd)] * 6 + [row(D_MODEL), const(w_out.shape),
                                                        const((1, D_MODEL)), const((1, D_MODEL))],
        out_specs=row(D_MODEL),
        out_shape=jax.ShapeDtypeStruct((m, D_MODEL), F32),
        compiler_params=_cparams(("arbitrary",)),
        name="even_out",
    )(oa, *obs, *lses, x, w_out, g.reshape(1, -1), beta.reshape(1, -1))


def _gelu(x):
    return 0.5 * x * (1.0 + jnp.tanh(math.sqrt(2.0 / math.pi) * (x + 0.044715 * (x * x * x))))


def _ffn_kernel(x_ref, wu_ref, cw_ref, cb_ref, wd_ref, g_ref, b_ref, out_ref, acc_ref, gs_ref, tail_ref,
                *, tiles_per_seq):
    halo = 8

    @pl.when(pl.program_id(0) % tiles_per_seq == 0)
    def _():
        tail_ref[...] = jnp.zeros(tail_ref.shape, F32)

    x = x_ref[...]
    xb = x.astype(BF16)
    for c in range(D_FF // FF_CHUNK):
        cols = slice(c * FF_CHUNK, (c + 1) * FF_CHUNK)
        a = _dot(xb, wu_ref[:, cols])
        gate = _dot(xb, wu_ref[:, D_FF + c * FF_CHUNK:D_FF + (c + 1) * FF_CHUNK])
        gs_ref[0:halo, :] = tail_ref[:, cols]
        gs_ref[halo:, :] = gate
        tail_ref[:, cols] = gate[TM - halo:, :]
        conv = (gs_ref[pl.ds(halo - 2, TM), :] * cw_ref[0:1, cols]
                + gs_ref[pl.ds(halo - 1, TM), :] * cw_ref[1:2, cols]
                + gate * cw_ref[2:3, cols] + cb_ref[:, cols])
        u = (_gelu(conv) * a).astype(BF16)
        part = _dot(u, wd_ref[cols, :])
        if c == 0:
            acc_ref[...] = part
        else:
            acc_ref[...] += part
    out_ref[...] = _residual_ln(x, acc_ref[...], g_ref[...], b_ref[...])


def _ffn(x, w_up, conv_w, conv_b, w_down, g, beta, seq):
    m = x.shape[0]
    const = lambda shape: pl.BlockSpec(shape, lambda i: (0, 0))
    return pl.pallas_call(
        functools.partial(_ffn_kernel, tiles_per_seq=seq // TM),
        grid=(m // TM,),
        in_specs=[pl.BlockSpec((TM, D_MODEL), lambda i: (i, 0)), const(w_up.shape), const(conv_w.shape),
                  const((1, D_FF)), const(w_down.shape), const((1, D_MODEL)), const((1, D_MODEL))],
        out_specs=pl.BlockSpec((TM, D_MODEL), lambda i: (i, 0)),
        out_shape=jax.ShapeDtypeStruct((m, D_MODEL), F32),
        scratch_shapes=[pltpu.VMEM((TM, D_MODEL), F32), pltpu.VMEM((TM + 8, FF_CHUNK), F32),
                        pltpu.VMEM((8, D_FF), F32)],
        compiler_params=_cparams(("arbitrary",)),
        name="conv_ffn",
    )(x, w_up, conv_w, conv_b.reshape(1, -1), w_down, g.reshape(1, -1), beta.reshape(1, -1))


def _compress_kernel(ch_ref, pe_ref, w1_ref, w2_ref, o_ref):
    half = CMP_STRIDE * C_DIM
    ch = ch_ref[0, 0]
    a = _dot((ch + pe_ref[0, 0:1, :]).astype(BF16), w1_ref[0, 0:half, :])
    b = _dot((ch + pe_ref[0, 1:2, :]).astype(BF16), w1_ref[0, half:, :])
    n = ch.shape[0]
    hid = _gelu(a + pltpu.roll(b, n - 1, 0))
    o_ref[0, 0] = _dot(hid.astype(BF16), w2_ref[0]).astype(o_ref.dtype)


def _compress(chunks, pe, w1, w2d):
    b, _, n, width = chunks.shape
    return pl.pallas_call(
        _compress_kernel,
        grid=(b, 4),
        in_specs=[pl.BlockSpec((1, 1, n, width), lambda bi, j: (bi, j, 0, 0)),
                  pl.BlockSpec((1, 2, width), lambda bi, j: (j // 2, 0, 0)),
                  pl.BlockSpec((1, 2 * width, CMP_HIDDEN), lambda bi, j: (j // 2, 0, 0)),
                  pl.BlockSpec((1, CMP_HIDDEN, LANES), lambda bi, j: (j // 2, 0, 0))],
        out_specs=pl.BlockSpec((1, 1, n, LANES), lambda bi, j: (bi, j, 0, 0)),
        out_shape=jax.ShapeDtypeStruct((b, 4, n, LANES), BF16),
        compiler_params=_cparams(("arbitrary", "arbitrary")),
        name="nsa_compress",
    )(chunks, pe, w1, w2d)


def _cmp_kernel(q_ref, kc_ref, vc_ref, tb_ref, o_ref, sel_ref, pg_ref, sc_ref, selt_ref):
    t = T_ATT
    qi = pl.program_id(1)
    n_c = kc_ref.shape[2]
    n_sel = n_c // (SLC_BLOCK // CMP_STRIDE)
    lo = _lane_lo(t)
    kc = kc_ref[0, 0]
    vc = vc_ref[0, 0]
    pg = jnp.zeros((t, n_c), F32)
    for pair in range(2):
        cols = slice(pair * LANES, (pair + 1) * LANES)
        q = q_ref[0, :, cols]
        o_half = []
        for half in range(2):
            keep = lo if half == 0 else jnp.logical_not(lo)
            qm = jnp.where(keep, q * (C_DIM ** -0.5), jnp.zeros_like(q))
            s = _dot_nt(qm, kc) + tb_ref[2 * pair + half]
            m = jnp.maximum(jnp.max(s, axis=-1, keepdims=True), M_INIT)
            p = jnp.exp(s - m)
            den = jnp.sum(p, axis=-1, keepdims=True)
            p = p / jnp.where(den > 0, den, 1.0)
            o_half.append(_dot(p.astype(BF16), vc))
            pg = pg + p
        o_ref[0, :, cols] = jnp.where(lo, o_half[0], o_half[1])
    pad = 8
    pg_t = pg.T
    for c in range(t // LANES):
        pg_ref[c, 0:pad, :] = jnp.zeros((pad, LANES), F32)
        pg_ref[c, pad:pad + n_c, :] = pg_t[:, c * LANES:(c + 1) * LANES]
        pg_ref[c, pad + n_c:, :] = jnp.zeros((pad, LANES), F32)
    r = SLC_BLOCK // CMP_STRIDE
    tap = lambda k: jnp.concatenate(
        [pg_ref[c, pl.ds(pad + k, n_sel, stride=r), :] for c in range(t // LANES)], axis=1)
    score = (0.5 * tap(-1) + ((tap(0) + tap(1)) + tap(2))) + 0.5 * tap(3)
    jb = lax.broadcasted_iota(jnp.int32, (n_sel, t), 0)
    qblk = jnp.right_shift(qi * t + lax.broadcasted_iota(jnp.int32, (n_sel, t), 1), SLC_SHIFT)
    forced = (jb == 0) | (jb == qblk) | (jb == qblk - 1)
    sc = jnp.where(forced, jnp.inf, jnp.where(jb <= qblk, score, NEG_INF))
    sc_ref[...] = sc
    cnt = jnp.zeros((n_sel, t), jnp.int32)
    for i in range(n_sel):
        row = sc_ref[pl.ds(i, 1), :]
        beats = (row > sc) | ((row == sc) & (i < jb))
        cnt = cnt + beats.astype(jnp.int32)
    selt_ref[...] = jnp.zeros(selt_ref.shape, F32)
    selt_ref[0:n_sel, :] = (cnt < SLC_TOP_N).astype(F32)
    sel_ref[0, 0] = selt_ref[...].T.astype(sel_ref.dtype)


def _cmp_attention(hb3, kvc, tb):
    b, s, _ = hb3.shape
    t = T_ATT
    n_c = s // CMP_STRIDE
    n_sel = s // SLC_BLOCK
    assert n_sel <= LANES
    gw = (C_HEADS // C_KV_GROUPS) * C_DIM
    return pl.pallas_call(
        _cmp_kernel,
        grid=(C_KV_GROUPS, s // t, b),
        in_specs=[pl.BlockSpec((1, t, gw), lambda g, qi, bi: (bi, qi, g)),
                  pl.BlockSpec((1, 1, n_c, LANES), lambda g, qi, bi: (bi, g, 0, 0)),
                  pl.BlockSpec((1, 1, n_c, LANES), lambda g, qi, bi: (bi, 2 + g, 0, 0)),
                  pl.BlockSpec((4, t, n_c), lambda g, qi, bi: (g, qi, 0))],
        out_specs=[pl.BlockSpec((1, t, gw), lambda g, qi, bi: (bi, qi, g)),
                   pl.BlockSpec((1, 1, t, LANES), lambda g, qi, bi: (bi, g, qi, 0))],
        out_shape=[jax.ShapeDtypeStruct((b, s, C_HEADS * C_DIM), F32),
                   jax.ShapeDtypeStruct((b, C_KV_GROUPS, s, LANES), BF16)],
        scratch_shapes=[pltpu.VMEM((t // LANES, n_c + 16, LANES), F32), pltpu.VMEM((n_sel, t), F32),
                        pltpu.VMEM((LANES, t), F32)],
        compiler_params=_cparams(("arbitrary", "arbitrary", "arbitrary")),
        name="nsa_cmp_attention",
    )(hb3, kvc, kvc, tb)


def _nsa_kernel(q_ref, k_ref, v_ref, tb_ref, *rest, n_prev, selected):
    if selected:
        sel_ref, o_ref = rest
    else:
        (o_ref,) = rest
    t = T_ATT
    qi = pl.program_id(2)
    lo = _lane_lo(t)
    full = slice(0, LANES)
    q = q_ref[0]
    if selected:
        sel = sel_ref[0, 0]
        blk_row = lax.broadcasted_iota(jnp.int32, (LANES, t), 0)
        blk_col = jnp.right_shift(lax.broadcasted_iota(jnp.int32, (LANES, t), 1), SLC_SHIFT)
    o_half = []
    for half in range(2):
        keep = lo if half == 0 else jnp.logical_not(lo)
        qm = jnp.where(keep, q * (C_DIM ** -0.5), jnp.zeros_like(q))

        def tile_fn(j, s, half=half):
            s = s + tb_ref[half, qi - j]
            if selected:
                expand = jnp.where(blk_row == blk_col + j * (t // SLC_BLOCK), 1.0, 0.0).astype(BF16)
                s = jnp.where(_dot(sel, expand) > 0.5, s, NEG_INF)
            return s

        j_lo = 0 if n_prev is None else jnp.maximum(qi - n_prev, 0)
        o_half.append(_flash(qm, k_ref, full, v_ref, full, j_lo, qi + 1, tile_fn, t))
    o_ref[0] = jnp.where(lo, o_half[0], o_half[1])


def _nsa_attention(hb3, tb, k_blk, v_blk, n_prev, sel=None):
    b, s, _ = hb3.shape
    t = T_ATT
    n_d = tb.shape[1]
    in_specs = [pl.BlockSpec((1, t, LANES), lambda hp, bi, qi: (bi, qi, hp)),
                pl.BlockSpec((1, s, LANES), lambda hp, bi, qi: (bi, 0, k_blk + hp // 2)),
                pl.BlockSpec((1, s, LANES), lambda hp, bi, qi: (bi, 0, v_blk + hp // 2)),
                pl.BlockSpec((2, n_d, t, t), lambda hp, bi, qi: (hp, 0, 0, 0))]
    args = [hb3, hb3, hb3, tb]
    if sel is not None:
        in_specs.append(pl.BlockSpec((1, 1, t, LANES), lambda hp, bi, qi: (bi, hp // 2, qi, 0)))
        args.append(sel)
    return pl.pallas_call(
        functools.partial(_nsa_kernel, n_prev=n_prev, selected=sel is not None),
        grid=(C_HEADS // 2, b, s // t),
        in_specs=in_specs,
        out_specs=pl.BlockSpec((1, t, LANES), lambda hp, bi, qi: (bi, qi, hp)),
        out_shape=jax.ShapeDtypeStruct((b, s, C_HEADS * C_DIM), F32),
        compiler_params=_cparams(("arbitrary", "arbitrary", "arbitrary")),
        name="nsa_selected" if sel is not None else "nsa_window",
    )(*args)


def _rms(x, g):
    return x * lax.rsqrt(jnp.mean(x * x, axis=-1, keepdims=True) + RMS_EPS) * g


def _mla_prep_kernel(cq_ref, ckv_ref, kr1_ref, kr2_ref, cos_ref, sin_ref, gq_ref, gkv_ref,
                     wq1_ref, wq2_ref, wk_ref, wv_ref, q_ref, k_ref, v_ref):
    cqn = _rms(cq_ref[...], gq_ref[...]).astype(BF16)
    c = _rms(ckv_ref[...], gkv_ref[...]).astype(BF16)
    cos, sin = cos_ref[...], sin_ref[...]
    k_rope = kr1_ref[...] * cos + kr2_ref[...] * sin
    for h in range(D_HEADS):
        cols = slice(h * LANES, (h + 1) * LANES)
        q_ref[:, cols] = (_dot(cqn, wq1_ref[:, cols]) * cos + _dot(cqn, wq2_ref[:, cols]) * sin).astype(q_ref.dtype)
        k_ref[:, cols] = (_dot(c, wk_ref[:, cols]) + k_rope).astype(k_ref.dtype)
    v_ref[...] = _dot(c, wv_ref[...]).astype(v_ref.dtype)


def _mla_prep(hf, cos, sin, gq, gkv, wq1, wq2, wk, wv, seq):
    m = hf.shape[0]
    n_seq = seq // TM
    const = lambda shape: pl.BlockSpec(shape, lambda i: (0, 0))
    hq = D_HEADS * LANES
    return pl.pallas_call(
        _mla_prep_kernel,
        grid=(m // TM,),
        in_specs=[pl.BlockSpec((TM, D_Q_LORA), lambda i: (i, 1)),
                  pl.BlockSpec((TM, D_KV_LORA), lambda i: (i, 3)),
                  pl.BlockSpec((TM, LANES), lambda i: (i, 8)),
                  pl.BlockSpec((TM, LANES), lambda i: (i, 9)),
                  pl.BlockSpec((TM, LANES), lambda i: (i % n_seq, 0)),
                  pl.BlockSpec((TM, LANES), lambda i: (i % n_seq, 0)),
                  const((1, D_Q_LORA)), const((1, D_KV_LORA)),
                  const(wq1.shape), const(wq2.shape), const(wk.shape), const(wv.shape)],
        out_specs=[pl.BlockSpec((TM, hq), lambda i: (i, 0)), pl.BlockSpec((TM, hq), lambda i: (i, 0)),
                   pl.BlockSpec((TM, D_HEADS * D_V), lambda i: (i, 0))],
        out_shape=[jax.ShapeDtypeStruct((m, hq), BF16), jax.ShapeDtypeStruct((m, hq), BF16),
                   jax.ShapeDtypeStruct((m, D_HEADS * D_V), BF16)],
        compiler_params=_cparams(("arbitrary",)),
        name="mla_prep",
    )(hf, hf, hf, hf, cos, sin, gq.reshape(1, -1), gkv.reshape(1, -1), wq1, wq2, wk, wv)


def _mla_kernel(q_ref, k_ref, v_ref, o_ref):
    t = T_ATT
    qi = pl.program_id(2)
    lo = _lane_lo(t)
    scale = (D_NOPE + D_ROPE) ** -0.5
    causal = (lax.broadcasted_iota(jnp.int32, (t, t), 0) >= lax.broadcasted_iota(jnp.int32, (t, t), 1))
    o_half = []
    for half in range(2):
        cols = slice(half * LANES, (half + 1) * LANES)

        def tile_fn(j, s):
            s = s * scale
            return jnp.where((j < qi) | causal, s, NEG_INF)

        o_half.append(_flash(q_ref[0, :, cols], k_ref, cols, v_ref, slice(0, LANES), 0, qi + 1, tile_fn, t))
    o_ref[0] = jnp.where(lo, o_half[0], o_half[1]).astype(o_ref.dtype)


def _mla_attention(q3, k3, v3):
    b, s, _ = q3.shape
    t = T_ATT
    return pl.pallas_call(
        _mla_kernel,
        grid=(D_HEADS // 2, b, s // t),
        in_specs=[pl.BlockSpec((1, t, 2 * LANES), lambda hp, bi, qi: (bi, qi, hp)),
                  pl.BlockSpec((1, s, 2 * LANES), lambda hp, bi, qi: (bi, 0, hp)),
                  pl.BlockSpec((1, s, LANES), lambda hp, bi, qi: (bi, 0, hp))],
        out_specs=pl.BlockSpec((1, t, LANES), lambda hp, bi, qi: (bi, qi, hp)),
        out_shape=jax.ShapeDtypeStruct((b, s, D_HEADS * D_V), BF16),
        compiler_params=_cparams(("arbitrary", "arbitrary", "arbitrary")),
        name="mla_attention",
    )(q3, k3, v3)


def _odd_out_kernel(gate_ref, oc_ref, os_ref, ow_ref, od_ref, x_ref, w_ref, g_ref, b_ref, out_ref):
    gates = 1.0 / (1.0 + jnp.exp(-gate_ref[...]))
    lo = _lane_lo(TM)
    n_c = C_HEADS * C_DIM
    y = _dot(od_ref[...], w_ref[n_c:, :])
    for blk in range(n_c // LANES):
        cols = slice(blk * LANES, (blk + 1) * LANES)
        acc = jnp.zeros((TM, LANES), F32)
        for br, o_ref in enumerate((oc_ref, os_ref, ow_ref)):
            c0 = br * C_HEADS + 2 * blk
            gexp = jnp.where(lo, gates[:, c0:c0 + 1], gates[:, c0 + 1:c0 + 2])
            acc = acc + gexp * o_ref[:, cols]
        y = y + _dot(acc.astype(BF16), w_ref[cols, :])
    out_ref[...] = _residual_ln(x_ref[...], y, g_ref[...], b_ref[...])


def _odd_out(hf, o_cmp, o_slc, o_win, o_d, x, w_out, g, beta):
    m = x.shape[0]
    n_c = C_HEADS * C_DIM
    row = lambda width: pl.BlockSpec((TM, width), lambda i: (i, 0))
    const = lambda shape: pl.BlockSpec(shape, lambda i: (0, 0))
    return pl.pallas_call(
        _odd_out_kernel,
        grid=(m // TM,),
        in_specs=[pl.BlockSpec((TM, LANES), lambda i: (i, 2)),
                  row(n_c), row(n_c), row(n_c), row(D_HEADS * D_V), row(D_MODEL),
                  const(w_out.shape), const((1, D_MODEL)), const((1, D_MODEL))],
        out_specs=row(D_MODEL),
        out_shape=jax.ShapeDtypeStruct((m, D_MODEL), F32),
        compiler_params=_cparams(("arbitrary",)),
        name="odd_out",
    )(hf, o_cmp, o_slc, o_win, o_d, x, w_out, g.reshape(1, -1), beta.reshape(1, -1))


ODD_BF16_W = 1536
ODD_F32_W = 1280


def _odd_in_columns():
    q_c = C_HEADS * C_DIM
    kv = lambda br, which, g: q_c + ((br * 2 + which) * C_KV_GROUPS + g) * C_DIM + np.arange(C_DIM)
    gate0 = q_c + 3 * 2 * C_KV_GROUPS * C_DIM
    cq0 = gate0 + 3 * C_HEADS
    ckv0 = cq0 + D_Q_LORA
    kr0 = ckv0 + D_KV_LORA
    zeros = lambda n: np.full(n, -1)
    cols = [np.arange(q_c)]
    for br in (1, 2):
        for which in (0, 1):
            for g in range(C_KV_GROUPS):
                cols += [kv(br, which, g), kv(br, which, g)]
    assert sum(len(c) for c in cols) == ODD_BF16_W
    for which in (0, 1):
        cols += [kv(0, which, 0), kv(0, which, 1)]
    cols += [gate0 + np.arange(3 * C_HEADS), zeros(LANES - 3 * C_HEADS)]
    cols += [cq0 + np.arange(D_Q_LORA), ckv0 + np.arange(D_KV_LORA)]
    half = D_ROPE // 2
    kr = kr0 + np.arange(D_ROPE)
    cols += [zeros(D_NOPE), kr, zeros(LANES - D_NOPE - D_ROPE)]
    cols += [zeros(D_NOPE), kr[half:], kr[:half], zeros(LANES - D_NOPE - D_ROPE)]
    cols = np.concatenate(cols)
    assert len(cols) == ODD_BF16_W + ODD_F32_W
    return cols


def _gather_cols(w, cols):
    return jnp.where(jnp.asarray(cols >= 0)[None, :], w[:, np.maximum(cols, 0)], 0.0)


def _mla_weight_columns():
    dq = D_NOPE + D_ROPE
    half = D_ROPE // 2
    zeros = lambda n: np.full(n, -1)
    q1, q2, k1 = [], [], []
    for h in range(D_HEADS):
        rope0 = h * dq + D_NOPE
        q1 += [h * dq + np.arange(dq), zeros(LANES - dq)]
        q2 += [zeros(D_NOPE), rope0 + half + np.arange(half), rope0 + np.arange(half), zeros(LANES - dq)]
        k1 += [h * D_NOPE + np.arange(D_NOPE), zeros(LANES - D_NOPE)]
    return np.concatenate(q1), np.concatenate(q2), np.concatenate(k1)


def _rope_tables(seq):
    half = D_ROPE // 2
    inv = ROPE_THETA ** (-jnp.arange(half, dtype=F32) / half)
    ang = jnp.arange(seq).astype(F32)[:, None] * inv
    cos, sin = jnp.cos(ang), jnp.sin(ang)
    pad = jnp.zeros((seq, LANES - D_NOPE - D_ROPE), F32)
    cos_t = jnp.concatenate([jnp.ones((seq, D_NOPE), F32), cos, cos, pad], axis=1)
    sin_t = jnp.concatenate([jnp.zeros((seq, D_NOPE), F32), -sin, sin, pad], axis=1)
    return cos_t, sin_t


def kernel(x, rel_bias, ev_w_in, ev_w_out, ev_lambda, ev_subln, od_w_in, od_w_out, od_cmp_pe, od_cmp_w1, od_cmp_w2, od_q_norm, od_kv_norm, od_w_uq, od_w_uk, od_w_uv, ffn_w_up, ffn_conv_w, ffn_conv_b, ffn_w_down, ln_g, ln_b):
    b, s, _ = x.shape
    m = b * s
    assert s % 1024 == 0 and s // B_PAIRS[-1][1] >= 2 * T_DIL and s // SLC_BLOCK <= LANES
    n_d = s // T_ATT

    tb_t = rel_bias.T.astype(F32)
    tb_causal = _bias_table(tb_t[:C_HEADS], _toeplitz_idx(n_d, T_ATT, s))
    tb_win = _bias_table(tb_t[:C_HEADS], _toeplitz_idx(-(-WIN_SIZE // T_ATT) + 1, T_ATT, WIN_SIZE - 1))
    tb_cmp = _bias_table(tb_t[:C_HEADS], _cmp_idx(s)).reshape(C_HEADS, s, s // CMP_STRIDE)
    tb_dil = [_bias_table(tb_t[A_HEADS + i * B_HEADS:A_HEADS + (i + 1) * B_HEADS], _dilated_idx(d))
              for i, (_, d) in enumerate(B_PAIRS)]
    n_prev_win = -(-(WIN_SIZE - 1) // T_ATT)
    cos_t, sin_t = _rope_tables(s)
    odd_cols = _odd_in_columns()
    q1_cols, q2_cols, k1_cols = _mla_weight_columns()

    xf = x.reshape(m, D_MODEL)
    for l in range(DEPTH):
        i = l // 2
        if l % 2 == 0:
            (h,) = _inproj(xf, ev_w_in[i].astype(BF16), (EVEN_IN,), (BF16,), 768)
            h3 = h.reshape(b, s, EVEN_IN)
            lam_init = 0.8 - 0.6 * math.exp(-0.3 * l)
            o_a = _diff_attention(h3, tb_causal, ev_lambda[i].astype(F32), ev_subln[i], lam_init)
            obs, lses = [], []
            for p_idx, (_, d) in enumerate(B_PAIRS):
                o, lse = _dilated_attention(h3, tb_dil[p_idx], p_idx, d)
                obs.append(o)
                lses.append(lse)
            xf = _even_out(o_a.reshape(m, -1), obs, lses, xf, ev_w_out[i].astype(BF16), ln_g[l, 0], ln_b[l, 0])
        else:
            w_in = _gather_cols(od_w_in[i], odd_cols).astype(BF16)
            hb, hf = _inproj(xf, w_in, (ODD_BF16_W, ODD_F32_W), (BF16, F32), 256)
            hb3 = hb.reshape(b, s, ODD_BF16_W)
            n_ch = s // CMP_STRIDE
            chunks = (hf[:, :4 * C_DIM].reshape(b, n_ch, CMP_STRIDE, 4, C_DIM)
                      .transpose(0, 3, 1, 2, 4).reshape(b, 4, n_ch, CMP_STRIDE * C_DIM))
            pe = od_cmp_pe[i].reshape(2, 2, CMP_STRIDE * C_DIM)
            w2d = jnp.concatenate([od_cmp_w2[i], od_cmp_w2[i]], axis=-1).astype(BF16)
            kvc = _compress(chunks, pe, od_cmp_w1[i].astype(BF16), w2d)
            o_cmp, sel = _cmp_attention(hb3, kvc, tb_cmp)
            o_slc = _nsa_attention(hb3, tb_causal, 4, 6, None, sel)
            o_win = _nsa_attention(hb3, tb_win, 8, 10, n_prev_win)
            q_d, k_d, v_d = _mla_prep(hf, cos_t, sin_t, od_q_norm[i], od_kv_norm[i],
                                      _gather_cols(od_w_uq[i], q1_cols).astype(BF16),
                                      _gather_cols(od_w_uq[i], q2_cols).astype(BF16),
                                      _gather_cols(od_w_uk[i], k1_cols).astype(BF16),
                                      od_w_uv[i].astype(BF16), s)
            o_d = _mla_attention(q_d.reshape(b, s, -1), k_d.reshape(b, s, -1), v_d.reshape(b, s, -1))
            xf = _odd_out(hf, o_cmp.reshape(m, -1), o_slc.reshape(m, -1), o_win.reshape(m, -1),
                          o_d.reshape(m, -1), xf, od_w_out[i].astype(BF16), ln_g[l, 0], ln_b[l, 0])
        xf = _ffn(xf, ffn_w_up[l].astype(BF16), ffn_conv_w[l], ffn_conv_b[l], ffn_w_down[l].astype(BF16),
                  ln_g[l, 1], ln_b[l, 1], s)
    return xf.reshape(b, s, D_MODEL)
```

```python
import functools
import math

import numpy as np
import jax
import jax.numpy as jnp
from jax import lax
from jax.experimental import pallas as pl
from jax.experimental.pallas import tpu as pltpu

F32 = jnp.float32
BF16 = jnp.bfloat16

D_MODEL = 1024
DEPTH = 4
LN_EPS = 1e-5
RMS_EPS = 1e-5
N_BUCKETS = 32
MAX_DISTANCE = 2048
A_HEADS, A_QK, A_V = 4, 64, 128
B_PAIRS = ((128, 1), (512, 4), (2048, 16))
B_HEADS, B_DIM = 4, 64
C_HEADS, C_KV_GROUPS, C_DIM = 8, 2, 64
CMP_LEN, CMP_STRIDE, CMP_HIDDEN = 32, 16, 256
SLC_BLOCK, SLC_TOP_N = 64, 16
SLC_SHIFT = 6
WIN_SIZE = 512
D_HEADS, D_Q_LORA, D_KV_LORA, D_NOPE, D_ROPE, D_V = 8, 384, 256, 64, 32, 64
ROPE_THETA = 10000.0
D_FF = 2816
EVEN_IN = 2 * A_HEADS * A_QK * 2 + A_HEADS * A_V + len(B_PAIRS) * 3 * B_HEADS * B_DIM
ALPHA = (2 * DEPTH) ** 0.25

LANES = 128
VMEM_LIMIT = 56 * 1024 * 1024
T_ATT = 256
T_DIL = 128
TM = 512
FF_CHUNK = 256
NEG_INF = float("-inf")
M_INIT = -1e30


def _cparams(sem):
    return pltpu.CompilerParams(dimension_semantics=sem, vmem_limit_bytes=VMEM_LIMIT)


def _dot(a, b):
    return jnp.dot(a, b, preferred_element_type=F32)


def _dot_nt(a, b):
    return lax.dot_general(a, b, (((1,), (1,)), ((), ())), preferred_element_type=F32)


def _bucket_np(dist):
    n = np.maximum(dist, 0)
    nf = np.maximum(n, 1).astype(np.float32)
    max_exact = N_BUCKETS // 2
    large = max_exact + (np.log(nf / max_exact) / math.log(MAX_DISTANCE / max_exact)
                         * (N_BUCKETS - max_exact)).astype(np.int32)
    return np.where(n < max_exact, n, np.minimum(large, N_BUCKETS - 1)).astype(np.int32)


def _toeplitz_idx(n_masked, n_delta, t, max_dist):
    r = np.arange(t)[:, None]
    c = np.arange(t)[None, :]
    out = []
    for delta in range(-n_masked, n_delta):
        dist = t * delta + r - c
        ok = (dist >= 0) & (dist <= max_dist)
        out.append(np.where(ok, _bucket_np(dist), -1))
    return np.stack(out).astype(np.int32)


def _dilated_idx(d):
    t = T_DIL
    r = np.arange(t)[:, None]
    c = np.arange(2 * t)[None, :]
    dist0 = r - c
    ok0 = (dist0 >= 0) & (c < t)
    dist1 = t + r - c
    ok1 = (dist1 >= 0) & (dist1 <= t)
    return np.stack([np.where(ok0, _bucket_np(dist0 * d), -1),
                     np.where(ok1, _bucket_np(dist1 * d), -1)]).astype(np.int32)


def _cmp_idx(seq):
    ncb = seq // CMP_STRIDE - 1
    q = np.arange(seq)[:, None]
    c = np.arange(seq // CMP_STRIDE)[None, :]
    dist = q - (c * CMP_STRIDE + CMP_LEN - 1)
    ok = (dist >= 0) & (c < ncb)
    return np.where(ok, _bucket_np(dist), -1).astype(np.int32).reshape(seq // T_ATT, T_ATT, seq // CMP_STRIDE)


def _bias_kernel(tbl_ref, idx_ref, o_ref):
    h = pl.program_id(0)
    idx = idx_ref[0]
    out = jnp.full(idx.shape, NEG_INF, F32)
    for b in range(N_BUCKETS):
        out = jnp.where(idx == b, tbl_ref[h, b], out)
    o_ref[0, 0] = out


def _bias_table(tbl, idx):
    n_h = tbl.shape[0]
    n, r, c = idx.shape
    return pl.pallas_call(
        _bias_kernel,
        grid=(n_h, n),
        in_specs=[pl.BlockSpec(memory_space=pltpu.SMEM),
                  pl.BlockSpec((1, r, c), lambda h, i: (i, 0, 0))],
        out_specs=pl.BlockSpec((1, 1, r, c), lambda h, i: (h, i, 0, 0)),
        out_shape=jax.ShapeDtypeStruct((n_h, n, r, c), F32),
        compiler_params=_cparams(("arbitrary", "arbitrary")),
        name="bias_table",
    )(tbl, jnp.asarray(idx))


def _inproj_kernel(x_ref, w_ref, *o_refs, widths, chunk):
    xb = x_ref[...].astype(BF16)
    off = 0
    for o_ref, width in zip(o_refs, widths):
        for c0 in range(0, width, chunk):
            o_ref[:, c0:c0 + chunk] = _dot(xb, w_ref[:, off + c0:off + c0 + chunk]).astype(o_ref.dtype)
        off += width


def _inproj(x, w, widths, dtypes, chunk):
    m = x.shape[0]
    n = w.shape[1]
    assert sum(widths) == n and all(wd % chunk == 0 for wd in widths)
    return pl.pallas_call(
        functools.partial(_inproj_kernel, widths=widths, chunk=chunk),
        grid=(m // TM,),
        in_specs=[pl.BlockSpec((TM, D_MODEL), lambda i: (i, 0)),
                  pl.BlockSpec((D_MODEL, n), lambda i: (0, 0))],
        out_specs=[pl.BlockSpec((TM, wd), lambda i: (i, 0)) for wd in widths],
        out_shape=[jax.ShapeDtypeStruct((m, wd), dt) for wd, dt in zip(widths, dtypes)],
        compiler_params=_cparams(("arbitrary",)),
        name="in_proj",
    )(x, w)


def _flash_init(streams):
    out = []
    for q, _, _, _, v_cols in streams:
        t_q = q.shape[0]
        out.append((jnp.full((t_q, 1), M_INIT, F32), jnp.zeros((t_q, 1), F32),
                    jnp.zeros((t_q, v_cols.stop - v_cols.start), F32)))
    return tuple(out)


def _flash_step(streams, score_fns, row_start, n_rows, i, carry):
    rows = pl.ds(pl.multiple_of(row_start, T_ATT), n_rows)
    out = []
    for (q, k_ref, k_cols, v_ref, v_cols), fn, (m, l, acc) in zip(streams, score_fns, carry):
        s = fn(i, _dot_nt(q, k_ref[0, rows, k_cols]))
        m_new = jnp.maximum(m, jnp.max(s, axis=-1, keepdims=True))
        alpha = jnp.exp(m - m_new)
        p = jnp.exp(s - m_new)
        l = alpha * l + jnp.sum(p, axis=-1, keepdims=True)
        acc = alpha * acc + _dot(p.astype(BF16), v_ref[0, rows, v_cols])
        out.append((m_new, l, acc))
    return tuple(out)


def _flash_causal(streams, score_fns, n_main, tail_fns=None):
    kt = 2 * T_ATT
    carry = lax.fori_loop(0, n_main, lambda i, c: _flash_step(streams, score_fns, i * kt, kt, i, c),
                          _flash_init(streams))
    if tail_fns is not None:
        carry = _flash_step(streams, tail_fns, n_main * kt, kt, n_main, carry)
    return [acc / l for _, l, acc in carry]


def _pair_bias(tb_ref, h, qi):
    t = T_ATT

    def fn(i, s):
        d0 = qi - 2 * i
        return jnp.concatenate([s[:, :t] + tb_ref[h, d0 + 1], s[:, t:] + tb_ref[h, d0]], axis=1)

    return fn


def _lane_lo(rows):
    return lax.broadcasted_iota(jnp.int32, (rows, LANES), 1) < (LANES // 2)


def _diff_kernel(lam_ref, g_ref, q0_ref, q1_ref, k0_ref, k1_ref, v_ref, tb_ref, o_ref, *, lam_init):
    qi = pl.program_id(2)
    lp = lam_ref[...]
    lam = (jnp.exp(jnp.sum(lp[0:1] * lp[1:2], axis=-1, keepdims=True))
           - jnp.exp(jnp.sum(lp[2:3] * lp[3:4], axis=-1, keepdims=True)) + lam_init)
    lo = _lane_lo(T_ATT)
    full = slice(0, LANES)
    for half in range(2):
        keep = lo if half == 0 else jnp.logical_not(lo)
        v_cols = slice(half * A_V, (half + 1) * A_V)
        streams = []
        for q_ref, k_ref in ((q0_ref, k0_ref), (q1_ref, k1_ref)):
            q = q_ref[0]
            qm = jnp.where(keep, q * (A_QK ** -0.5), jnp.zeros_like(q))
            streams.append((qm, k_ref, full, v_ref, v_cols))
        outs = _flash_causal(streams, [_pair_bias(tb_ref, half, qi)] * 2, qi // 2 + 1)
        d = outs[0] - lam * outs[1]
        ms = jnp.mean(d * d, axis=-1, keepdims=True)
        o_ref[0, :, v_cols] = (d * lax.rsqrt(ms + RMS_EPS) * g_ref[...] * (1.0 - lam_init)).astype(o_ref.dtype)


def _diff_attention(h3, tb, lam_p, subln_g, lam_init):
    b, s, _ = h3.shape
    t = T_ATT
    n_d = s // t
    return pl.pallas_call(
        functools.partial(_diff_kernel, lam_init=lam_init),
        grid=(2, b, s // t),
        in_specs=[pl.BlockSpec((4, A_QK), lambda hp, bi, qi: (0, 0)),
                  pl.BlockSpec((1, A_V), lambda hp, bi, qi: (0, 0)),
                  pl.BlockSpec((1, t, LANES), lambda hp, bi, qi: (bi, qi, hp)),
                  pl.BlockSpec((1, t, LANES), lambda hp, bi, qi: (bi, qi, 2 + hp)),
                  pl.BlockSpec((1, s, LANES), lambda hp, bi, qi: (bi, 0, 4 + hp)),
                  pl.BlockSpec((1, s, LANES), lambda hp, bi, qi: (bi, 0, 6 + hp)),
                  pl.BlockSpec((1, s, 2 * A_V), lambda hp, bi, qi: (bi, 0, 4 + hp)),
                  pl.BlockSpec((2, n_d + 1, t, t), lambda hp, bi, qi: (hp, 0, 0, 0))],
        out_specs=pl.BlockSpec((1, t, 2 * A_V), lambda hp, bi, qi: (bi, qi, hp)),
        out_shape=jax.ShapeDtypeStruct((b, s, A_HEADS * A_V), BF16),
        compiler_params=_cparams(("arbitrary", "arbitrary", "arbitrary")),
        name="diff_attention",
    )(lam_p, subln_g.reshape(1, A_V), h3, h3, h3, h3, h3, tb)


def _dilated_kernel(q_ref, k_ref, v_ref, tb_ref, o_ref, lse_ref, *, rows_per_step):
    ci = pl.program_id(2)
    t = T_DIL
    lo = _lane_lo(t)
    scale = B_DIM ** -0.5

    def block(bi_local, carry):
        bi = ci * (rows_per_step // t) + bi_local
        var = jnp.minimum(bi, 1)
        krows = pl.ds(pl.multiple_of(jnp.maximum(bi - 1, 0) * t, t), 2 * t)
        qrows = pl.ds(pl.multiple_of(bi_local * t, t), t)
        for pair in range(2):
            cols = slice(pair * LANES, (pair + 1) * LANES)
            q = q_ref[0, qrows, cols]
            kk = k_ref[0, krows, cols]
            vv = v_ref[0, krows, cols]
            o_half, l_half = [], []
            for half in range(2):
                keep = lo if half == 0 else jnp.logical_not(lo)
                qm = jnp.where(keep, q * scale, jnp.zeros_like(q))
                s = _dot_nt(qm, kk) + tb_ref[2 * pair + half, var]
                m = jnp.max(s, axis=-1, keepdims=True)
                p = jnp.exp(s - m)
                l = jnp.sum(p, axis=-1, keepdims=True)
                o_half.append(_dot(p.astype(BF16), vv) / l)
                l_half.append(m + jnp.log(l))
            o_ref[0, qrows, cols] = jnp.where(lo, o_half[0], o_half[1])
            lse_ref[0, qrows, cols] = jnp.where(lo, l_half[0], l_half[1])
        return carry

    lax.fori_loop(0, rows_per_step // t, block, 0)


def _dilated_attention(h3, tb, pair_idx, d):
    b, s, n_cols = h3.shape
    l_res = s // d
    hd = B_HEADS * B_DIM
    rows = min(l_res, 1024)
    hv = h3.reshape(b, l_res, d * n_cols)
    cpb = n_cols // hd
    base = (2 * A_HEADS * A_QK * 2 + A_HEADS * A_V) // hd + 3 * pair_idx
    o, lse = pl.pallas_call(
        functools.partial(_dilated_kernel, rows_per_step=rows),
        grid=(b, d, l_res // rows),
        in_specs=[pl.BlockSpec((1, rows, hd), lambda bi, r, c: (bi, c, r * cpb + base)),
                  pl.BlockSpec((1, l_res, hd), lambda bi, r, c: (bi, 0, r * cpb + base + 1)),
                  pl.BlockSpec((1, l_res, hd), lambda bi, r, c: (bi, 0, r * cpb + base + 2)),
                  pl.BlockSpec((B_HEADS, 2, T_DIL, 2 * T_DIL), lambda bi, r, c: (0, 0, 0, 0))],
        out_specs=[pl.BlockSpec((1, rows, hd), lambda bi, r, c: (bi, c, r)),
                   pl.BlockSpec((1, rows, hd), lambda bi, r, c: (bi, c, r))],
        out_shape=[jax.ShapeDtypeStruct((b, l_res, d * hd), F32)] * 2,
        compiler_params=_cparams(("arbitrary", "arbitrary", "arbitrary")),
        name="dilated_attention",
    )(hv, hv, hv, tb)
    return o.reshape(b * s, hd), lse.reshape(b * s, hd)


def _residual_ln(x, y, g, beta):
    z = ALPHA * x + y
    mu = jnp.mean(z, axis=-1, keepdims=True)
    zc = z - mu
    var = jnp.mean(zc * zc, axis=-1, keepdims=True)
    return zc * lax.rsqrt(var + LN_EPS) * g + beta


def _even_out_kernel(oa_ref, o0_ref, o1_ref, o2_ref, l0_ref, l1_ref, l2_ref, x_ref, w_ref, g_ref, b_ref, out_ref):
    l0, l1, l2 = l0_ref[...], l1_ref[...], l2_ref[...]
    mx = jnp.maximum(jnp.maximum(l0, l1), l2)
    e0, e1, e2 = jnp.exp(l0 - mx), jnp.exp(l1 - mx), jnp.exp(l2 - mx)
    den = e0 + e1 + e2
    ob = (e0 / den) * o0_ref[...] + (e1 / den) * o1_ref[...] + (e2 / den) * o2_ref[...]
    n_a = A_HEADS * A_V
    y = _dot(oa_ref[...], w_ref[0:n_a, :]) + _dot(ob.astype(BF16), w_ref[n_a:, :])
    out_ref[...] = _residual_ln(x_ref[...], y, g_ref[...], b_ref[...])


def _even_out(oa, obs, lses, x, w_out, g, beta):
    m = x.shape[0]
    hd = B_HEADS * B_DIM
    row = lambda width: pl.BlockSpec((TM, width), lambda i: (i, 0))
    const = lambda shape: pl.BlockSpec(shape, lambda i: (0, 0))
    return pl.pallas_call(
        _even_out_kernel,
        grid=(m // TM,),
        in_specs=[row(A_HEADS * A_V)] + [row(hd)] * 6 + [row(D_MODEL), const(w_out.shape),
                                                        const((1, D_MODEL)), const((1, D_MODEL))],
        out_specs=row(D_MODEL),
        out_shape=jax.ShapeDtypeStruct((m, D_MODEL), F32),
        compiler_params=_cparams(("arbitrary",)),
        name="even_out",
    )(oa, *obs, *lses, x, w_out, g.reshape(1, -1), beta.reshape(1, -1))


def _gelu(x):
    return 0.5 * x * (1.0 + jnp.tanh(math.sqrt(2.0 / math.pi) * (x + 0.044715 * (x * x * x))))


def _ffn_kernel(x_ref, wu_ref, cw_ref, cb_ref, wd_ref, g_ref, b_ref, out_ref, acc_ref, gs_ref, tail_ref,
                *, tiles_per_seq):
    halo = 8

    @pl.when(pl.program_id(0) % tiles_per_seq == 0)
    def _():
        tail_ref[...] = jnp.zeros(tail_ref.shape, F32)

    x = x_ref[...]
    xb = x.astype(BF16)
    for c in range(D_FF // FF_CHUNK):
        cols = slice(c * FF_CHUNK, (c + 1) * FF_CHUNK)
        a = _dot(xb, wu_ref[:, cols])
        gate = _dot(xb, wu_ref[:, D_FF + c * FF_CHUNK:D_FF + (c + 1) * FF_CHUNK])
        gs_ref[0:halo, :] = tail_ref[:, cols]
        gs_ref[halo:, :] = gate
        tail_ref[:, cols] = gate[TM - halo:, :]
        conv = (gs_ref[pl.ds(halo - 2, TM), :] * cw_ref[0:1, cols]
                + gs_ref[pl.ds(halo - 1, TM), :] * cw_ref[1:2, cols]
                + gate * cw_ref[2:3, cols] + cb_ref[:, cols])
        u = (_gelu(conv) * a).astype(BF16)
        part = _dot(u, wd_ref[cols, :])
        if c == 0:
            acc_ref[...] = part
        else:
            acc_ref[...] += part
    out_ref[...] = _residual_ln(x, acc_ref[...], g_ref[...], b_ref[...])


def _ffn(x, w_up, conv_w, conv_b, w_down, g, beta, seq):
    m = x.shape[0]
    const = lambda shape: pl.BlockSpec(shape, lambda i: (0, 0))
    return pl.pallas_call(
        functools.partial(_ffn_kernel, tiles_per_seq=seq // TM),
        grid=(m // TM,),
        in_specs=[pl.BlockSpec((TM, D_MODEL), lambda i: (i, 0)), const(w_up.shape), const(conv_w.shape),
                  const((1, D_FF)), const(w_down.shape), const((1, D_MODEL)), const((1, D_MODEL))],
        out_specs=pl.BlockSpec((TM, D_MODEL), lambda i: (i, 0)),
        out_shape=jax.ShapeDtypeStruct((m, D_MODEL), F32),
        scratch_shapes=[pltpu.VMEM((TM, D_MODEL), F32), pltpu.VMEM((TM + 8, FF_CHUNK), F32),
                        pltpu.VMEM((8, D_FF), F32)],
        compiler_params=_cparams(("arbitrary",)),
        name="conv_ffn",
    )(x, w_up, conv_w, conv_b.reshape(1, -1), w_down, g.reshape(1, -1), beta.reshape(1, -1))


def _compress_kernel(ch_ref, pe_ref, w1_ref, w2_ref, o_ref):
    half = CMP_STRIDE * C_DIM
    ch = ch_ref[0, 0]
    a = _dot((ch + pe_ref[0, 0:1, :]).astype(BF16), w1_ref[0, 0:half, :])
    b = _dot((ch + pe_ref[0, 1:2, :]).astype(BF16), w1_ref[0, half:, :])
    n = ch.shape[0]
    hid = _gelu(a + pltpu.roll(b, n - 1, 0))
    o_ref[0, 0] = _dot(hid.astype(BF16), w2_ref[0]).astype(o_ref.dtype)


def _compress(chunks, pe, w1, w2d):
    b, _, n, width = chunks.shape
    return pl.pallas_call(
        _compress_kernel,
        grid=(b, 4),
        in_specs=[pl.BlockSpec((1, 1, n, width), lambda bi, j: (bi, j, 0, 0)),
                  pl.BlockSpec((1, 2, width), lambda bi, j: (j // 2, 0, 0)),
                  pl.BlockSpec((1, 2 * width, CMP_HIDDEN), lambda bi, j: (j // 2, 0, 0)),
                  pl.BlockSpec((1, CMP_HIDDEN, LANES), lambda bi, j: (j // 2, 0, 0))],
        out_specs=pl.BlockSpec((1, 1, n, LANES), lambda bi, j: (bi, j, 0, 0)),
        out_shape=jax.ShapeDtypeStruct((b, 4, n, LANES), BF16),
        compiler_params=_cparams(("arbitrary", "arbitrary")),
        name="nsa_compress",
    )(chunks, pe, w1, w2d)


def _cmp_kernel(q_ref, kc_ref, vc_ref, tb_ref, o_ref, sel_ref, pg_ref, sc_ref, selt_ref):
    t = T_ATT
    qi = pl.program_id(1)
    n_c = kc_ref.shape[2]
    n_sel = n_c // (SLC_BLOCK // CMP_STRIDE)
    lo = _lane_lo(t)
    kc = kc_ref[0, 0]
    vc = vc_ref[0, 0]
    pg = jnp.zeros((t, n_c), F32)
    for pair in range(2):
        cols = slice(pair * LANES, (pair + 1) * LANES)
        q = q_ref[0, :, cols]
        o_half = []
        for half in range(2):
            keep = lo if half == 0 else jnp.logical_not(lo)
            qm = jnp.where(keep, q * (C_DIM ** -0.5), jnp.zeros_like(q))
            s = _dot_nt(qm, kc) + tb_ref[2 * pair + half]
            m = jnp.maximum(jnp.max(s, axis=-1, keepdims=True), M_INIT)
            p = jnp.exp(s - m)
            den = jnp.sum(p, axis=-1, keepdims=True)
            p = p / jnp.where(den > 0, den, 1.0)
            o_half.append(_dot(p.astype(BF16), vc))
            pg = pg + p
        o_ref[0, :, cols] = jnp.where(lo, o_half[0], o_half[1])
    pad = 8
    pg_t = pg.T
    for c in range(t // LANES):
        pg_ref[c, 0:pad, :] = jnp.zeros((pad, LANES), F32)
        pg_ref[c, pad:pad + n_c, :] = pg_t[:, c * LANES:(c + 1) * LANES]
        pg_ref[c, pad + n_c:, :] = jnp.zeros((pad, LANES), F32)
    r = SLC_BLOCK // CMP_STRIDE
    tap = lambda k: jnp.concatenate(
        [pg_ref[c, pl.ds(pad + k, n_sel, stride=r), :] for c in range(t // LANES)], axis=1)
    score = (0.5 * tap(-1) + ((tap(0) + tap(1)) + tap(2))) + 0.5 * tap(3)
    jb = lax.broadcasted_iota(jnp.int32, (n_sel, t), 0)
    qblk = jnp.right_shift(qi * t + lax.broadcasted_iota(jnp.int32, (n_sel, t), 1), SLC_SHIFT)
    forced = (jb == 0) | (jb == qblk) | (jb == qblk - 1)
    sc = jnp.where(forced, jnp.inf, jnp.where(jb <= qblk, score, NEG_INF))
    sc_ref[...] = sc
    cnt = jnp.zeros((n_sel, t), jnp.int32)
    for i in range(n_sel):
        row = sc_ref[pl.ds(i, 1), :]
        beats = (row > sc) | ((row == sc) & (i < jb))
        cnt = cnt + beats.astype(jnp.int32)
    selt_ref[...] = jnp.zeros(selt_ref.shape, F32)
    selt_ref[0:n_sel, :] = (cnt < SLC_TOP_N).astype(F32)
    sel_ref[0, 0] = selt_ref[...].T.astype(sel_ref.dtype)


def _cmp_attention(hb3, kvc, tb):
    b, s, _ = hb3.shape
    t = T_ATT
    n_c = s // CMP_STRIDE
    n_sel = s // SLC_BLOCK
    assert n_sel <= LANES
    gw = (C_HEADS // C_KV_GROUPS) * C_DIM
    return pl.pallas_call(
        _cmp_kernel,
        grid=(C_KV_GROUPS, s // t, b),
        in_specs=[pl.BlockSpec((1, t, gw), lambda g, qi, bi: (bi, qi, g)),
                  pl.BlockSpec((1, 1, n_c, LANES), lambda g, qi, bi: (bi, g, 0, 0)),
                  pl.BlockSpec((1, 1, n_c, LANES), lambda g, qi, bi: (bi, 2 + g, 0, 0)),
                  pl.BlockSpec((4, t, n_c), lambda g, qi, bi: (g, qi, 0))],
        out_specs=[pl.BlockSpec((1, t, gw), lambda g, qi, bi: (bi, qi, g)),
                   pl.BlockSpec((1, 1, t, LANES), lambda g, qi, bi: (bi, g, qi, 0))],
        out_shape=[jax.ShapeDtypeStruct((b, s, C_HEADS * C_DIM), F32),
                   jax.ShapeDtypeStruct((b, C_KV_GROUPS, s, LANES), BF16)],
        scratch_shapes=[pltpu.VMEM((t // LANES, n_c + 16, LANES), F32), pltpu.VMEM((n_sel, t), F32),
                        pltpu.VMEM((LANES, t), F32)],
        compiler_params=_cparams(("arbitrary", "arbitrary", "arbitrary")),
        name="nsa_cmp_attention",
    )(hb3, kvc, kvc, tb)


def _nsa_streams(q_ref, k_ref, v_ref):
    lo = _lane_lo(T_ATT)
    full = slice(0, LANES)
    q = q_ref[0] * (C_DIM ** -0.5)
    zero = jnp.zeros_like(q)
    return [(jnp.where(lo, q, zero), k_ref, full, v_ref, full),
            (jnp.where(lo, zero, q), k_ref, full, v_ref, full)], lo


def _slc_kernel(q_ref, k_ref, v_ref, tb_ref, sel_ref, o_ref):
    t = T_ATT
    qi = pl.program_id(2)
    streams, lo = _nsa_streams(q_ref, k_ref, v_ref)
    sel = sel_ref[0, 0]
    blk_row = lax.broadcasted_iota(jnp.int32, (LANES, 2 * t), 0)
    blk_col = jnp.right_shift(lax.broadcasted_iota(jnp.int32, (LANES, 2 * t), 1), SLC_SHIFT)

    def score_fn(h):
        bias = _pair_bias(tb_ref, h, qi)

        def fn(i, s):
            expand = jnp.where(blk_row == blk_col + i * (2 * t // SLC_BLOCK), 1.0, 0.0).astype(BF16)
            return jnp.where(_dot(sel, expand) > 0.5, bias(i, s), NEG_INF)

        return fn

    o = _flash_causal(streams, [score_fn(0), score_fn(1)], qi // 2 + 1)
    o_ref[0] = jnp.where(lo, o[0], o[1])


def _win_kernel(q_ref, k_ref, v_ref, tb_ref, o_ref, *, n_prev):
    t = T_ATT
    qi = pl.program_id(2)
    streams, lo = _nsa_streams(q_ref, k_ref, v_ref)
    j0 = jnp.maximum(qi - n_prev, 0)

    def score_fn(h):
        def fn(_, s):
            return jnp.concatenate([s[:, u * t:(u + 1) * t] + tb_ref[h, qi - j0 - u + n_prev]
                                    for u in range(n_prev + 1)], axis=1)
        return fn

    carry = _flash_step(streams, [score_fn(0), score_fn(1)], j0 * t, (n_prev + 1) * t, 0, _flash_init(streams))
    o = [acc / l for _, l, acc in carry]
    o_ref[0] = jnp.where(lo, o[0], o[1])


def _nsa_attention(hb3, tb, k_blk, v_blk, n_prev=None, sel=None):
    b, s, _ = hb3.shape
    t = T_ATT
    n_e = tb.shape[1]
    in_specs = [pl.BlockSpec((1, t, LANES), lambda hp, bi, qi: (bi, qi, hp)),
                pl.BlockSpec((1, s, LANES), lambda hp, bi, qi: (bi, 0, k_blk + hp // 2)),
                pl.BlockSpec((1, s, LANES), lambda hp, bi, qi: (bi, 0, v_blk + hp // 2)),
                pl.BlockSpec((2, n_e, t, t), lambda hp, bi, qi: (hp, 0, 0, 0))]
    args = [hb3, hb3, hb3, tb]
    if sel is not None:
        in_specs.append(pl.BlockSpec((1, 1, t, LANES), lambda hp, bi, qi: (bi, hp // 2, qi, 0)))
        args.append(sel)
        body, name = _slc_kernel, "nsa_selected"
    else:
        assert s >= (n_prev + 1) * t
        body, name = functools.partial(_win_kernel, n_prev=n_prev), "nsa_window"
    return pl.pallas_call(
        body,
        grid=(C_HEADS // 2, b, s // t),
        in_specs=in_specs,
        out_specs=pl.BlockSpec((1, t, LANES), lambda hp, bi, qi: (bi, qi, hp)),
        out_shape=jax.ShapeDtypeStruct((b, s, C_HEADS * C_DIM), F32),
        compiler_params=_cparams(("arbitrary", "arbitrary", "arbitrary")),
        name=name,
    )(*args)


def _rms(x, g):
    return x * lax.rsqrt(jnp.mean(x * x, axis=-1, keepdims=True) + RMS_EPS) * g


def _mla_prep_kernel(cq_ref, ckv_ref, kr1_ref, kr2_ref, cos_ref, sin_ref, gq_ref, gkv_ref,
                     wq1_ref, wq2_ref, wk_ref, wv_ref, q_ref, k_ref, v_ref):
    cqn = _rms(cq_ref[...], gq_ref[...]).astype(BF16)
    c = _rms(ckv_ref[...], gkv_ref[...]).astype(BF16)
    cos, sin = cos_ref[...], sin_ref[...]
    k_rope = kr1_ref[...] * cos + kr2_ref[...] * sin
    scale = (D_NOPE + D_ROPE) ** -0.5
    for h in range(D_HEADS):
        cols = slice(h * LANES, (h + 1) * LANES)
        q = _dot(cqn, wq1_ref[:, cols]) * cos + _dot(cqn, wq2_ref[:, cols]) * sin
        q_ref[:, cols] = (q * scale).astype(q_ref.dtype)
        k_ref[:, cols] = (_dot(c, wk_ref[:, cols]) + k_rope).astype(k_ref.dtype)
    v_ref[...] = _dot(c, wv_ref[...]).astype(v_ref.dtype)


def _mla_prep(hf, cos, sin, gq, gkv, wq1, wq2, wk, wv, seq):
    m = hf.shape[0]
    n_seq = seq // TM
    const = lambda shape: pl.BlockSpec(shape, lambda i: (0, 0))
    hq = D_HEADS * LANES
    return pl.pallas_call(
        _mla_prep_kernel,
        grid=(m // TM,),
        in_specs=[pl.BlockSpec((TM, D_Q_LORA), lambda i: (i, 1)),
                  pl.BlockSpec((TM, D_KV_LORA), lambda i: (i, 3)),
                  pl.BlockSpec((TM, LANES), lambda i: (i, 8)),
                  pl.BlockSpec((TM, LANES), lambda i: (i, 9)),
                  pl.BlockSpec((TM, LANES), lambda i: (i % n_seq, 0)),
                  pl.BlockSpec((TM, LANES), lambda i: (i % n_seq, 0)),
                  const((1, D_Q_LORA)), const((1, D_KV_LORA)),
                  const(wq1.shape), const(wq2.shape), const(wk.shape), const(wv.shape)],
        out_specs=[pl.BlockSpec((TM, hq), lambda i: (i, 0)), pl.BlockSpec((TM, hq), lambda i: (i, 0)),
                   pl.BlockSpec((TM, D_HEADS * D_V), lambda i: (i, 0))],
        out_shape=[jax.ShapeDtypeStruct((m, hq), BF16), jax.ShapeDtypeStruct((m, hq), BF16),
                   jax.ShapeDtypeStruct((m, D_HEADS * D_V), BF16)],
        compiler_params=_cparams(("arbitrary",)),
        name="mla_prep",
    )(hf, hf, hf, hf, cos, sin, gq.reshape(1, -1), gkv.reshape(1, -1), wq1, wq2, wk, wv)


def _mla_kernel(q_ref, k_ref, v_ref, o_ref):
    t = T_ATT
    qi = pl.program_id(2)
    lo = _lane_lo(t)
    streams = []
    for half in range(2):
        cols = slice(half * LANES, (half + 1) * LANES)
        streams.append((q_ref[0, :, cols], k_ref, cols, v_ref, slice(0, LANES)))
    n_main = qi // 2
    key_minus_query = (lax.broadcasted_iota(jnp.int32, (t, 2 * t), 1)
                       - lax.broadcasted_iota(jnp.int32, (t, 2 * t), 0))

    def plain(i, s):
        return s

    def diagonal(i, s):
        return jnp.where(key_minus_query <= (qi - 2 * i) * t, s, NEG_INF)

    o = _flash_causal(streams, [plain] * 2, n_main, [diagonal] * 2)
    o_ref[0] = jnp.where(lo, o[0], o[1]).astype(o_ref.dtype)


def _mla_attention(q3, k3, v3):
    b, s, _ = q3.shape
    t = T_ATT
    return pl.pallas_call(
        _mla_kernel,
        grid=(D_HEADS // 2, b, s // t),
        in_specs=[pl.BlockSpec((1, t, 2 * LANES), lambda hp, bi, qi: (bi, qi, hp)),
                  pl.BlockSpec((1, s, 2 * LANES), lambda hp, bi, qi: (bi, 0, hp)),
                  pl.BlockSpec((1, s, LANES), lambda hp, bi, qi: (bi, 0, hp))],
        out_specs=pl.BlockSpec((1, t, LANES), lambda hp, bi, qi: (bi, qi, hp)),
        out_shape=jax.ShapeDtypeStruct((b, s, D_HEADS * D_V), BF16),
        compiler_params=_cparams(("arbitrary", "arbitrary", "arbitrary")),
        name="mla_attention",
    )(q3, k3, v3)


def _odd_out_kernel(gate_ref, oc_ref, os_ref, ow_ref, od_ref, x_ref, w_ref, g_ref, b_ref, out_ref):
    gates = 1.0 / (1.0 + jnp.exp(-gate_ref[...]))
    lo = _lane_lo(TM)
    n_c = C_HEADS * C_DIM
    y = _dot(od_ref[...], w_ref[n_c:, :])
    for blk in range(n_c // LANES):
        cols = slice(blk * LANES, (blk + 1) * LANES)
        acc = jnp.zeros((TM, LANES), F32)
        for br, o_ref in enumerate((oc_ref, os_ref, ow_ref)):
            c0 = br * C_HEADS + 2 * blk
            gexp = jnp.where(lo, gates[:, c0:c0 + 1], gates[:, c0 + 1:c0 + 2])
            acc = acc + gexp * o_ref[:, cols]
        y = y + _dot(acc.astype(BF16), w_ref[cols, :])
    out_ref[...] = _residual_ln(x_ref[...], y, g_ref[...], b_ref[...])


def _odd_out(hf, o_cmp, o_slc, o_win, o_d, x, w_out, g, beta):
    m = x.shape[0]
    n_c = C_HEADS * C_DIM
    row = lambda width: pl.BlockSpec((TM, width), lambda i: (i, 0))
    const = lambda shape: pl.BlockSpec(shape, lambda i: (0, 0))
    return pl.pallas_call(
        _odd_out_kernel,
        grid=(m // TM,),
        in_specs=[pl.BlockSpec((TM, LANES), lambda i: (i, 2)),
                  row(n_c), row(n_c), row(n_c), row(D_HEADS * D_V), row(D_MODEL),
                  const(w_out.shape), const((1, D_MODEL)), const((1, D_MODEL))],
        out_specs=row(D_MODEL),
        out_shape=jax.ShapeDtypeStruct((m, D_MODEL), F32),
        compiler_params=_cparams(("arbitrary",)),
        name="odd_out",
    )(hf, o_cmp, o_slc, o_win, o_d, x, w_out, g.reshape(1, -1), beta.reshape(1, -1))


ODD_BF16_W = 1536
ODD_F32_W = 1280


def _odd_in_columns():
    q_c = C_HEADS * C_DIM
    kv = lambda br, which, g: q_c + ((br * 2 + which) * C_KV_GROUPS + g) * C_DIM + np.arange(C_DIM)
    gate0 = q_c + 3 * 2 * C_KV_GROUPS * C_DIM
    cq0 = gate0 + 3 * C_HEADS
    ckv0 = cq0 + D_Q_LORA
    kr0 = ckv0 + D_KV_LORA
    zeros = lambda n: np.full(n, -1)
    cols = [np.arange(q_c)]
    for br in (1, 2):
        for which in (0, 1):
            for g in range(C_KV_GROUPS):
                cols += [kv(br, which, g), kv(br, which, g)]
    assert sum(len(c) for c in cols) == ODD_BF16_W
    for which in (0, 1):
        cols += [kv(0, which, 0), kv(0, which, 1)]
    cols += [gate0 + np.arange(3 * C_HEADS), zeros(LANES - 3 * C_HEADS)]
    cols += [cq0 + np.arange(D_Q_LORA), ckv0 + np.arange(D_KV_LORA)]
    half = D_ROPE // 2
    kr = kr0 + np.arange(D_ROPE)
    cols += [zeros(D_NOPE), kr, zeros(LANES - D_NOPE - D_ROPE)]
    cols += [zeros(D_NOPE), kr[half:], kr[:half], zeros(LANES - D_NOPE - D_ROPE)]
    cols = np.concatenate(cols)
    assert len(cols) == ODD_BF16_W + ODD_F32_W
    return cols


def _gather_cols(w, cols):
    return jnp.where(jnp.asarray(cols >= 0)[None, :], w[:, np.maximum(cols, 0)], 0.0)


def _mla_weight_columns():
    dq = D_NOPE + D_ROPE
    half = D_ROPE // 2
    zeros = lambda n: np.full(n, -1)
    q1, q2, k1 = [], [], []
    for h in range(D_HEADS):
        rope0 = h * dq + D_NOPE
        q1 += [h * dq + np.arange(dq), zeros(LANES - dq)]
        q2 += [zeros(D_NOPE), rope0 + half + np.arange(half), rope0 + np.arange(half), zeros(LANES - dq)]
        k1 += [h * D_NOPE + np.arange(D_NOPE), zeros(LANES - D_NOPE)]
    return np.concatenate(q1), np.concatenate(q2), np.concatenate(k1)


def _rope_tables(seq):
    half = D_ROPE // 2
    inv = ROPE_THETA ** (-jnp.arange(half, dtype=F32) / half)
    ang = jnp.arange(seq).astype(F32)[:, None] * inv
    cos, sin = jnp.cos(ang), jnp.sin(ang)
    pad = jnp.zeros((seq, LANES - D_NOPE - D_ROPE), F32)
    cos_t = jnp.concatenate([jnp.ones((seq, D_NOPE), F32), cos, cos, pad], axis=1)
    sin_t = jnp.concatenate([jnp.zeros((seq, D_NOPE), F32), -sin, sin, pad], axis=1)
    return cos_t, sin_t


def kernel(x, rel_bias, ev_w_in, ev_w_out, ev_lambda, ev_subln, od_w_in, od_w_out, od_cmp_pe, od_cmp_w1, od_cmp_w2, od_q_norm, od_kv_norm, od_w_uq, od_w_uk, od_w_uv, ffn_w_up, ffn_conv_w, ffn_conv_b, ffn_w_down, ln_g, ln_b):
    b, s, _ = x.shape
    m = b * s
    assert s % 1024 == 0 and s // B_PAIRS[-1][1] >= 2 * T_DIL and s // SLC_BLOCK <= LANES
    n_d = s // T_ATT

    tb_t = rel_bias.T.astype(F32)
    n_prev_win = -(-(WIN_SIZE - 1) // T_ATT)
    tb_causal = _bias_table(tb_t[:C_HEADS], _toeplitz_idx(1, n_d, T_ATT, s))
    tb_win = _bias_table(tb_t[:C_HEADS], _toeplitz_idx(n_prev_win, n_prev_win + 1, T_ATT, WIN_SIZE - 1))
    tb_cmp = _bias_table(tb_t[:C_HEADS], _cmp_idx(s)).reshape(C_HEADS, s, s // CMP_STRIDE)
    tb_dil = [_bias_table(tb_t[A_HEADS + i * B_HEADS:A_HEADS + (i + 1) * B_HEADS], _dilated_idx(d))
              for i, (_, d) in enumerate(B_PAIRS)]
    cos_t, sin_t = _rope_tables(s)
    odd_cols = _odd_in_columns()
    q1_cols, q2_cols, k1_cols = _mla_weight_columns()

    xf = x.reshape(m, D_MODEL)
    for l in range(DEPTH):
        i = l // 2
        if l % 2 == 0:
            (h,) = _inproj(xf, ev_w_in[i].astype(BF16), (EVEN_IN,), (BF16,), 768)
            h3 = h.reshape(b, s, EVEN_IN)
            lam_init = 0.8 - 0.6 * math.exp(-0.3 * l)
            o_a = _diff_attention(h3, tb_causal, ev_lambda[i].astype(F32), ev_subln[i], lam_init)
            obs, lses = [], []
            for p_idx, (_, d) in enumerate(B_PAIRS):
                o, lse = _dilated_attention(h3, tb_dil[p_idx], p_idx, d)
                obs.append(o)
                lses.append(lse)
            xf = _even_out(o_a.reshape(m, -1), obs, lses, xf, ev_w_out[i].astype(BF16), ln_g[l, 0], ln_b[l, 0])
        else:
            w_in = _gather_cols(od_w_in[i], odd_cols).astype(BF16)
            hb, hf = _inproj(xf, w_in, (ODD_BF16_W, ODD_F32_W), (BF16, F32), 256)
            hb3 = hb.reshape(b, s, ODD_BF16_W)
            n_ch = s // CMP_STRIDE
            chunks = (hf[:, :4 * C_DIM].reshape(b, n_ch, CMP_STRIDE, 4, C_DIM)
                      .transpose(0, 3, 1, 2, 4).reshape(b, 4, n_ch, CMP_STRIDE * C_DIM))
            pe = od_cmp_pe[i].reshape(2, 2, CMP_STRIDE * C_DIM)
            w2d = jnp.concatenate([od_cmp_w2[i], od_cmp_w2[i]], axis=-1).astype(BF16)
            kvc = _compress(chunks, pe, od_cmp_w1[i].astype(BF16), w2d)
            o_cmp, sel = _cmp_attention(hb3, kvc, tb_cmp)
            o_slc = _nsa_attention(hb3, tb_causal, 4, 6, sel=sel)
            o_win = _nsa_attention(hb3, tb_win, 8, 10, n_prev=n_prev_win)
            q_d, k_d, v_d = _mla_prep(hf, cos_t, sin_t, od_q_norm[i], od_kv_norm[i],
                                      _gather_cols(od_w_uq[i], q1_cols).astype(BF16),
                                      _gather_cols(od_w_uq[i], q2_cols).astype(BF16),
                                      _gather_cols(od_w_uk[i], k1_cols).astype(BF16),
                                      od_w_uv[i].astype(BF16), s)
            o_d = _mla_attention(q_d.reshape(b, s, -1), k_d.reshape(b, s, -1), v_d.reshape(b, s, -1))
            xf = _odd_out(hf, o_cmp.reshape(m, -1), o_slc.reshape(m, -1), o_win.reshape(m, -1),
                          o_d.reshape(m, -1), xf, od_w_out[i].astype(BF16), ln_g[l, 0], ln_b[l, 0])
        xf = _ffn(xf, ffn_w_up[l].astype(BF16), ffn_conv_w[l], ffn_conv_b[l], ffn_w_down[l].astype(BF16),
                  ln_g[l, 1], ln_b[l, 1], s)
    return xf.reshape(b, s, D_MODEL)
```

```python
import functools
import math

import numpy as np
import jax
import jax.numpy as jnp
from jax import lax
from jax.experimental import pallas as pl
from jax.experimental.pallas import tpu as pltpu

F32 = jnp.float32
BF16 = jnp.bfloat16

D_MODEL = 1024
DEPTH = 4
LN_EPS = 1e-5
RMS_EPS = 1e-5
N_BUCKETS = 32
MAX_DISTANCE = 2048
A_HEADS, A_QK, A_V = 4, 64, 128
B_PAIRS = ((128, 1), (512, 4), (2048, 16))
B_HEADS, B_DIM = 4, 64
C_HEADS, C_KV_GROUPS, C_DIM = 8, 2, 64
CMP_LEN, CMP_STRIDE, CMP_HIDDEN = 32, 16, 256
SLC_BLOCK, SLC_TOP_N = 64, 16
SLC_SHIFT = 6
WIN_SIZE = 512
D_HEADS, D_Q_LORA, D_KV_LORA, D_NOPE, D_ROPE, D_V = 8, 384, 256, 64, 32, 64
ROPE_THETA = 10000.0
D_FF = 2816
EVEN_DILATED = len(B_PAIRS) * 3 * B_HEADS * B_DIM
EVEN_IN = 2 * A_HEADS * A_QK * 2 + A_HEADS * A_V + EVEN_DILATED
ALPHA = (2 * DEPTH) ** 0.25

LANES = 128
HALF_LANE_SHIFT = 6
VMEM_LIMIT = 56 * 1024 * 1024
T_ATT = 256
T_DIL = 128
TM = 512
FF_CHUNK = 256
NEG_INF = float("-inf")
M_INIT = -1e30


def _cparams(sem):
    return pltpu.CompilerParams(dimension_semantics=sem, vmem_limit_bytes=VMEM_LIMIT)


def _dot(a, b):
    return jnp.dot(a, b, preferred_element_type=F32)


def _dot_nt(a, b):
    return lax.dot_general(a, b, (((1,), (1,)), ((), ())), preferred_element_type=F32)


def _bucket_np(dist):
    n = np.maximum(dist, 0)
    nf = np.maximum(n, 1).astype(np.float32)
    max_exact = N_BUCKETS // 2
    large = max_exact + (np.log(nf / max_exact) / math.log(MAX_DISTANCE / max_exact)
                         * (N_BUCKETS - max_exact)).astype(np.int32)
    return np.where(n < max_exact, n, np.minimum(large, N_BUCKETS - 1)).astype(np.int32)


def _toeplitz_idx(n_masked, n_delta, t, max_dist):
    key = np.arange(t)[:, None]
    query = np.arange(t)[None, :]
    out = []
    for delta in range(-n_masked, n_delta):
        dist = t * delta + query - key
        ok = (dist >= 0) & (dist <= max_dist)
        out.append(np.where(ok, _bucket_np(dist), -1))
    return np.stack(out).astype(np.int32)


def _dilated_idx(d):
    t = T_DIL
    r = np.arange(t)[:, None]
    c = np.arange(2 * t)[None, :]
    dist0 = r - c
    ok0 = (dist0 >= 0) & (c < t)
    dist1 = t + r - c
    ok1 = (dist1 >= 0) & (dist1 <= t)
    return np.stack([np.where(ok0, _bucket_np(dist0 * d), -1),
                     np.where(ok1, _bucket_np(dist1 * d), -1)]).astype(np.int32)


def _cmp_idx(seq):
    ncb = seq // CMP_STRIDE - 1
    q = np.arange(seq)[:, None]
    c = np.arange(seq // CMP_STRIDE)[None, :]
    dist = q - (c * CMP_STRIDE + CMP_LEN - 1)
    ok = (dist >= 0) & (c < ncb)
    return np.where(ok, _bucket_np(dist), -1).astype(np.int32).reshape(seq // T_ATT, T_ATT, seq // CMP_STRIDE)


def _bias_kernel(tbl_ref, idx_ref, o_ref):
    h = pl.program_id(0)
    idx = idx_ref[0]
    out = jnp.full(idx.shape, NEG_INF, F32)
    for b in range(N_BUCKETS):
        out = jnp.where(idx == b, tbl_ref[h, b], out)
    o_ref[0, 0] = out


def _bias_table(tbl, idx):
    n_h = tbl.shape[0]
    n, r, c = idx.shape
    return pl.pallas_call(
        _bias_kernel,
        grid=(n_h, n),
        in_specs=[pl.BlockSpec(memory_space=pltpu.SMEM),
                  pl.BlockSpec((1, r, c), lambda h, i: (i, 0, 0))],
        out_specs=pl.BlockSpec((1, 1, r, c), lambda h, i: (h, i, 0, 0)),
        out_shape=jax.ShapeDtypeStruct((n_h, n, r, c), F32),
        compiler_params=_cparams(("arbitrary", "arbitrary")),
        name="bias_table",
    )(tbl, jnp.asarray(idx))


def _inproj_kernel(x_ref, w_ref, *o_refs, widths, chunk):
    xb = x_ref[...].astype(BF16)
    off = 0
    for o_ref, width in zip(o_refs, widths):
        for c0 in range(0, width, chunk):
            o_ref[:, c0:c0 + chunk] = _dot(xb, w_ref[:, off + c0:off + c0 + chunk]).astype(o_ref.dtype)
        off += width


def _inproj(x, w, widths, dtypes, chunk):
    m = x.shape[0]
    n = w.shape[1]
    assert sum(widths) == n and all(wd % chunk == 0 for wd in widths)
    return pl.pallas_call(
        functools.partial(_inproj_kernel, widths=widths, chunk=chunk),
        grid=(m // TM,),
        in_specs=[pl.BlockSpec((TM, D_MODEL), lambda i: (i, 0)),
                  pl.BlockSpec((D_MODEL, n), lambda i: (0, 0))],
        out_specs=[pl.BlockSpec((TM, wd), lambda i: (i, 0)) for wd in widths],
        out_shape=[jax.ShapeDtypeStruct((m, wd), dt) for wd, dt in zip(widths, dtypes)],
        compiler_params=_cparams(("arbitrary",)),
        name="in_proj",
    )(x, w)


def _tree_reduce_rows(x, combine, reduce_fn):
    parts = [x[r:r + 8] for r in range(0, x.shape[0], 8)]
    while len(parts) > 1:
        parts = [combine(parts[k], parts[k + 1]) for k in range(0, len(parts) - 1, 2)] + \
                ([parts[-1]] if len(parts) % 2 else [])
    return reduce_fn(parts[0], axis=0, keepdims=True)


def _flash_init(streams):
    out = []
    for q, _, _, _, vt_rows in streams:
        t_q = q.shape[0]
        out.append((jnp.full((1, t_q), M_INIT, F32), jnp.zeros((1, t_q), F32),
                    jnp.zeros((vt_rows.stop - vt_rows.start, t_q), F32)))
    return tuple(out)


def _flash_scores(streams, key_start, n_keys):
    keys = pl.ds(pl.multiple_of(key_start, T_ATT), n_keys)
    return tuple(_dot_nt(k_ref[0, keys, k_cols], q) for q, k_ref, k_cols, _, _ in streams)


def _flash_update(streams, score_fns, key_start, n_keys, i, raw, state):
    keys = pl.ds(pl.multiple_of(key_start, T_ATT), n_keys)
    scores = [fn(i, s) for fn, s in zip(score_fns, raw)]
    m_new = [jnp.maximum(m, _tree_reduce_rows(s, jnp.maximum, jnp.max)) for s, (m, _, _) in zip(scores, state)]
    probs = [jnp.exp(s - mn) for s, mn in zip(scores, m_new)]
    out = []
    for (_, _, _, vt_ref, vt_rows), p, mn, (m, l, acc) in zip(streams, probs, m_new, state):
        alpha = jnp.exp(m - mn)
        l = alpha * l + _tree_reduce_rows(p, jnp.add, jnp.sum)
        acc = alpha * acc + _dot(vt_ref[vt_rows, keys], p.astype(BF16))
        out.append((mn, l, acc))
    return tuple(out)


def _flash_finish(state):
    return [acc / l for _, l, acc in state]


def _flash_balanced(pi, n_d, qa, qb, k_refs, k_cols, vt_ref, vt_rows, score, slot_ctx=None):
    kt = 2 * T_ATT
    t_q = qa[0].shape[0]
    n_streams = len(qa)
    n_slots = n_d // 2 + 1
    max_a = n_d // 4
    n_a = pi // 2 + 1
    tile_a, tile_b = pi, n_d - 1 - pi

    def slot(s):
        if s == 0:
            return True, tile_a, n_a - 1
        if s >= max_a:
            return False, tile_b, s - n_a
        is_a = s < n_a
        return is_a, jnp.where(is_a, tile_a, tile_b), jnp.where(is_a, n_a - 1 - s, s - n_a)

    def pick(is_a, a, b):
        return (a if is_a else b) if isinstance(is_a, bool) else jnp.where(is_a, a, b)

    def keys_of(i):
        return pl.ds(pl.multiple_of(i * kt, kt), kt)

    def raw_scores(s):
        is_a, _, i = slot(s)
        return [_dot_nt(k_refs[n][0, keys_of(i), k_cols[n]], pick(is_a, qa[n], qb[n])) for n in range(n_streams)]

    state = [(jnp.full((1, t_q), M_INIT, F32), jnp.zeros((1, t_q), F32),
              jnp.zeros((vt_rows[n].stop - vt_rows[n].start, t_q), F32)) for n in range(n_streams)]
    out_a = [jnp.zeros_like(acc) for _, _, acc in state]
    pending = None
    raw = raw_scores(0)

    def fold(state, pending):
        alphas, probs, i_prev = pending
        return [(m, l, alphas[n] * acc + _dot(vt_ref[vt_rows[n], keys_of(i_prev)], probs[n]))
                for n, (m, l, acc) in enumerate(state)]

    for s in range(n_slots):
        raw_next = raw_scores(s + 1) if s + 1 < n_slots else None
        is_a, q_tile, i = slot(s)
        if pending is not None:
            state = fold(state, pending)
        if 1 <= s <= max_a:
            switch = s == n_a
            out_a = [jnp.where(switch, acc / l, o) for (_, l, acc), o in zip(state, out_a)]
            state = [(jnp.where(switch, M_INIT, m), jnp.where(switch, 0.0, l), jnp.where(switch, 0.0, acc))
                     for m, l, acc in state]
        ctx = None if slot_ctx is None else slot_ctx(is_a, q_tile, i)
        diag = s == 0 or s == n_slots - 1
        scores = [score(n, ctx, q_tile, i, raw[n], diag) for n in range(n_streams)]
        m_new = [jnp.maximum(m, _tree_reduce_rows(sc, jnp.maximum, jnp.max)) for sc, (m, _, _) in zip(scores, state)]
        probs = [jnp.exp(sc - mn) for sc, mn in zip(scores, m_new)]
        alphas = [jnp.exp(m - mn) for (m, _, _), mn in zip(state, m_new)]
        state = [(mn, al * l + _tree_reduce_rows(p, jnp.add, jnp.sum), acc)
                 for mn, al, p, (_, l, acc) in zip(m_new, alphas, probs, state)]
        pending = (alphas, [p.astype(BF16) for p in probs], i)
        raw = raw_next
    state = fold(state, pending)
    return out_a, [acc / l for _, l, acc in state]


def _pair_bias(tb_ref, h, q_tile, i, s):
    t = T_ATT
    d0 = q_tile - 2 * i
    return jnp.concatenate([s[:t] + tb_ref[h, d0 + 1], s[t:] + tb_ref[h, d0]], axis=0)


def _fill_transposed(vt_ref, v_ref):
    n = v_ref.shape[1]
    step = 2 * T_ATT
    for c0 in range(0, n, step):
        vt_ref[:, c0:c0 + step] = v_ref[0, c0:c0 + step, :].astype(F32).T.astype(BF16)


def _lane_lo(rows):
    return lax.broadcasted_iota(jnp.int32, (rows, LANES), 1) < (LANES // 2)


def _diff_kernel(lam_ref, g_ref, q0a_ref, q0b_ref, q1a_ref, q1b_ref, k0_ref, k1_ref, v_ref, tb_ref,
                 oa_ref, ob_ref, vt_ref, *, lam_init, n_d):
    h = pl.program_id(0)
    pi = pl.program_id(2)

    @pl.when(pi == 0)
    def _():
        _fill_transposed(vt_ref, v_ref)

    lp = lam_ref[...]
    lam = (jnp.exp(jnp.sum(lp[0:1] * lp[1:2], axis=-1, keepdims=True))
           - jnp.exp(jnp.sum(lp[2:3] * lp[3:4], axis=-1, keepdims=True)) + lam_init)
    keep = jnp.right_shift(lax.broadcasted_iota(jnp.int32, (T_ATT, LANES), 1), HALF_LANE_SHIFT) == h % 2

    def prep(q_ref):
        q = q_ref[0] * (A_QK ** -0.5)
        return jnp.where(keep, q, jnp.zeros_like(q))

    full = slice(0, LANES)

    def score(n, ctx, q_tile, i, s, diag):
        return _pair_bias(tb_ref, 0, q_tile, i, s)

    out_a, out_b = _flash_balanced(pi, n_d, [prep(q0a_ref), prep(q1a_ref)], [prep(q0b_ref), prep(q1b_ref)],
                                   [k0_ref, k1_ref], [full, full], vt_ref, [slice(0, A_V)] * 2, score)
    for outs, o_ref in ((out_a, oa_ref), (out_b, ob_ref)):
        d = (outs[0] - lam * outs[1]).T
        ms = jnp.mean(d * d, axis=-1, keepdims=True)
        o_ref[0] = (d * lax.rsqrt(ms + RMS_EPS) * g_ref[...] * (1.0 - lam_init)).astype(o_ref.dtype)


def _join_halves(o_a, o_b):
    return jnp.concatenate([o_a, o_b], axis=1)


def _diff_attention(h3, tb, lam_p, subln_g, lam_init):
    b, s, _ = h3.shape
    t = T_ATT
    n_d = s // t
    n_p = n_d // 2
    q_spec = lambda col0, second: pl.BlockSpec(
        (1, t, LANES), lambda h, bi, pi: (bi, (n_d - 1 - pi) if second else pi, col0 + h // 2))
    kv_spec = lambda col_fn: pl.BlockSpec((1, s, LANES), lambda h, bi, pi: (bi, 0, col_fn(h)))
    o_a, o_b = pl.pallas_call(
        functools.partial(_diff_kernel, lam_init=lam_init, n_d=n_d),
        grid=(A_HEADS, b, n_p),
        in_specs=[pl.BlockSpec((4, A_QK), lambda h, bi, pi: (0, 0)),
                  pl.BlockSpec((1, A_V), lambda h, bi, pi: (0, 0)),
                  q_spec(0, False), q_spec(0, True), q_spec(2, False), q_spec(2, True),
                  kv_spec(lambda h: 4 + h // 2), kv_spec(lambda h: 6 + h // 2), kv_spec(lambda h: 8 + h),
                  pl.BlockSpec((1, n_d + 1, t, t), lambda h, bi, pi: (h, 0, 0, 0))],
        out_specs=[pl.BlockSpec((1, t, A_V), lambda h, bi, pi: (bi, pi, h)),
                   pl.BlockSpec((1, t, A_V), lambda h, bi, pi: (bi, n_p - 1 - pi, h))],
        out_shape=[jax.ShapeDtypeStruct((b, s // 2, A_HEADS * A_V), BF16)] * 2,
        scratch_shapes=[pltpu.VMEM((A_V, s), BF16)],
        compiler_params=_cparams(("arbitrary", "arbitrary", "arbitrary")),
        name="diff_attention",
    )(lam_p, subln_g.reshape(1, A_V), h3, h3, h3, h3, h3, h3, h3, tb)
    return _join_halves(o_a, o_b)


def _dilated_kernel(q_ref, k_ref, v_ref, tb_ref, o_ref, lse_ref, *, d):
    r = pl.program_id(2)
    t = T_DIL
    n_blocks = q_ref.shape[1] // (d * t)
    lo = _lane_lo(t)
    scale = B_DIM ** -0.5

    def block(bi, carry):
        var = jnp.minimum(bi, 1)
        qrows = pl.ds(r + d * t * bi, t, stride=d)
        krows = pl.ds(r + d * t * jnp.maximum(bi - 1, 0), 2 * t, stride=d)
        q = (q_ref[0, qrows, :] * scale).astype(BF16)
        kk = k_ref[0, krows, :].astype(BF16)
        vv = v_ref[0, krows, :].astype(BF16)
        o_half, l_half = [], []
        for half in range(2):
            keep = lo if half == 0 else jnp.logical_not(lo)
            qm = jnp.where(keep, q, jnp.zeros_like(q))
            s = _dot_nt(qm, kk) + tb_ref[half, var]
            m = jnp.max(s, axis=-1, keepdims=True)
            p = jnp.exp(s - m)
            l = jnp.sum(p, axis=-1, keepdims=True)
            o_half.append(_dot(p.astype(BF16), vv) / l)
            l_half.append(m + jnp.log(l))
        o_ref[0, qrows, :] = jnp.where(lo, o_half[0], o_half[1])
        lse_ref[0, qrows, :] = jnp.where(lo, l_half[0], l_half[1])
        return carry

    lax.fori_loop(0, n_blocks, block, 0, unroll=min(n_blocks, 4))


def _dilated_attention(hd3, tb, pair_idx, d):
    b, s, _ = hd3.shape
    hd = B_HEADS * B_DIM
    base = pair_idx * 3 * (hd // LANES)
    seq = lambda which: pl.BlockSpec((1, s, LANES), lambda bi, hp, r: (bi, 0, base + which * (hd // LANES) + hp))
    out = pl.BlockSpec((1, s, LANES), lambda bi, hp, r: (bi, 0, hp))
    o, lse = pl.pallas_call(
        functools.partial(_dilated_kernel, d=d),
        grid=(b, hd // LANES, d),
        in_specs=[seq(0), seq(1), seq(2),
                  pl.BlockSpec((2, 2, T_DIL, 2 * T_DIL), lambda bi, hp, r: (hp, 0, 0, 0))],
        out_specs=[out, out],
        out_shape=[jax.ShapeDtypeStruct((b, s, hd), F32)] * 2,
        compiler_params=_cparams(("arbitrary", "arbitrary", "arbitrary")),
        name="dilated_attention",
    )(hd3, hd3, hd3, tb)
    return o.reshape(b * s, hd), lse.reshape(b * s, hd)


def _residual_ln(x, y, g, beta):
    z = ALPHA * x + y
    mu = jnp.mean(z, axis=-1, keepdims=True)
    zc = z - mu
    var = jnp.mean(zc * zc, axis=-1, keepdims=True)
    return zc * lax.rsqrt(var + LN_EPS) * g + beta


def _even_out_kernel(oa_ref, o0_ref, o1_ref, o2_ref, l0_ref, l1_ref, l2_ref, x_ref, w_ref, g_ref, b_ref, out_ref):
    l0, l1, l2 = l0_ref[...], l1_ref[...], l2_ref[...]
    mx = jnp.maximum(jnp.maximum(l0, l1), l2)
    e0, e1, e2 = jnp.exp(l0 - mx), jnp.exp(l1 - mx), jnp.exp(l2 - mx)
    den = e0 + e1 + e2
    ob = (e0 / den) * o0_ref[...] + (e1 / den) * o1_ref[...] + (e2 / den) * o2_ref[...]
    n_a = A_HEADS * A_V
    y = _dot(oa_ref[...], w_ref[0:n_a, :]) + _dot(ob.astype(BF16), w_ref[n_a:, :])
    out_ref[...] = _residual_ln(x_ref[...], y, g_ref[...], b_ref[...])


def _even_out(oa, obs, lses, x, w_out, g, beta):
    m = x.shape[0]
    hd = B_HEADS * B_DIM
    row = lambda width: pl.BlockSpec((TM, width), lambda i: (i, 0))
    const = lambda shape: pl.BlockSpec(shape, lambda i: (0, 0))
    return pl.pallas_call(
        _even_out_kernel,
        grid=(m // TM,),
        in_specs=[row(A_HEADS * A_V)] + [row(hd)] * 6 + [row(D_MODEL), const(w_out.shape),
                                                        const((1, D_MODEL)), const((1, D_MODEL))],
        out_specs=row(D_MODEL),
        out_shape=jax.ShapeDtypeStruct((m, D_MODEL), F32),
        compiler_params=_cparams(("arbitrary",)),
        name="even_out",
    )(oa, *obs, *lses, x, w_out, g.reshape(1, -1), beta.reshape(1, -1))


def _gelu(x):
    return 0.5 * x * (1.0 + jnp.tanh(math.sqrt(2.0 / math.pi) * (x + 0.044715 * (x * x * x))))


def _ffn_kernel(x_ref, wu_ref, cw_ref, cb_ref, wd_ref, g_ref, b_ref, out_ref, acc_ref, gs_ref, tail_ref,
                *, tiles_per_seq):
    halo = 8

    @pl.when(pl.program_id(0) % tiles_per_seq == 0)
    def _():
        tail_ref[...] = jnp.zeros(tail_ref.shape, F32)

    x = x_ref[...]
    xb = x.astype(BF16)
    for c in range(D_FF // FF_CHUNK):
        cols = slice(c * FF_CHUNK, (c + 1) * FF_CHUNK)
        a = _dot(xb, wu_ref[:, cols])
        gate = _dot(xb, wu_ref[:, D_FF + c * FF_CHUNK:D_FF + (c + 1) * FF_CHUNK])
        gs_ref[0:halo, :] = tail_ref[:, cols]
        gs_ref[halo:, :] = gate
        tail_ref[:, cols] = gate[TM - halo:, :]
        conv = (gs_ref[pl.ds(halo - 2, TM), :] * cw_ref[0:1, cols]
                + gs_ref[pl.ds(halo - 1, TM), :] * cw_ref[1:2, cols]
                + gate * cw_ref[2:3, cols] + cb_ref[:, cols])
        u = (_gelu(conv) * a).astype(BF16)
        part = _dot(u, wd_ref[cols, :])
        if c == 0:
            acc_ref[...] = part
        else:
            acc_ref[...] += part
    out_ref[...] = _residual_ln(x, acc_ref[...], g_ref[...], b_ref[...])


def _ffn(x, w_up, conv_w, conv_b, w_down, g, beta, seq):
    m = x.shape[0]
    const = lambda shape: pl.BlockSpec(shape, lambda i: (0, 0))
    return pl.pallas_call(
        functools.partial(_ffn_kernel, tiles_per_seq=seq // TM),
        grid=(m // TM,),
        in_specs=[pl.BlockSpec((TM, D_MODEL), lambda i: (i, 0)), const(w_up.shape), const(conv_w.shape),
                  const((1, D_FF)), const(w_down.shape), const((1, D_MODEL)), const((1, D_MODEL))],
        out_specs=pl.BlockSpec((TM, D_MODEL), lambda i: (i, 0)),
        out_shape=jax.ShapeDtypeStruct((m, D_MODEL), F32),
        scratch_shapes=[pltpu.VMEM((TM, D_MODEL), F32), pltpu.VMEM((TM + 8, FF_CHUNK), F32),
                        pltpu.VMEM((8, D_FF), F32)],
        compiler_params=_cparams(("arbitrary",)),
        name="conv_ffn",
    )(x, w_up, conv_w, conv_b.reshape(1, -1), w_down, g.reshape(1, -1), beta.reshape(1, -1))


def _compress_kernel(ch_ref, pe_ref, w1_ref, w2_ref, o_ref):
    half = CMP_STRIDE * C_DIM
    ch = ch_ref[0, 0]
    a = _dot((ch + pe_ref[0, 0:1, :]).astype(BF16), w1_ref[0, 0:half, :])
    b = _dot((ch + pe_ref[0, 1:2, :]).astype(BF16), w1_ref[0, half:, :])
    n = ch.shape[0]
    hid = _gelu(a + pltpu.roll(b, n - 1, 0))
    o_ref[0, 0] = _dot(hid.astype(BF16), w2_ref[0]).astype(o_ref.dtype)


def _compress(chunks, pe, w1, w2d):
    b, _, n, width = chunks.shape
    return pl.pallas_call(
        _compress_kernel,
        grid=(b, 4),
        in_specs=[pl.BlockSpec((1, 1, n, width), lambda bi, j: (bi, j, 0, 0)),
                  pl.BlockSpec((1, 2, width), lambda bi, j: (j // 2, 0, 0)),
                  pl.BlockSpec((1, 2 * width, CMP_HIDDEN), lambda bi, j: (j // 2, 0, 0)),
                  pl.BlockSpec((1, CMP_HIDDEN, LANES), lambda bi, j: (j // 2, 0, 0))],
        out_specs=pl.BlockSpec((1, 1, n, LANES), lambda bi, j: (bi, j, 0, 0)),
        out_shape=jax.ShapeDtypeStruct((b, 4, n, LANES), BF16),
        compiler_params=_cparams(("arbitrary", "arbitrary")),
        name="nsa_compress",
    )(chunks, pe, w1, w2d)


def _cmp_kernel(q_ref, kc_ref, vc_ref, tb_ref, o_ref, sel_ref, pg_ref, sc_ref):
    t = T_ATT
    qi = pl.program_id(1)
    n_c = kc_ref.shape[2]
    n_sel = n_c // (SLC_BLOCK // CMP_STRIDE)
    lo = _lane_lo(t)
    kc = kc_ref[0, 0]
    vc = vc_ref[0, 0]
    pg = jnp.zeros((t, n_c), F32)
    for pair in range(2):
        cols = slice(pair * LANES, (pair + 1) * LANES)
        q = q_ref[0, :, cols]
        o_half = []
        for half in range(2):
            keep = lo if half == 0 else jnp.logical_not(lo)
            qm = jnp.where(keep, q * (C_DIM ** -0.5), jnp.zeros_like(q))
            s = _dot_nt(qm, kc) + tb_ref[2 * pair + half]
            m = jnp.maximum(jnp.max(s, axis=-1, keepdims=True), M_INIT)
            p = jnp.exp(s - m)
            den = jnp.sum(p, axis=-1, keepdims=True)
            p = p / jnp.where(den > 0, den, 1.0)
            o_half.append(_dot(p.astype(BF16), vc))
            pg = pg + p
        o_ref[0, :, cols] = jnp.where(lo, o_half[0], o_half[1])
    pad = 8
    pg_t = pg.T
    for c in range(t // LANES):
        pg_ref[c, 0:pad, :] = jnp.zeros((pad, LANES), F32)
        pg_ref[c, pad:pad + n_c, :] = pg_t[:, c * LANES:(c + 1) * LANES]
        pg_ref[c, pad + n_c:, :] = jnp.zeros((pad, LANES), F32)
    r = SLC_BLOCK // CMP_STRIDE
    tap = lambda k: jnp.concatenate(
        [pg_ref[c, pl.ds(pad + k, n_sel, stride=r), :] for c in range(t // LANES)], axis=1)
    score = (0.5 * tap(-1) + ((tap(0) + tap(1)) + tap(2))) + 0.5 * tap(3)
    jb = lax.broadcasted_iota(jnp.int32, (n_sel, t), 0)
    qblk = jnp.right_shift(qi * t + lax.broadcasted_iota(jnp.int32, (n_sel, t), 1), SLC_SHIFT)
    forced = (jb == 0) | (jb == qblk) | (jb == qblk - 1)
    sc = jnp.where(forced, jnp.inf, jnp.where(jb <= qblk, score, NEG_INF))
    sc_ref[...] = sc
    cnt = jnp.zeros((n_sel, t), jnp.int32)
    for i in range(n_sel):
        row = sc_ref[pl.ds(i, 1), :]
        beats = (row > sc) | ((row == sc) & (i < jb))
        cnt = cnt + beats.astype(jnp.int32)
    sel_ref[0, 0] = jnp.zeros(sel_ref.shape[2:], sel_ref.dtype)
    sel_ref[0, 0, 0:n_sel, :] = (cnt < SLC_TOP_N).astype(sel_ref.dtype)


def _cmp_attention(hb3, kvc, tb):
    b, s, _ = hb3.shape
    t = T_ATT
    n_c = s // CMP_STRIDE
    n_sel = s // SLC_BLOCK
    assert n_sel <= LANES
    gw = (C_HEADS // C_KV_GROUPS) * C_DIM
    return pl.pallas_call(
        _cmp_kernel,
        grid=(C_KV_GROUPS, s // t, b),
        in_specs=[pl.BlockSpec((1, t, gw), lambda g, qi, bi: (bi, qi, g)),
                  pl.BlockSpec((1, 1, n_c, LANES), lambda g, qi, bi: (bi, g, 0, 0)),
                  pl.BlockSpec((1, 1, n_c, LANES), lambda g, qi, bi: (bi, 2 + g, 0, 0)),
                  pl.BlockSpec((4, t, n_c), lambda g, qi, bi: (g, qi, 0))],
        out_specs=[pl.BlockSpec((1, t, gw), lambda g, qi, bi: (bi, qi, g)),
                   pl.BlockSpec((1, 1, LANES, t), lambda g, qi, bi: (bi, g, 0, qi))],
        out_shape=[jax.ShapeDtypeStruct((b, s, C_HEADS * C_DIM), F32),
                   jax.ShapeDtypeStruct((b, C_KV_GROUPS, LANES, s), BF16)],
        scratch_shapes=[pltpu.VMEM((t // LANES, n_c + 16, LANES), F32), pltpu.VMEM((n_sel, t), F32)],
        compiler_params=_cparams(("arbitrary", "arbitrary", "arbitrary")),
        name="nsa_cmp_attention",
    )(hb3, kvc, kvc, tb)


def _head_pair_queries(q_ref):
    lo = _lane_lo(T_ATT)
    q = q_ref[0] * (C_DIM ** -0.5)
    zero = jnp.zeros_like(q)
    return [jnp.where(lo, q, zero), jnp.where(lo, zero, q)]


def _slc_kernel(qa_ref, qb_ref, k_ref, v_ref, tb_ref, sela_ref, selb_ref, oa_ref, ob_ref, vt_ref, *, n_d):
    t = T_ATT
    pi = pl.program_id(2)

    @pl.when(pi == 0)
    def _():
        _fill_transposed(vt_ref, v_ref)

    key_blk = jnp.right_shift(lax.broadcasted_iota(jnp.int32, (2 * t, LANES), 0), SLC_SHIFT)
    blk_slot = lax.broadcasted_iota(jnp.int32, (2 * t, LANES), 1)
    sel_a, sel_b = sela_ref[0, 0], selb_ref[0, 0]

    def slot_ctx(is_a, q_tile, i):
        sel = (sel_a if is_a else sel_b) if isinstance(is_a, bool) else jnp.where(is_a, sel_a, sel_b)
        expand = jnp.where(blk_slot == key_blk + i * (2 * t // SLC_BLOCK), 1.0, 0.0).astype(BF16)
        return _dot(expand, sel) > 0.5

    def score(n, ctx, q_tile, i, s, diag):
        return jnp.where(ctx, _pair_bias(tb_ref, n, q_tile, i, s), NEG_INF)

    full = slice(0, LANES)
    out_a, out_b = _flash_balanced(pi, n_d, _head_pair_queries(qa_ref), _head_pair_queries(qb_ref),
                                   [k_ref, k_ref], [full, full], vt_ref, [slice(0, C_DIM)] * 2, score, slot_ctx)
    oa_ref[0] = jnp.concatenate(out_a, axis=0).T
    ob_ref[0] = jnp.concatenate(out_b, axis=0).T


def _win_kernel(q_ref, k_ref, v_ref, tb_ref, o_ref, vt_ref, *, n_prev):
    t = T_ATT
    qi = pl.program_id(2)

    @pl.when(qi == 0)
    def _():
        _fill_transposed(vt_ref, v_ref)

    full = slice(0, LANES)
    streams = [(q, k_ref, full, vt_ref, slice(0, C_DIM)) for q in _head_pair_queries(q_ref)]
    j0 = jnp.maximum(qi - n_prev, 0)

    def score_fn(h):
        def fn(_, s):
            return jnp.concatenate([s[u * t:(u + 1) * t] + tb_ref[h, qi - j0 - u + n_prev]
                                    for u in range(n_prev + 1)], axis=0)
        return fn

    n_keys = (n_prev + 1) * t
    state = _flash_update(streams, [score_fn(0), score_fn(1)], j0 * t, n_keys, 0,
                          _flash_scores(streams, j0 * t, n_keys), _flash_init(streams))
    o_ref[0] = jnp.concatenate(_flash_finish(state), axis=0).T


def _slc_attention(hb3, tb, k_blk, v_blk, sel):
    b, s, _ = hb3.shape
    t = T_ATT
    n_d = s // t
    n_p = n_d // 2
    o_a, o_b = pl.pallas_call(
        functools.partial(_slc_kernel, n_d=n_d),
        grid=(C_HEADS // 2, b, n_p),
        in_specs=[pl.BlockSpec((1, t, LANES), lambda hp, bi, pi: (bi, pi, hp)),
                  pl.BlockSpec((1, t, LANES), lambda hp, bi, pi: (bi, n_d - 1 - pi, hp)),
                  pl.BlockSpec((1, s, LANES), lambda hp, bi, pi: (bi, 0, k_blk + hp // 2)),
                  pl.BlockSpec((1, s, LANES), lambda hp, bi, pi: (bi, 0, v_blk + hp // 2)),
                  pl.BlockSpec((2, n_d + 1, t, t), lambda hp, bi, pi: (hp, 0, 0, 0)),
                  pl.BlockSpec((1, 1, LANES, t), lambda hp, bi, pi: (bi, hp // 2, 0, pi)),
                  pl.BlockSpec((1, 1, LANES, t), lambda hp, bi, pi: (bi, hp // 2, 0, n_d - 1 - pi))],
        out_specs=[pl.BlockSpec((1, t, LANES), lambda hp, bi, pi: (bi, pi, hp)),
                   pl.BlockSpec((1, t, LANES), lambda hp, bi, pi: (bi, n_p - 1 - pi, hp))],
        out_shape=[jax.ShapeDtypeStruct((b, s // 2, C_HEADS * C_DIM), F32)] * 2,
        scratch_shapes=[pltpu.VMEM((LANES, s), BF16)],
        compiler_params=_cparams(("arbitrary", "arbitrary", "arbitrary")),
        name="nsa_selected",
    )(hb3, hb3, hb3, hb3, tb, sel, sel)
    return _join_halves(o_a, o_b)


def _win_attention(hb3, tb, k_blk, v_blk, n_prev):
    b, s, _ = hb3.shape
    t = T_ATT
    assert s >= (n_prev + 1) * t
    return pl.pallas_call(
        functools.partial(_win_kernel, n_prev=n_prev),
        grid=(C_HEADS // 2, b, s // t),
        in_specs=[pl.BlockSpec((1, t, LANES), lambda hp, bi, qi: (bi, qi, hp)),
                  pl.BlockSpec((1, s, LANES), lambda hp, bi, qi: (bi, 0, k_blk + hp // 2)),
                  pl.BlockSpec((1, s, LANES), lambda hp, bi, qi: (bi, 0, v_blk + hp // 2)),
                  pl.BlockSpec((2, 2 * n_prev + 1, t, t), lambda hp, bi, qi: (hp, 0, 0, 0))],
        out_specs=pl.BlockSpec((1, t, LANES), lambda hp, bi, qi: (bi, qi, hp)),
        out_shape=jax.ShapeDtypeStruct((b, s, C_HEADS * C_DIM), F32),
        scratch_shapes=[pltpu.VMEM((LANES, s), BF16)],
        compiler_params=_cparams(("arbitrary", "arbitrary", "arbitrary")),
        name="nsa_window",
    )(hb3, hb3, hb3, tb)


def _rms(x, g):
    return x * lax.rsqrt(jnp.mean(x * x, axis=-1, keepdims=True) + RMS_EPS) * g


def _mla_prep_kernel(cq_ref, ckv_ref, kr1_ref, kr2_ref, cos_ref, sin_ref, gq_ref, gkv_ref,
                     wq1_ref, wq2_ref, wk_ref, wv_ref, q_ref, k_ref, v_ref):
    cqn = _rms(cq_ref[...], gq_ref[...]).astype(BF16)
    c = _rms(ckv_ref[...], gkv_ref[...]).astype(BF16)
    cos, sin = cos_ref[...], sin_ref[...]
    k_rope = kr1_ref[...] * cos + kr2_ref[...] * sin
    scale = (D_NOPE + D_ROPE) ** -0.5
    for h in range(D_HEADS):
        cols = slice(h * LANES, (h + 1) * LANES)
        q = _dot(cqn, wq1_ref[:, cols]) * cos + _dot(cqn, wq2_ref[:, cols]) * sin
        q_ref[:, cols] = (q * scale).astype(q_ref.dtype)
        k_ref[:, cols] = (_dot(c, wk_ref[:, cols]) + k_rope).astype(k_ref.dtype)
    v_ref[...] = _dot(c, wv_ref[...]).astype(v_ref.dtype)


def _mla_prep(hf, cos, sin, gq, gkv, wq1, wq2, wk, wv, seq):
    m = hf.shape[0]
    n_seq = seq // TM
    const = lambda shape: pl.BlockSpec(shape, lambda i: (0, 0))
    hq = D_HEADS * LANES
    return pl.pallas_call(
        _mla_prep_kernel,
        grid=(m // TM,),
        in_specs=[pl.BlockSpec((TM, D_Q_LORA), lambda i: (i, 1)),
                  pl.BlockSpec((TM, D_KV_LORA), lambda i: (i, 3)),
                  pl.BlockSpec((TM, LANES), lambda i: (i, 8)),
                  pl.BlockSpec((TM, LANES), lambda i: (i, 9)),
                  pl.BlockSpec((TM, LANES), lambda i: (i % n_seq, 0)),
                  pl.BlockSpec((TM, LANES), lambda i: (i % n_seq, 0)),
                  const((1, D_Q_LORA)), const((1, D_KV_LORA)),
                  const(wq1.shape), const(wq2.shape), const(wk.shape), const(wv.shape)],
        out_specs=[pl.BlockSpec((TM, hq), lambda i: (i, 0)), pl.BlockSpec((TM, hq), lambda i: (i, 0)),
                   pl.BlockSpec((TM, D_HEADS * D_V), lambda i: (i, 0))],
        out_shape=[jax.ShapeDtypeStruct((m, hq), BF16), jax.ShapeDtypeStruct((m, hq), BF16),
                   jax.ShapeDtypeStruct((m, D_HEADS * D_V), BF16)],
        compiler_params=_cparams(("arbitrary",)),
        name="mla_prep",
    )(hf, hf, hf, hf, cos, sin, gq.reshape(1, -1), gkv.reshape(1, -1), wq1, wq2, wk, wv)


def _mla_kernel(qa_ref, qb_ref, k_ref, v_ref, oa_ref, ob_ref, vt_ref, *, n_d):
    t = T_ATT
    pi = pl.program_id(2)

    @pl.when(pi == 0)
    def _():
        _fill_transposed(vt_ref, v_ref)

    head_cols = [slice(half * LANES, (half + 1) * LANES) for half in range(2)]
    key_minus_query = (lax.broadcasted_iota(jnp.int32, (2 * t, t), 0)
                       - lax.broadcasted_iota(jnp.int32, (2 * t, t), 1))

    def score(n, ctx, q_tile, i, s, diag):
        return jnp.where(key_minus_query <= (q_tile - 2 * i) * t, s, NEG_INF) if diag else s

    out_a, out_b = _flash_balanced(pi, n_d, [qa_ref[0, :, c] for c in head_cols], [qb_ref[0, :, c] for c in head_cols],
                                   [k_ref, k_ref], head_cols, vt_ref,
                                   [slice(half * D_V, (half + 1) * D_V) for half in range(2)], score)
    oa_ref[0] = jnp.concatenate(out_a, axis=0).T.astype(oa_ref.dtype)
    ob_ref[0] = jnp.concatenate(out_b, axis=0).T.astype(ob_ref.dtype)


def _mla_attention(q3, k3, v3):
    b, s, _ = q3.shape
    t = T_ATT
    n_d = s // t
    n_p = n_d // 2
    o_a, o_b = pl.pallas_call(
        functools.partial(_mla_kernel, n_d=n_d),
        grid=(D_HEADS // 2, b, n_p),
        in_specs=[pl.BlockSpec((1, t, 2 * LANES), lambda hp, bi, pi: (bi, pi, hp)),
                  pl.BlockSpec((1, t, 2 * LANES), lambda hp, bi, pi: (bi, n_d - 1 - pi, hp)),
                  pl.BlockSpec((1, s, 2 * LANES), lambda hp, bi, pi: (bi, 0, hp)),
                  pl.BlockSpec((1, s, LANES), lambda hp, bi, pi: (bi, 0, hp))],
        out_specs=[pl.BlockSpec((1, t, LANES), lambda hp, bi, pi: (bi, pi, hp)),
                   pl.BlockSpec((1, t, LANES), lambda hp, bi, pi: (bi, n_p - 1 - pi, hp))],
        out_shape=[jax.ShapeDtypeStruct((b, s // 2, D_HEADS * D_V), BF16)] * 2,
        scratch_shapes=[pltpu.VMEM((LANES, s), BF16)],
        compiler_params=_cparams(("arbitrary", "arbitrary", "arbitrary")),
        name="mla_attention",
    )(q3, q3, k3, v3)
    return _join_halves(o_a, o_b)


def _odd_out_kernel(gate_ref, oc_ref, os_ref, ow_ref, od_ref, x_ref, w_ref, g_ref, b_ref, out_ref):
    gates = 1.0 / (1.0 + jnp.exp(-gate_ref[...]))
    lo = _lane_lo(TM)
    n_c = C_HEADS * C_DIM
    y = _dot(od_ref[...], w_ref[n_c:, :])
    for blk in range(n_c // LANES):
        cols = slice(blk * LANES, (blk + 1) * LANES)
        acc = jnp.zeros((TM, LANES), F32)
        for br, o_ref in enumerate((oc_ref, os_ref, ow_ref)):
            c0 = br * C_HEADS + 2 * blk
            gexp = jnp.where(lo, gates[:, c0:c0 + 1], gates[:, c0 + 1:c0 + 2])
            acc = acc + gexp * o_ref[:, cols]
        y = y + _dot(acc.astype(BF16), w_ref[cols, :])
    out_ref[...] = _residual_ln(x_ref[...], y, g_ref[...], b_ref[...])


def _odd_out(hf, o_cmp, o_slc, o_win, o_d, x, w_out, g, beta):
    m = x.shape[0]
    n_c = C_HEADS * C_DIM
    row = lambda width: pl.BlockSpec((TM, width), lambda i: (i, 0))
    const = lambda shape: pl.BlockSpec(shape, lambda i: (0, 0))
    return pl.pallas_call(
        _odd_out_kernel,
        grid=(m // TM,),
        in_specs=[pl.BlockSpec((TM, LANES), lambda i: (i, 2)),
                  row(n_c), row(n_c), row(n_c), row(D_HEADS * D_V), row(D_MODEL),
                  const(w_out.shape), const((1, D_MODEL)), const((1, D_MODEL))],
        out_specs=row(D_MODEL),
        out_shape=jax.ShapeDtypeStruct((m, D_MODEL), F32),
        compiler_params=_cparams(("arbitrary",)),
        name="odd_out",
    )(hf, o_cmp, o_slc, o_win, o_d, x, w_out, g.reshape(1, -1), beta.reshape(1, -1))


ODD_BF16_W = 1536
ODD_F32_W = 1280


def _odd_in_columns():
    q_c = C_HEADS * C_DIM
    kv = lambda br, which, g: q_c + ((br * 2 + which) * C_KV_GROUPS + g) * C_DIM + np.arange(C_DIM)
    gate0 = q_c + 3 * 2 * C_KV_GROUPS * C_DIM
    cq0 = gate0 + 3 * C_HEADS
    ckv0 = cq0 + D_Q_LORA
    kr0 = ckv0 + D_KV_LORA
    zeros = lambda n: np.full(n, -1)
    cols = [np.arange(q_c)]
    for br in (1, 2):
        for which in (0, 1):
            for g in range(C_KV_GROUPS):
                cols += [kv(br, which, g), kv(br, which, g)]
    assert sum(len(c) for c in cols) == ODD_BF16_W
    for which in (0, 1):
        cols += [kv(0, which, 0), kv(0, which, 1)]
    cols += [gate0 + np.arange(3 * C_HEADS), zeros(LANES - 3 * C_HEADS)]
    cols += [cq0 + np.arange(D_Q_LORA), ckv0 + np.arange(D_KV_LORA)]
    half = D_ROPE // 2
    kr = kr0 + np.arange(D_ROPE)
    cols += [zeros(D_NOPE), kr, zeros(LANES - D_NOPE - D_ROPE)]
    cols += [zeros(D_NOPE), kr[half:], kr[:half], zeros(LANES - D_NOPE - D_ROPE)]
    cols = np.concatenate(cols)
    assert len(cols) == ODD_BF16_W + ODD_F32_W
    return cols


def _gather_cols(w, cols):
    return jnp.where(jnp.asarray(cols >= 0)[None, :], w[:, np.maximum(cols, 0)], 0.0)


def _mla_weight_columns():
    dq = D_NOPE + D_ROPE
    half = D_ROPE // 2
    zeros = lambda n: np.full(n, -1)
    q1, q2, k1 = [], [], []
    for h in range(D_HEADS):
        rope0 = h * dq + D_NOPE
        q1 += [h * dq + np.arange(dq), zeros(LANES - dq)]
        q2 += [zeros(D_NOPE), rope0 + half + np.arange(half), rope0 + np.arange(half), zeros(LANES - dq)]
        k1 += [h * D_NOPE + np.arange(D_NOPE), zeros(LANES - D_NOPE)]
    return np.concatenate(q1), np.concatenate(q2), np.concatenate(k1)


def _rope_tables(seq):
    half = D_ROPE // 2
    inv = ROPE_THETA ** (-jnp.arange(half, dtype=F32) / half)
    ang = jnp.arange(seq).astype(F32)[:, None] * inv
    cos, sin = jnp.cos(ang), jnp.sin(ang)
    pad = jnp.zeros((seq, LANES - D_NOPE - D_ROPE), F32)
    cos_t = jnp.concatenate([jnp.ones((seq, D_NOPE), F32), cos, cos, pad], axis=1)
    sin_t = jnp.concatenate([jnp.zeros((seq, D_NOPE), F32), -sin, sin, pad], axis=1)
    return cos_t, sin_t


def kernel(x, rel_bias, ev_w_in, ev_w_out, ev_lambda, ev_subln, od_w_in, od_w_out, od_cmp_pe, od_cmp_w1, od_cmp_w2, od_q_norm, od_kv_norm, od_w_uq, od_w_uk, od_w_uv, ffn_w_up, ffn_conv_w, ffn_conv_b, ffn_w_down, ln_g, ln_b):
    b, s, _ = x.shape
    m = b * s
    assert s % 1024 == 0 and s // B_PAIRS[-1][1] >= 2 * T_DIL and s // SLC_BLOCK <= LANES
    n_d = s // T_ATT

    tb_t = rel_bias.T.astype(F32)
    n_prev_win = -(-(WIN_SIZE - 1) // T_ATT)
    tb_causal = _bias_table(tb_t[:C_HEADS], _toeplitz_idx(1, n_d, T_ATT, s))
    tb_win = _bias_table(tb_t[:C_HEADS], _toeplitz_idx(n_prev_win, n_prev_win + 1, T_ATT, WIN_SIZE - 1))
    tb_cmp = _bias_table(tb_t[:C_HEADS], _cmp_idx(s)).reshape(C_HEADS, s, s // CMP_STRIDE)
    tb_dil = [_bias_table(tb_t[A_HEADS + i * B_HEADS:A_HEADS + (i + 1) * B_HEADS], _dilated_idx(d))
              for i, (_, d) in enumerate(B_PAIRS)]
    cos_t, sin_t = _rope_tables(s)
    odd_cols = _odd_in_columns()
    q1_cols, q2_cols, k1_cols = _mla_weight_columns()

    xf = x.reshape(m, D_MODEL)
    for l in range(DEPTH):
        i = l // 2
        if l % 2 == 0:
            n_a = EVEN_IN - EVEN_DILATED
            h_a, h_b = _inproj(xf, ev_w_in[i].astype(BF16), (n_a, EVEN_DILATED), (BF16, F32), 768)
            h3 = h_a.reshape(b, s, n_a)
            hd3 = h_b.reshape(b, s, EVEN_DILATED)
            lam_init = 0.8 - 0.6 * math.exp(-0.3 * l)
            o_a = _diff_attention(h3, tb_causal, ev_lambda[i].astype(F32), ev_subln[i], lam_init)
            obs, lses = [], []
            for p_idx, (_, d) in enumerate(B_PAIRS):
                o, lse = _dilated_attention(hd3, tb_dil[p_idx], p_idx, d)
                obs.append(o)
                lses.append(lse)
            xf = _even_out(o_a.reshape(m, -1), obs, lses, xf, ev_w_out[i].astype(BF16), ln_g[l, 0], ln_b[l, 0])
        else:
            w_in = _gather_cols(od_w_in[i], odd_cols).astype(BF16)
            hb, hf = _inproj(xf, w_in, (ODD_BF16_W, ODD_F32_W), (BF16, F32), 256)
            hb3 = hb.reshape(b, s, ODD_BF16_W)
            n_ch = s // CMP_STRIDE
            chunks = (hf[:, :4 * C_DIM].reshape(b, n_ch, CMP_STRIDE, 4, C_DIM)
                      .transpose(0, 3, 1, 2, 4).reshape(b, 4, n_ch, CMP_STRIDE * C_DIM))
            pe = od_cmp_pe[i].reshape(2, 2, CMP_STRIDE * C_DIM)
            w2d = jnp.concatenate([od_cmp_w2[i], od_cmp_w2[i]], axis=-1).astype(BF16)
            kvc = _compress(chunks, pe, od_cmp_w1[i].astype(BF16), w2d)
            o_cmp, sel = _cmp_attention(hb3, kvc, tb_cmp)
            o_slc = _slc_attention(hb3, tb_causal, 4, 6, sel)
            o_win = _win_attention(hb3, tb_win, 8, 10, n_prev_win)
            q_d, k_d, v_d = _mla_prep(hf, cos_t, sin_t, od_q_norm[i], od_kv_norm[i],
                                      _gather_cols(od_w_uq[i], q1_cols).astype(BF16),
                                      _gather_cols(od_w_uq[i], q2_cols).astype(BF16),
                                      _gather_cols(od_w_uk[i], k1_cols).astype(BF16),
                                      od_w_uv[i].astype(BF16), s)
            o_d = _mla_attention(q_d.reshape(b, s, -1), k_d.reshape(b, s, -1), v_d.reshape(b, s, -1))
            xf = _odd_out(hf, o_cmp.reshape(m, -1), o_slc.reshape(m, -1), o_win.reshape(m, -1),
                          o_d.reshape(m, -1), xf, od_w_out[i].astype(BF16), ln_g[l, 0], ln_b[l, 0])
        xf = _ffn(xf, ffn_w_up[l].astype(BF16), ffn_conv_w[l], ffn_conv_b[l], ffn_w_down[l].astype(BF16),
                  ln_g[l, 1], ln_b[l, 1], s)
    return xf.reshape(b, s, D_MODEL)
```

```python
import functools
import math

import numpy as np
import jax
import jax.numpy as jnp
from jax import lax
from jax.experimental import pallas as pl
from jax.experimental.pallas import tpu as pltpu

F32 = jnp.float32
BF16 = jnp.bfloat16

D_MODEL = 1024
DEPTH = 4
LN_EPS = 1e-5
RMS_EPS = 1e-5
N_BUCKETS = 32
MAX_DISTANCE = 2048
A_HEADS, A_QK, A_V = 4, 64, 128
B_PAIRS = ((128, 1), (512, 4), (2048, 16))
B_HEADS, B_DIM = 4, 64
C_HEADS, C_KV_GROUPS, C_DIM = 8, 2, 64
CMP_LEN, CMP_STRIDE, CMP_HIDDEN = 32, 16, 256
SLC_BLOCK, SLC_TOP_N = 64, 16
SLC_SHIFT = 6
WIN_SIZE = 512
D_HEADS, D_Q_LORA, D_KV_LORA, D_NOPE, D_ROPE, D_V = 8, 384, 256, 64, 32, 64
ROPE_THETA = 10000.0
D_FF = 2816
EVEN_DILATED = len(B_PAIRS) * 3 * B_HEADS * B_DIM
EVEN_IN = 2 * A_HEADS * A_QK * 2 + A_HEADS * A_V + EVEN_DILATED
ALPHA = (2 * DEPTH) ** 0.25

LANES = 128
HALF_LANE_SHIFT = 6
VMEM_LIMIT = 56 * 1024 * 1024
T_ATT = 256
T_DIL = 128
TM = 512
FF_CHUNK = 256
NEG_INF = float("-inf")
M_INIT = -1e30
SEL_PENALTY = 2.0 ** 101


def _cparams(sem):
    return pltpu.CompilerParams(dimension_semantics=sem, vmem_limit_bytes=VMEM_LIMIT)


def _dot(a, b):
    return jnp.dot(a, b, preferred_element_type=F32)


def _dot_nt(a, b):
    return lax.dot_general(a, b, (((1,), (1,)), ((), ())), preferred_element_type=F32)


def _bucket_np(dist):
    n = np.maximum(dist, 0)
    nf = np.maximum(n, 1).astype(np.float32)
    max_exact = N_BUCKETS // 2
    large = max_exact + (np.log(nf / max_exact) / math.log(MAX_DISTANCE / max_exact)
                         * (N_BUCKETS - max_exact)).astype(np.int32)
    return np.where(n < max_exact, n, np.minimum(large, N_BUCKETS - 1)).astype(np.int32)


def _toeplitz_idx(n_masked, n_delta, t, max_dist):
    key = np.arange(t)[:, None]
    query = np.arange(t)[None, :]
    out = []
    for delta in range(-n_masked, n_delta):
        dist = t * delta + query - key
        ok = (dist >= 0) & (dist <= max_dist)
        out.append(np.where(ok, _bucket_np(dist), -1))
    return np.stack(out).astype(np.int32)


def _dilated_idx(d):
    t = T_DIL
    r = np.arange(t)[:, None]
    c = np.arange(2 * t)[None, :]
    dist0 = r - c
    ok0 = (dist0 >= 0) & (c < t)
    dist1 = t + r - c
    ok1 = (dist1 >= 0) & (dist1 <= t)
    return np.stack([np.where(ok0, _bucket_np(dist0 * d), -1),
                     np.where(ok1, _bucket_np(dist1 * d), -1)]).astype(np.int32)


def _cmp_idx(seq):
    ncb = seq // CMP_STRIDE - 1
    q = np.arange(seq)[:, None]
    c = np.arange(seq // CMP_STRIDE)[None, :]
    dist = q - (c * CMP_STRIDE + CMP_LEN - 1)
    ok = (dist >= 0) & (c < ncb)
    return np.where(ok, _bucket_np(dist), -1).astype(np.int32).reshape(seq // T_ATT, T_ATT, seq // CMP_STRIDE)


def _bias_kernel(tbl_ref, idx_ref, o_ref):
    h = pl.program_id(0)
    idx = idx_ref[0]
    out = jnp.full(idx.shape, NEG_INF, F32)
    for b in range(N_BUCKETS):
        out = jnp.where(idx == b, tbl_ref[h, b], out)
    o_ref[0, 0] = out


def _bias_table(tbl, idx):
    n_h = tbl.shape[0]
    n, r, c = idx.shape
    return pl.pallas_call(
        _bias_kernel,
        grid=(n_h, n),
        in_specs=[pl.BlockSpec(memory_space=pltpu.SMEM),
                  pl.BlockSpec((1, r, c), lambda h, i: (i, 0, 0))],
        out_specs=pl.BlockSpec((1, 1, r, c), lambda h, i: (h, i, 0, 0)),
        out_shape=jax.ShapeDtypeStruct((n_h, n, r, c), F32),
        compiler_params=_cparams(("arbitrary", "arbitrary")),
        name="bias_table",
    )(tbl, jnp.asarray(idx))


def _inproj_kernel(x_ref, w_ref, *o_refs, widths, chunk):
    xb = x_ref[...].astype(BF16)
    off = 0
    for o_ref, width in zip(o_refs, widths):
        for c0 in range(0, width, chunk):
            o_ref[:, c0:c0 + chunk] = _dot(xb, w_ref[:, off + c0:off + c0 + chunk]).astype(o_ref.dtype)
        off += width


def _inproj(x, w, widths, dtypes, chunk):
    m = x.shape[0]
    n = w.shape[1]
    assert sum(widths) == n and all(wd % chunk == 0 for wd in widths)
    return pl.pallas_call(
        functools.partial(_inproj_kernel, widths=widths, chunk=chunk),
        grid=(m // TM,),
        in_specs=[pl.BlockSpec((TM, D_MODEL), lambda i: (i, 0)),
                  pl.BlockSpec((D_MODEL, n), lambda i: (0, 0))],
        out_specs=[pl.BlockSpec((TM, wd), lambda i: (i, 0)) for wd in widths],
        out_shape=[jax.ShapeDtypeStruct((m, wd), dt) for wd, dt in zip(widths, dtypes)],
        compiler_params=_cparams(("arbitrary",)),
        name="in_proj",
    )(x, w)


def _tree_reduce_rows(x, combine, reduce_fn):
    parts = [x[r:r + 8] for r in range(0, x.shape[0], 8)]
    while len(parts) > 1:
        parts = [combine(parts[k], parts[k + 1]) for k in range(0, len(parts) - 1, 2)] + \
                ([parts[-1]] if len(parts) % 2 else [])
    return reduce_fn(parts[0], axis=0, keepdims=True)


def _flash_init(streams):
    out = []
    for q, _, _, _, vt_rows in streams:
        t_q = q.shape[0]
        out.append((jnp.full((1, t_q), M_INIT, F32), jnp.zeros((1, t_q), F32),
                    jnp.zeros((vt_rows.stop - vt_rows.start, t_q), F32)))
    return tuple(out)


def _flash_scores(streams, key_start, n_keys):
    keys = pl.ds(pl.multiple_of(key_start, T_ATT), n_keys)
    return tuple(_dot_nt(k_ref[0, keys, k_cols], q) for q, k_ref, k_cols, _, _ in streams)


def _flash_update(streams, score_fns, key_start, n_keys, i, raw, state):
    keys = pl.ds(pl.multiple_of(key_start, T_ATT), n_keys)
    scores = [fn(i, s) for fn, s in zip(score_fns, raw)]
    m_new = [jnp.maximum(m, _tree_reduce_rows(s, jnp.maximum, jnp.max)) for s, (m, _, _) in zip(scores, state)]
    probs = [jnp.exp(s - mn) for s, mn in zip(scores, m_new)]
    out = []
    for (_, _, _, vt_ref, vt_rows), p, mn, (m, l, acc) in zip(streams, probs, m_new, state):
        alpha = jnp.exp(m - mn)
        l = alpha * l + _tree_reduce_rows(p, jnp.add, jnp.sum)
        acc = alpha * acc + _dot(vt_ref[vt_rows, keys], p.astype(BF16))
        out.append((mn, l, acc))
    return tuple(out)


def _flash_finish(state):
    return [acc / l for _, l, acc in state]


def _flash_balanced(pi, n_d, qa, qb, k_refs, k_cols, vt_ref, vt_rows, score, key_extra=None):
    kt = 2 * T_ATT
    t_q = qa[0].shape[0]
    n_streams = len(qa)
    n_slots = n_d // 2 + 1
    max_a = n_d // 4
    n_a = pi // 2 + 1
    tile_a, tile_b = pi, n_d - 1 - pi

    def slot(s):
        if s == 0:
            return True, tile_a, n_a - 1
        if s >= max_a:
            return False, tile_b, s - n_a
        is_a = s < n_a
        return is_a, jnp.where(is_a, tile_a, tile_b), jnp.where(is_a, n_a - 1 - s, s - n_a)

    def pick(is_a, a, b):
        return (a if is_a else b) if isinstance(is_a, bool) else jnp.where(is_a, a, b)

    def keys_of(i):
        return pl.ds(pl.multiple_of(i * kt, kt), kt)

    def raw_scores(s):
        is_a, _, i = slot(s)
        extra = None if key_extra is None else key_extra(i)
        out = []
        for n in range(n_streams):
            keys = k_refs[n][0, keys_of(i), k_cols[n]]
            if extra is not None:
                keys = jnp.concatenate([keys, extra], axis=1)
            out.append(_dot_nt(keys, pick(is_a, qa[n], qb[n])))
        return out

    state = [(jnp.full((1, t_q), M_INIT, F32), jnp.zeros((1, t_q), F32),
              jnp.zeros((vt_rows[n].stop - vt_rows[n].start, t_q), F32)) for n in range(n_streams)]
    out_a = [jnp.zeros_like(acc) for _, _, acc in state]
    pending = None
    raw = raw_scores(0)

    def fold(state, pending):
        alphas, probs, i_prev = pending
        return [(m, l, alphas[n] * acc + _dot(vt_ref[vt_rows[n], keys_of(i_prev)], probs[n]))
                for n, (m, l, acc) in enumerate(state)]

    for s in range(n_slots):
        raw_next = raw_scores(s + 1) if s + 1 < n_slots else None
        is_a, q_tile, i = slot(s)
        if pending is not None:
            state = fold(state, pending)
        if 1 <= s <= max_a:
            switch = s == n_a
            out_a = [jnp.where(switch, acc / l, o) for (_, l, acc), o in zip(state, out_a)]
            state = [(jnp.where(switch, M_INIT, m), jnp.where(switch, 0.0, l), jnp.where(switch, 0.0, acc))
                     for m, l, acc in state]
        diag = s == 0 or s == n_slots - 1
        scores = [score(n, q_tile, i, raw[n], diag) for n in range(n_streams)]
        m_new = [jnp.maximum(m, _tree_reduce_rows(sc, jnp.maximum, jnp.max)) for sc, (m, _, _) in zip(scores, state)]
        probs = [jnp.exp(sc - mn) for sc, mn in zip(scores, m_new)]
        alphas = [jnp.exp(m - mn) for (m, _, _), mn in zip(state, m_new)]
        state = [(mn, al * l + _tree_reduce_rows(p, jnp.add, jnp.sum), acc)
                 for mn, al, p, (_, l, acc) in zip(m_new, alphas, probs, state)]
        pending = (alphas, [p.astype(BF16) for p in probs], i)
        raw = raw_next
    state = fold(state, pending)
    return out_a, [acc / l for _, l, acc in state]


def _pair_bias(tb_ref, h, q_tile, i, s):
    t = T_ATT
    d0 = q_tile - 2 * i
    return jnp.concatenate([s[:t] + tb_ref[h, d0 + 1], s[t:] + tb_ref[h, d0]], axis=0)


def _fill_transposed(vt_ref, v_ref):
    n = v_ref.shape[1]
    step = 2 * T_ATT
    for c0 in range(0, n, step):
        vt_ref[:, c0:c0 + step] = v_ref[0, c0:c0 + step, :].astype(F32).T.astype(BF16)


def _lane_lo(rows):
    return lax.broadcasted_iota(jnp.int32, (rows, LANES), 1) < (LANES // 2)


def _diff_kernel(lam_ref, g_ref, q0a_ref, q0b_ref, q1a_ref, q1b_ref, k0_ref, k1_ref, v_ref, tb_ref,
                 oa_ref, ob_ref, vt_ref, *, lam_init, n_d):
    h = pl.program_id(0)
    pi = pl.program_id(2)

    @pl.when(pi == 0)
    def _():
        _fill_transposed(vt_ref, v_ref)

    lp = lam_ref[...]
    lam = (jnp.exp(jnp.sum(lp[0:1] * lp[1:2], axis=-1, keepdims=True))
           - jnp.exp(jnp.sum(lp[2:3] * lp[3:4], axis=-1, keepdims=True)) + lam_init)
    keep = jnp.right_shift(lax.broadcasted_iota(jnp.int32, (T_ATT, LANES), 1), HALF_LANE_SHIFT) == h % 2

    def prep(q_ref):
        q = q_ref[0] * (A_QK ** -0.5)
        return jnp.where(keep, q, jnp.zeros_like(q))

    full = slice(0, LANES)

    def score(n, q_tile, i, s, diag):
        return _pair_bias(tb_ref, 0, q_tile, i, s)

    out_a, out_b = _flash_balanced(pi, n_d, [prep(q0a_ref), prep(q1a_ref)], [prep(q0b_ref), prep(q1b_ref)],
                                   [k0_ref, k1_ref], [full, full], vt_ref, [slice(0, A_V)] * 2, score)
    for outs, o_ref in ((out_a, oa_ref), (out_b, ob_ref)):
        d = (outs[0] - lam * outs[1]).T
        ms = jnp.mean(d * d, axis=-1, keepdims=True)
        o_ref[0] = (d * lax.rsqrt(ms + RMS_EPS) * g_ref[...] * (1.0 - lam_init)).astype(o_ref.dtype)


def _join_halves(o_a, o_b):
    return jnp.concatenate([o_a, o_b], axis=1)


def _diff_attention(h3, tb, lam_p, subln_g, lam_init):
    b, s, _ = h3.shape
    t = T_ATT
    n_d = s // t
    n_p = n_d // 2
    q_spec = lambda col0, second: pl.BlockSpec(
        (1, t, LANES), lambda h, bi, pi: (bi, (n_d - 1 - pi) if second else pi, col0 + h // 2))
    kv_spec = lambda col_fn: pl.BlockSpec((1, s, LANES), lambda h, bi, pi: (bi, 0, col_fn(h)))
    o_a, o_b = pl.pallas_call(
        functools.partial(_diff_kernel, lam_init=lam_init, n_d=n_d),
        grid=(A_HEADS, b, n_p),
        in_specs=[pl.BlockSpec((4, A_QK), lambda h, bi, pi: (0, 0)),
                  pl.BlockSpec((1, A_V), lambda h, bi, pi: (0, 0)),
                  q_spec(0, False), q_spec(0, True), q_spec(2, False), q_spec(2, True),
                  kv_spec(lambda h: 4 + h // 2), kv_spec(lambda h: 6 + h // 2), kv_spec(lambda h: 8 + h),
                  pl.BlockSpec((1, n_d + 1, t, t), lambda h, bi, pi: (h, 0, 0, 0))],
        out_specs=[pl.BlockSpec((1, t, A_V), lambda h, bi, pi: (bi, pi, h)),
                   pl.BlockSpec((1, t, A_V), lambda h, bi, pi: (bi, n_p - 1 - pi, h))],
        out_shape=[jax.ShapeDtypeStruct((b, s // 2, A_HEADS * A_V), BF16)] * 2,
        scratch_shapes=[pltpu.VMEM((A_V, s), BF16)],
        compiler_params=_cparams(("arbitrary", "arbitrary", "arbitrary")),
        name="diff_attention",
    )(lam_p, subln_g.reshape(1, A_V), h3, h3, h3, h3, h3, h3, h3, tb)
    return _join_halves(o_a, o_b)


def _dilated_kernel(q_ref, k_ref, v_ref, tb_ref, o_ref, lse_ref, *, d):
    r = pl.program_id(2)
    t = T_DIL
    n_blocks = q_ref.shape[1] // (d * t)
    lo = _lane_lo(t)
    scale = B_DIM ** -0.5

    def block(bi, carry):
        var = jnp.minimum(bi, 1)
        qrows = pl.ds(r + d * t * bi, t, stride=d)
        krows = pl.ds(r + d * t * jnp.maximum(bi - 1, 0), 2 * t, stride=d)
        q = (q_ref[0, qrows, :] * scale).astype(BF16)
        kk = k_ref[0, krows, :].astype(BF16)
        vv = v_ref[0, krows, :].astype(BF16)
        o_half, l_half = [], []
        for half in range(2):
            keep = lo if half == 0 else jnp.logical_not(lo)
            qm = jnp.where(keep, q, jnp.zeros_like(q))
            s = _dot_nt(qm, kk) + tb_ref[half, var]
            m = jnp.max(s, axis=-1, keepdims=True)
            p = jnp.exp(s - m)
            l = jnp.sum(p, axis=-1, keepdims=True)
            o_half.append(_dot(p.astype(BF16), vv) / l)
            l_half.append(m + jnp.log(l))
        o_ref[0, qrows, :] = jnp.where(lo, o_half[0], o_half[1])
        lse_ref[0, qrows, :] = jnp.where(lo, l_half[0], l_half[1])
        return carry

    lax.fori_loop(0, n_blocks, block, 0, unroll=min(n_blocks, 4))


def _dilated_attention(hd3, tb, pair_idx, d):
    b, s, _ = hd3.shape
    hd = B_HEADS * B_DIM
    base = pair_idx * 3 * (hd // LANES)
    seq = lambda which: pl.BlockSpec((1, s, LANES), lambda bi, hp, r: (bi, 0, base + which * (hd // LANES) + hp))
    out = pl.BlockSpec((1, s, LANES), lambda bi, hp, r: (bi, 0, hp))
    o, lse = pl.pallas_call(
        functools.partial(_dilated_kernel, d=d),
        grid=(b, hd // LANES, d),
        in_specs=[seq(0), seq(1), seq(2),
                  pl.BlockSpec((2, 2, T_DIL, 2 * T_DIL), lambda bi, hp, r: (hp, 0, 0, 0))],
        out_specs=[out, out],
        out_shape=[jax.ShapeDtypeStruct((b, s, hd), F32)] * 2,
        compiler_params=_cparams(("arbitrary", "arbitrary", "arbitrary")),
        name="dilated_attention",
    )(hd3, hd3, hd3, tb)
    return o.reshape(b * s, hd), lse.reshape(b * s, hd)


def _residual_ln(x, y, g, beta):
    z = ALPHA * x + y
    mu = jnp.mean(z, axis=-1, keepdims=True)
    zc = z - mu
    var = jnp.mean(zc * zc, axis=-1, keepdims=True)
    return zc * lax.rsqrt(var + LN_EPS) * g + beta


def _even_out_kernel(oa_ref, o0_ref, o1_ref, o2_ref, l0_ref, l1_ref, l2_ref, x_ref, w_ref, g_ref, b_ref, out_ref):
    l0, l1, l2 = l0_ref[...], l1_ref[...], l2_ref[...]
    mx = jnp.maximum(jnp.maximum(l0, l1), l2)
    e0, e1, e2 = jnp.exp(l0 - mx), jnp.exp(l1 - mx), jnp.exp(l2 - mx)
    den = e0 + e1 + e2
    ob = (e0 / den) * o0_ref[...] + (e1 / den) * o1_ref[...] + (e2 / den) * o2_ref[...]
    n_a = A_HEADS * A_V
    y = _dot(oa_ref[...], w_ref[0:n_a, :]) + _dot(ob.astype(BF16), w_ref[n_a:, :])
    out_ref[...] = _residual_ln(x_ref[...], y, g_ref[...], b_ref[...])


def _even_out(oa, obs, lses, x, w_out, g, beta):
    m = x.shape[0]
    hd = B_HEADS * B_DIM
    row = lambda width: pl.BlockSpec((TM, width), lambda i: (i, 0))
    const = lambda shape: pl.BlockSpec(shape, lambda i: (0, 0))
    return pl.pallas_call(
        _even_out_kernel,
        grid=(m // TM,),
        in_specs=[row(A_HEADS * A_V)] + [row(hd)] * 6 + [row(D_MODEL), const(w_out.shape),
                                                        const((1, D_MODEL)), const((1, D_MODEL))],
        out_specs=row(D_MODEL),
        out_shape=jax.ShapeDtypeStruct((m, D_MODEL), F32),
        compiler_params=_cparams(("arbitrary",)),
        name="even_out",
    )(oa, *obs, *lses, x, w_out, g.reshape(1, -1), beta.reshape(1, -1))


def _gelu(x):
    return 0.5 * x * (1.0 + jnp.tanh(math.sqrt(2.0 / math.pi) * (x + 0.044715 * (x * x * x))))


def _ffn_kernel(x_ref, wu_ref, cw_ref, cb_ref, wd_ref, g_ref, b_ref, out_ref, u_ref, gs_ref, tail_ref,
                *, tiles_per_seq):
    halo = 8
    n_chunks = D_FF // FF_CHUNK

    @pl.when(pl.program_id(0) % tiles_per_seq == 0)
    def _():
        tail_ref[...] = jnp.zeros(tail_ref.shape, F32)

    x = x_ref[...]
    xb = x.astype(BF16)

    def up(c):
        cols = slice(c * FF_CHUNK, (c + 1) * FF_CHUNK)
        return _dot(xb, wu_ref[:, cols]), _dot(xb, wu_ref[:, D_FF + c * FF_CHUNK:D_FF + (c + 1) * FF_CHUNK])

    def activate(c, a, gate):
        cols = slice(c * FF_CHUNK, (c + 1) * FF_CHUNK)
        gs = gs_ref.at[c % 2]
        gs[0:halo, :] = tail_ref[:, cols]
        gs[halo:, :] = gate
        tail_ref[:, cols] = gate[TM - halo:, :]
        conv = (gs[pl.ds(halo - 2, TM), :] * cw_ref[0:1, cols] + gs[pl.ds(halo - 1, TM), :] * cw_ref[1:2, cols]
                + gate * cw_ref[2:3, cols] + cb_ref[:, cols])
        u_ref[:, cols] = (_gelu(conv) * a).astype(BF16)

    nxt = up(0)
    for c in range(n_chunks):
        cur = nxt
        if c + 1 < n_chunks:
            nxt = up(c + 1)
        activate(c, *cur)
    out_ref[...] = _residual_ln(x, _dot(u_ref[...], wd_ref[...]), g_ref[...], b_ref[...])


def _ffn(x, w_up, conv_w, conv_b, w_down, g, beta, seq):
    m = x.shape[0]
    const = lambda shape: pl.BlockSpec(shape, lambda i: (0, 0))
    return pl.pallas_call(
        functools.partial(_ffn_kernel, tiles_per_seq=seq // TM),
        grid=(m // TM,),
        in_specs=[pl.BlockSpec((TM, D_MODEL), lambda i: (i, 0)), const(w_up.shape), const(conv_w.shape),
                  const((1, D_FF)), const(w_down.shape), const((1, D_MODEL)), const((1, D_MODEL))],
        out_specs=pl.BlockSpec((TM, D_MODEL), lambda i: (i, 0)),
        out_shape=jax.ShapeDtypeStruct((m, D_MODEL), F32),
        scratch_shapes=[pltpu.VMEM((TM, D_FF), BF16), pltpu.VMEM((2, TM + 8, FF_CHUNK), F32),
                        pltpu.VMEM((8, D_FF), F32)],
        compiler_params=_cparams(("arbitrary",)),
        name="conv_ffn",
    )(x, w_up, conv_w, conv_b.reshape(1, -1), w_down, g.reshape(1, -1), beta.reshape(1, -1))


def _compress_kernel(ch_ref, pe_ref, w1_ref, w2_ref, o_ref):
    half = CMP_STRIDE * C_DIM
    ch = ch_ref[0, 0]
    a = _dot((ch + pe_ref[0, 0:1, :]).astype(BF16), w1_ref[0, 0:half, :])
    b = _dot((ch + pe_ref[0, 1:2, :]).astype(BF16), w1_ref[0, half:, :])
    n = ch.shape[0]
    hid = _gelu(a + pltpu.roll(b, n - 1, 0))
    o_ref[0, 0] = _dot(hid.astype(BF16), w2_ref[0]).astype(o_ref.dtype)


def _compress(chunks, pe, w1, w2d):
    b, _, n, width = chunks.shape
    return pl.pallas_call(
        _compress_kernel,
        grid=(b, 4),
        in_specs=[pl.BlockSpec((1, 1, n, width), lambda bi, j: (bi, j, 0, 0)),
                  pl.BlockSpec((1, 2, width), lambda bi, j: (j // 2, 0, 0)),
                  pl.BlockSpec((1, 2 * width, CMP_HIDDEN), lambda bi, j: (j // 2, 0, 0)),
                  pl.BlockSpec((1, CMP_HIDDEN, LANES), lambda bi, j: (j // 2, 0, 0))],
        out_specs=pl.BlockSpec((1, 1, n, LANES), lambda bi, j: (bi, j, 0, 0)),
        out_shape=jax.ShapeDtypeStruct((b, 4, n, LANES), BF16),
        compiler_params=_cparams(("arbitrary", "arbitrary")),
        name="nsa_compress",
    )(chunks, pe, w1, w2d)


def _cmp_kernel(q_ref, kc_ref, vc_ref, tb_ref, o_ref, sel_ref, pg_ref, sc_ref, pen_ref):
    t = T_ATT
    qi = pl.program_id(1)
    n_c = kc_ref.shape[2]
    n_sel = n_c // (SLC_BLOCK // CMP_STRIDE)
    lo = _lane_lo(t)
    kc = kc_ref[0, 0]
    vc = vc_ref[0, 0]
    pg = jnp.zeros((t, n_c), F32)
    for pair in range(2):
        cols = slice(pair * LANES, (pair + 1) * LANES)
        q = q_ref[0, :, cols]
        o_half = []
        for half in range(2):
            keep = lo if half == 0 else jnp.logical_not(lo)
            qm = jnp.where(keep, q * (C_DIM ** -0.5), jnp.zeros_like(q))
            s = _dot_nt(qm, kc) + tb_ref[2 * pair + half]
            m = jnp.maximum(jnp.max(s, axis=-1, keepdims=True), M_INIT)
            p = jnp.exp(s - m)
            den = jnp.sum(p, axis=-1, keepdims=True)
            p = p / jnp.where(den > 0, den, 1.0)
            o_half.append(_dot(p.astype(BF16), vc))
            pg = pg + p
        o_ref[0, :, cols] = jnp.where(lo, o_half[0], o_half[1])
    pad = 8
    pg_t = pg.T
    for c in range(t // LANES):
        pg_ref[c, 0:pad, :] = jnp.zeros((pad, LANES), F32)
        pg_ref[c, pad:pad + n_c, :] = pg_t[:, c * LANES:(c + 1) * LANES]
        pg_ref[c, pad + n_c:, :] = jnp.zeros((pad, LANES), F32)
    r = SLC_BLOCK // CMP_STRIDE
    tap = lambda k: jnp.concatenate(
        [pg_ref[c, pl.ds(pad + k, n_sel, stride=r), :] for c in range(t // LANES)], axis=1)
    score = (0.5 * tap(-1) + ((tap(0) + tap(1)) + tap(2))) + 0.5 * tap(3)
    jb = lax.broadcasted_iota(jnp.int32, (n_sel, t), 0)
    qblk = jnp.right_shift(qi * t + lax.broadcasted_iota(jnp.int32, (n_sel, t), 1), SLC_SHIFT)
    forced = (jb == 0) | (jb == qblk) | (jb == qblk - 1)
    sc = jnp.where(forced, jnp.inf, jnp.where(jb <= qblk, score, NEG_INF))
    sc_ref[...] = sc
    pen_ref[...] = jnp.zeros(pen_ref.shape, F32)
    groups = range(0, n_sel, 8)
    sc_g = [sc_ref[g0:g0 + 8, :] for g0 in groups]
    sub = lax.broadcasted_iota(jnp.int32, (8, t), 0)
    cnt_g = [jnp.zeros((8, t), jnp.int32) for _ in groups]
    for i in range(n_sel):
        row = jnp.broadcast_to(sc_ref[pl.ds(i, 1), :], (8, t))
        for k, g0 in enumerate(groups):
            if i < g0:
                beats = row >= sc_g[k]
            elif i >= g0 + 8:
                beats = row > sc_g[k]
            else:
                beats = (row > sc_g[k]) | ((row == sc_g[k]) & (i - g0 < sub))
            cnt_g[k] = cnt_g[k] + beats.astype(jnp.int32)
    for k, g0 in enumerate(groups):
        pen_ref[g0:g0 + 8, :] = jnp.where(cnt_g[k] < SLC_TOP_N, 0.0, -SEL_PENALTY)
    sel_ref[0, 0] = pen_ref[...].T.astype(sel_ref.dtype)


def _cmp_attention(hb3, kvc, tb):
    b, s, _ = hb3.shape
    t = T_ATT
    n_c = s // CMP_STRIDE
    n_sel = s // SLC_BLOCK
    assert n_sel <= LANES
    gw = (C_HEADS // C_KV_GROUPS) * C_DIM
    return pl.pallas_call(
        _cmp_kernel,
        grid=(C_KV_GROUPS, s // t, b),
        in_specs=[pl.BlockSpec((1, t, gw), lambda g, qi, bi: (bi, qi, g)),
                  pl.BlockSpec((1, 1, n_c, LANES), lambda g, qi, bi: (bi, g, 0, 0)),
                  pl.BlockSpec((1, 1, n_c, LANES), lambda g, qi, bi: (bi, 2 + g, 0, 0)),
                  pl.BlockSpec((4, t, n_c), lambda g, qi, bi: (g, qi, 0))],
        out_specs=[pl.BlockSpec((1, t, gw), lambda g, qi, bi: (bi, qi, g)),
                   pl.BlockSpec((1, 1, t, LANES), lambda g, qi, bi: (bi, g, qi, 0))],
        out_shape=[jax.ShapeDtypeStruct((b, s, C_HEADS * C_DIM), F32),
                   jax.ShapeDtypeStruct((b, C_KV_GROUPS, s, LANES), BF16)],
        scratch_shapes=[pltpu.VMEM((t // LANES, n_c + 16, LANES), F32), pltpu.VMEM((n_sel, t), F32),
                        pltpu.VMEM((LANES, t), F32)],
        compiler_params=_cparams(("arbitrary", "arbitrary", "arbitrary")),
        name="nsa_cmp_attention",
    )(hb3, kvc, kvc, tb)


def _head_pair_queries(q_ref):
    lo = _lane_lo(T_ATT)
    q = q_ref[0] * (C_DIM ** -0.5)
    zero = jnp.zeros_like(q)
    return [jnp.where(lo, q, zero), jnp.where(lo, zero, q)]


def _slc_kernel(qa_ref, qb_ref, k_ref, v_ref, tb_ref, sela_ref, selb_ref, oa_ref, ob_ref, vt_ref, *, n_d):
    t = T_ATT
    pi = pl.program_id(2)

    @pl.when(pi == 0)
    def _():
        _fill_transposed(vt_ref, v_ref)

    key_blk = jnp.right_shift(lax.broadcasted_iota(jnp.int32, (2 * t, LANES), 0), SLC_SHIFT)
    blk_slot = lax.broadcasted_iota(jnp.int32, (2 * t, LANES), 1)

    def one_hot_block(i):
        return jnp.where(blk_slot == key_blk + i * (2 * t // SLC_BLOCK), 1.0, 0.0).astype(BF16)

    def with_penalty(q_ref, sel_ref):
        return [jnp.concatenate([q, sel_ref[0, 0]], axis=1) for q in _head_pair_queries(q_ref)]

    def score(n, q_tile, i, s, diag):
        return _pair_bias(tb_ref, n, q_tile, i, s)

    full = slice(0, LANES)
    out_a, out_b = _flash_balanced(pi, n_d, with_penalty(qa_ref, sela_ref), with_penalty(qb_ref, selb_ref),
                                   [k_ref, k_ref], [full, full], vt_ref, [slice(0, C_DIM)] * 2, score,
                                   key_extra=one_hot_block)
    oa_ref[0] = jnp.concatenate(out_a, axis=0).T
    ob_ref[0] = jnp.concatenate(out_b, axis=0).T


def _win_kernel(q_ref, k_ref, v_ref, tb_ref, o_ref, vt_ref, *, n_prev, tiles_per_step):
    t = T_ATT
    step = pl.program_id(2)

    @pl.when(step == 0)
    def _():
        _fill_transposed(vt_ref, v_ref)

    full = slice(0, LANES)
    n_keys = (n_prev + 1) * t
    lo = _lane_lo(t)
    work = []
    for u_tile in range(tiles_per_step):
        qi = step * tiles_per_step + u_tile
        q = q_ref[0, u_tile * t:(u_tile + 1) * t, :] * (C_DIM ** -0.5)
        zero = jnp.zeros_like(q)
        streams = [(qh, k_ref, full, vt_ref, slice(0, C_DIM)) for qh in (jnp.where(lo, q, zero), jnp.where(lo, zero, q))]
        j0 = jnp.maximum(qi - n_prev, 0)
        work.append((qi, j0, streams, _flash_scores(streams, j0 * t, n_keys)))
    for u_tile, (qi, j0, streams, raw) in enumerate(work):
        def score_fn(h, qi=qi, j0=j0):
            def fn(_, s):
                return jnp.concatenate([s[u * t:(u + 1) * t] + tb_ref[h, qi - j0 - u + n_prev]
                                        for u in range(n_prev + 1)], axis=0)
            return fn

        state = _flash_update(streams, [score_fn(0), score_fn(1)], j0 * t, n_keys, 0, raw, _flash_init(streams))
        o_ref[0, u_tile * t:(u_tile + 1) * t, :] = jnp.concatenate(_flash_finish(state), axis=0).T


def _slc_attention(hb3, tb, k_blk, v_blk, sel):
    b, s, _ = hb3.shape
    t = T_ATT
    n_d = s // t
    n_p = n_d // 2
    o_a, o_b = pl.pallas_call(
        functools.partial(_slc_kernel, n_d=n_d),
        grid=(C_HEADS // 2, b, n_p),
        in_specs=[pl.BlockSpec((1, t, LANES), lambda hp, bi, pi: (bi, pi, hp)),
                  pl.BlockSpec((1, t, LANES), lambda hp, bi, pi: (bi, n_d - 1 - pi, hp)),
                  pl.BlockSpec((1, s, LANES), lambda hp, bi, pi: (bi, 0, k_blk + hp // 2)),
                  pl.BlockSpec((1, s, LANES), lambda hp, bi, pi: (bi, 0, v_blk + hp // 2)),
                  pl.BlockSpec((2, n_d + 1, t, t), lambda hp, bi, pi: (hp, 0, 0, 0)),
                  pl.BlockSpec((1, 1, t, LANES), lambda hp, bi, pi: (bi, hp // 2, pi, 0)),
                  pl.BlockSpec((1, 1, t, LANES), lambda hp, bi, pi: (bi, hp // 2, n_d - 1 - pi, 0))],
        out_specs=[pl.BlockSpec((1, t, LANES), lambda hp, bi, pi: (bi, pi, hp)),
                   pl.BlockSpec((1, t, LANES), lambda hp, bi, pi: (bi, n_p - 1 - pi, hp))],
        out_shape=[jax.ShapeDtypeStruct((b, s // 2, C_HEADS * C_DIM), F32)] * 2,
        scratch_shapes=[pltpu.VMEM((LANES, s), BF16)],
        compiler_params=_cparams(("arbitrary", "arbitrary", "arbitrary")),
        name="nsa_selected",
    )(hb3, hb3, hb3, hb3, tb, sel, sel)
    return _join_halves(o_a, o_b)


def _win_attention(hb3, tb, k_blk, v_blk, n_prev):
    b, s, _ = hb3.shape
    t = T_ATT
    assert s >= (n_prev + 1) * t
    tiles = 2
    return pl.pallas_call(
        functools.partial(_win_kernel, n_prev=n_prev, tiles_per_step=tiles),
        grid=(C_HEADS // 2, b, s // (tiles * t)),
        in_specs=[pl.BlockSpec((1, tiles * t, LANES), lambda hp, bi, qi: (bi, qi, hp)),
                  pl.BlockSpec((1, s, LANES), lambda hp, bi, qi: (bi, 0, k_blk + hp // 2)),
                  pl.BlockSpec((1, s, LANES), lambda hp, bi, qi: (bi, 0, v_blk + hp // 2)),
                  pl.BlockSpec((2, 2 * n_prev + 1, t, t), lambda hp, bi, qi: (hp, 0, 0, 0))],
        out_specs=pl.BlockSpec((1, tiles * t, LANES), lambda hp, bi, qi: (bi, qi, hp)),
        out_shape=jax.ShapeDtypeStruct((b, s, C_HEADS * C_DIM), F32),
        scratch_shapes=[pltpu.VMEM((LANES, s), BF16)],
        compiler_params=_cparams(("arbitrary", "arbitrary", "arbitrary")),
        name="nsa_window",
    )(hb3, hb3, hb3, tb)


def _rms(x, g):
    return x * lax.rsqrt(jnp.mean(x * x, axis=-1, keepdims=True) + RMS_EPS) * g


def _mla_prep_kernel(cq_ref, ckv_ref, kr1_ref, kr2_ref, cos_ref, sin_ref, gq_ref, gkv_ref,
                     wq1_ref, wq2_ref, wk_ref, wv_ref, q_ref, k_ref, v_ref):
    cqn = _rms(cq_ref[...], gq_ref[...]).astype(BF16)
    c = _rms(ckv_ref[...], gkv_ref[...]).astype(BF16)
    cos, sin = cos_ref[...], sin_ref[...]
    k_rope = kr1_ref[...] * cos + kr2_ref[...] * sin
    scale = (D_NOPE + D_ROPE) ** -0.5
    for h in range(D_HEADS):
        cols = slice(h * LANES, (h + 1) * LANES)
        q = _dot(cqn, wq1_ref[:, cols]) * cos + _dot(cqn, wq2_ref[:, cols]) * sin
        q_ref[:, cols] = (q * scale).astype(q_ref.dtype)
        k_ref[:, cols] = (_dot(c, wk_ref[:, cols]) + k_rope).astype(k_ref.dtype)
    v_ref[...] = _dot(c, wv_ref[...]).astype(v_ref.dtype)


def _mla_prep(hf, cos, sin, gq, gkv, wq1, wq2, wk, wv, seq):
    m = hf.shape[0]
    n_seq = seq // TM
    const = lambda shape: pl.BlockSpec(shape, lambda i: (0, 0))
    hq = D_HEADS * LANES
    return pl.pallas_call(
        _mla_prep_kernel,
        grid=(m // TM,),
        in_specs=[pl.BlockSpec((TM, D_Q_LORA), lambda i: (i, 1)),
                  pl.BlockSpec((TM, D_KV_LORA), lambda i: (i, 3)),
                  pl.BlockSpec((TM, LANES), lambda i: (i, 8)),
                  pl.BlockSpec((TM, LANES), lambda i: (i, 9)),
                  pl.BlockSpec((TM, LANES), lambda i: (i % n_seq, 0)),
                  pl.BlockSpec((TM, LANES), lambda i: (i % n_seq, 0)),
                  const((1, D_Q_LORA)), const((1, D_KV_LORA)),
                  const(wq1.shape), const(wq2.shape), const(wk.shape), const(wv.shape)],
        out_specs=[pl.BlockSpec((TM, hq), lambda i: (i, 0)), pl.BlockSpec((TM, hq), lambda i: (i, 0)),
                   pl.BlockSpec((TM, D_HEADS * D_V), lambda i: (i, 0))],
        out_shape=[jax.ShapeDtypeStruct((m, hq), BF16), jax.ShapeDtypeStruct((m, hq), BF16),
                   jax.ShapeDtypeStruct((m, D_HEADS * D_V), BF16)],
        compiler_params=_cparams(("arbitrary",)),
        name="mla_prep",
    )(hf, hf, hf, hf, cos, sin, gq.reshape(1, -1), gkv.reshape(1, -1), wq1, wq2, wk, wv)


def _mla_kernel(qa_ref, qb_ref, k_ref, v_ref, oa_ref, ob_ref, vt_ref, *, n_d):
    t = T_ATT
    pi = pl.program_id(2)

    @pl.when(pi == 0)
    def _():
        _fill_transposed(vt_ref, v_ref)

    head_cols = [slice(half * LANES, (half + 1) * LANES) for half in range(2)]
    key_minus_query = (lax.broadcasted_iota(jnp.int32, (2 * t, t), 0)
                       - lax.broadcasted_iota(jnp.int32, (2 * t, t), 1))

    def score(n, q_tile, i, s, diag):
        return jnp.where(key_minus_query <= (q_tile - 2 * i) * t, s, NEG_INF) if diag else s

    out_a, out_b = _flash_balanced(pi, n_d, [qa_ref[0, :, c] for c in head_cols], [qb_ref[0, :, c] for c in head_cols],
                                   [k_ref, k_ref], head_cols, vt_ref,
                                   [slice(half * D_V, (half + 1) * D_V) for half in range(2)], score)
    oa_ref[0] = jnp.concatenate(out_a, axis=0).T.astype(oa_ref.dtype)
    ob_ref[0] = jnp.concatenate(out_b, axis=0).T.astype(ob_ref.dtype)


def _mla_attention(q3, k3, v3):
    b, s, _ = q3.shape
    t = T_ATT
    n_d = s // t
    n_p = n_d // 2
    o_a, o_b = pl.pallas_call(
        functools.partial(_mla_kernel, n_d=n_d),
        grid=(D_HEADS // 2, b, n_p),
        in_specs=[pl.BlockSpec((1, t, 2 * LANES), lambda hp, bi, pi: (bi, pi, hp)),
                  pl.BlockSpec((1, t, 2 * LANES), lambda hp, bi, pi: (bi, n_d - 1 - pi, hp)),
                  pl.BlockSpec((1, s, 2 * LANES), lambda hp, bi, pi: (bi, 0, hp)),
                  pl.BlockSpec((1, s, LANES), lambda hp, bi, pi: (bi, 0, hp))],
        out_specs=[pl.BlockSpec((1, t, LANES), lambda hp, bi, pi: (bi, pi, hp)),
                   pl.BlockSpec((1, t, LANES), lambda hp, bi, pi: (bi, n_p - 1 - pi, hp))],
        out_shape=[jax.ShapeDtypeStruct((b, s // 2, D_HEADS * D_V), BF16)] * 2,
        scratch_shapes=[pltpu.VMEM((LANES, s), BF16)],
        compiler_params=_cparams(("arbitrary", "arbitrary", "arbitrary")),
        name="mla_attention",
    )(q3, q3, k3, v3)
    return _join_halves(o_a, o_b)


def _odd_out_kernel(gate_ref, oc_ref, os_ref, ow_ref, od_ref, x_ref, w_ref, g_ref, b_ref, out_ref):
    gates = 1.0 / (1.0 + jnp.exp(-gate_ref[...]))
    lo = _lane_lo(TM)
    n_c = C_HEADS * C_DIM
    y = _dot(od_ref[...], w_ref[n_c:, :])
    for blk in range(n_c // LANES):
        cols = slice(blk * LANES, (blk + 1) * LANES)
        acc = jnp.zeros((TM, LANES), F32)
        for br, o_ref in enumerate((oc_ref, os_ref, ow_ref)):
            c0 = br * C_HEADS + 2 * blk
            gexp = jnp.where(lo, gates[:, c0:c0 + 1], gates[:, c0 + 1:c0 + 2])
            acc = acc + gexp * o_ref[:, cols]
        y = y + _dot(acc.astype(BF16), w_ref[cols, :])
    out_ref[...] = _residual_ln(x_ref[...], y, g_ref[...], b_ref[...])


def _odd_out(hf, o_cmp, o_slc, o_win, o_d, x, w_out, g, beta):
    m = x.shape[0]
    n_c = C_HEADS * C_DIM
    row = lambda width: pl.BlockSpec((TM, width), lambda i: (i, 0))
    const = lambda shape: pl.BlockSpec(shape, lambda i: (0, 0))
    return pl.pallas_call(
        _odd_out_kernel,
        grid=(m // TM,),
        in_specs=[pl.BlockSpec((TM, LANES), lambda i: (i, 2)),
                  row(n_c), row(n_c), row(n_c), row(D_HEADS * D_V), row(D_MODEL),
                  const(w_out.shape), const((1, D_MODEL)), const((1, D_MODEL))],
        out_specs=row(D_MODEL),
        out_shape=jax.ShapeDtypeStruct((m, D_MODEL), F32),
        compiler_params=_cparams(("arbitrary",)),
        name="odd_out",
    )(hf, o_cmp, o_slc, o_win, o_d, x, w_out, g.reshape(1, -1), beta.reshape(1, -1))


ODD_BF16_W = 1536
ODD_F32_W = 1280


def _odd_in_columns():
    q_c = C_HEADS * C_DIM
    kv = lambda br, which, g: q_c + ((br * 2 + which) * C_KV_GROUPS + g) * C_DIM + np.arange(C_DIM)
    gate0 = q_c + 3 * 2 * C_KV_GROUPS * C_DIM
    cq0 = gate0 + 3 * C_HEADS
    ckv0 = cq0 + D_Q_LORA
    kr0 = ckv0 + D_KV_LORA
    zeros = lambda n: np.full(n, -1)
    cols = [np.arange(q_c)]
    for br in (1, 2):
        for which in (0, 1):
            for g in range(C_KV_GROUPS):
                cols += [kv(br, which, g), kv(br, which, g)]
    assert sum(len(c) for c in cols) == ODD_BF16_W
    for which in (0, 1):
        cols += [kv(0, which, 0), kv(0, which, 1)]
    cols += [gate0 + np.arange(3 * C_HEADS), zeros(LANES - 3 * C_HEADS)]
    cols += [cq0 + np.arange(D_Q_LORA), ckv0 + np.arange(D_KV_LORA)]
    half = D_ROPE // 2
    kr = kr0 + np.arange(D_ROPE)
    cols += [zeros(D_NOPE), kr, zeros(LANES - D_NOPE - D_ROPE)]
    cols += [zeros(D_NOPE), kr[half:], kr[:half], zeros(LANES - D_NOPE - D_ROPE)]
    cols = np.concatenate(cols)
    assert len(cols) == ODD_BF16_W + ODD_F32_W
    return cols


def _gather_cols(w, cols):
    return jnp.where(jnp.asarray(cols >= 0)[None, :], w[:, np.maximum(cols, 0)], 0.0)


def _mla_weight_columns():
    dq = D_NOPE + D_ROPE
    half = D_ROPE // 2
    zeros = lambda n: np.full(n, -1)
    q1, q2, k1 = [], [], []
    for h in range(D_HEADS):
        rope0 = h * dq + D_NOPE
        q1 += [h * dq + np.arange(dq), zeros(LANES - dq)]
        q2 += [zeros(D_NOPE), rope0 + half + np.arange(half), rope0 + np.arange(half), zeros(LANES - dq)]
        k1 += [h * D_NOPE + np.arange(D_NOPE), zeros(LANES - D_NOPE)]
    return np.concatenate(q1), np.concatenate(q2), np.concatenate(k1)


def _rope_tables(seq):
    half = D_ROPE // 2
    inv = ROPE_THETA ** (-jnp.arange(half, dtype=F32) / half)
    ang = jnp.arange(seq).astype(F32)[:, None] * inv
    cos, sin = jnp.cos(ang), jnp.sin(ang)
    pad = jnp.zeros((seq, LANES - D_NOPE - D_ROPE), F32)
    cos_t = jnp.concatenate([jnp.ones((seq, D_NOPE), F32), cos, cos, pad], axis=1)
    sin_t = jnp.concatenate([jnp.zeros((seq, D_NOPE), F32), -sin, sin, pad], axis=1)
    return cos_t, sin_t


def kernel(x, rel_bias, ev_w_in, ev_w_out, ev_lambda, ev_subln, od_w_in, od_w_out, od_cmp_pe, od_cmp_w1, od_cmp_w2, od_q_norm, od_kv_norm, od_w_uq, od_w_uk, od_w_uv, ffn_w_up, ffn_conv_w, ffn_conv_b, ffn_w_down, ln_g, ln_b):
    b, s, _ = x.shape
    m = b * s
    assert s % 1024 == 0 and s // B_PAIRS[-1][1] >= 2 * T_DIL and s // SLC_BLOCK <= LANES
    n_d = s // T_ATT

    tb_t = rel_bias.T.astype(F32)
    n_prev_win = -(-(WIN_SIZE - 1) // T_ATT)
    tb_causal = _bias_table(tb_t[:C_HEADS], _toeplitz_idx(1, n_d, T_ATT, s))
    tb_win = _bias_table(tb_t[:C_HEADS], _toeplitz_idx(n_prev_win, n_prev_win + 1, T_ATT, WIN_SIZE - 1))
    tb_cmp = _bias_table(tb_t[:C_HEADS], _cmp_idx(s)).reshape(C_HEADS, s, s // CMP_STRIDE)
    tb_dil = [_bias_table(tb_t[A_HEADS + i * B_HEADS:A_HEADS + (i + 1) * B_HEADS], _dilated_idx(d))
              for i, (_, d) in enumerate(B_PAIRS)]
    cos_t, sin_t = _rope_tables(s)
    odd_cols = _odd_in_columns()
    q1_cols, q2_cols, k1_cols = _mla_weight_columns()

    xf = x.reshape(m, D_MODEL)
    for l in range(DEPTH):
        i = l // 2
        if l % 2 == 0:
            n_a = EVEN_IN - EVEN_DILATED
            h_a, h_b = _inproj(xf, ev_w_in[i].astype(BF16), (n_a, EVEN_DILATED), (BF16, F32), 768)
            h3 = h_a.reshape(b, s, n_a)
            hd3 = h_b.reshape(b, s, EVEN_DILATED)
            lam_init = 0.8 - 0.6 * math.exp(-0.3 * l)
            o_a = _diff_attention(h3, tb_causal, ev_lambda[i].astype(F32), ev_subln[i], lam_init)
            obs, lses = [], []
            for p_idx, (_, d) in enumerate(B_PAIRS):
                o, lse = _dilated_attention(hd3, tb_dil[p_idx], p_idx, d)
                obs.append(o)
                lses.append(lse)
            xf = _even_out(o_a.reshape(m, -1), obs, lses, xf, ev_w_out[i].astype(BF16), ln_g[l, 0], ln_b[l, 0])
        else:
            w_in = _gather_cols(od_w_in[i], odd_cols).astype(BF16)
            hb, hf = _inproj(xf, w_in, (ODD_BF16_W, ODD_F32_W), (BF16, F32), 256)
            hb3 = hb.reshape(b, s, ODD_BF16_W)
            n_ch = s // CMP_STRIDE
            chunks = (hf[:, :4 * C_DIM].reshape(b, n_ch, CMP_STRIDE, 4, C_DIM)
                      .transpose(0, 3, 1, 2, 4).reshape(b, 4, n_ch, CMP_STRIDE * C_DIM))
            pe = od_cmp_pe[i].reshape(2, 2, CMP_STRIDE * C_DIM)
            w2d = jnp.concatenate([od_cmp_w2[i], od_cmp_w2[i]], axis=-1).astype(BF16)
            kvc = _compress(chunks, pe, od_cmp_w1[i].astype(BF16), w2d)
            o_cmp, sel = _cmp_attention(hb3, kvc, tb_cmp)
            o_slc = _slc_attention(hb3, tb_causal, 4, 6, sel)
            o_win = _win_attention(hb3, tb_win, 8, 10, n_prev_win)
            q_d, k_d, v_d = _mla_prep(hf, cos_t, sin_t, od_q_norm[i], od_kv_norm[i],
                                      _gather_cols(od_w_uq[i], q1_cols).astype(BF16),
                                      _gather_cols(od_w_uq[i], q2_cols).astype(BF16),
                                      _gather_cols(od_w_uk[i], k1_cols).astype(BF16),
                                      od_w_uv[i].astype(BF16), s)
            o_d = _mla_attention(q_d.reshape(b, s, -1), k_d.reshape(b, s, -1), v_d.reshape(b, s, -1))
            xf = _odd_out(hf, o_cmp.reshape(m, -1), o_slc.reshape(m, -1), o_win.reshape(m, -1),
                          o_d.reshape(m, -1), xf, od_w_out[i].astype(BF16), ln_g[l, 0], ln_b[l, 0])
        xf = _ffn(xf, ffn_w_up[l].astype(BF16), ffn_conv_w[l], ffn_conv_b[l], ffn_w_down[l].astype(BF16),
                  ln_g[l, 1], ln_b[l, 1], s)
    return xf.reshape(b, s, D_MODEL)
```

```python
import functools
import math

import numpy as np
import jax
import jax.numpy as jnp
from jax import lax
from jax.experimental import pallas as pl
from jax.experimental.pallas import tpu as pltpu

F32 = jnp.float32
BF16 = jnp.bfloat16

D_MODEL = 1024
DEPTH = 4
LN_EPS = 1e-5
RMS_EPS = 1e-5
N_BUCKETS = 32
MAX_DISTANCE = 2048
A_HEADS, A_QK, A_V = 4, 64, 128
B_PAIRS = ((128, 1), (512, 4), (2048, 16))
B_HEADS, B_DIM = 4, 64
C_HEADS, C_KV_GROUPS, C_DIM = 8, 2, 64
CMP_LEN, CMP_STRIDE, CMP_HIDDEN = 32, 16, 256
SLC_BLOCK, SLC_TOP_N = 64, 16
SLC_SHIFT = 6
WIN_SIZE = 512
D_HEADS, D_Q_LORA, D_KV_LORA, D_NOPE, D_ROPE, D_V = 8, 384, 256, 64, 32, 64
ROPE_THETA = 10000.0
D_FF = 2816
EVEN_DILATED = len(B_PAIRS) * 3 * B_HEADS * B_DIM
EVEN_IN = 2 * A_HEADS * A_QK * 2 + A_HEADS * A_V + EVEN_DILATED
ALPHA = (2 * DEPTH) ** 0.25

LANES = 128
HALF_LANE_SHIFT = 6
VMEM_LIMIT = 56 * 1024 * 1024
T_ATT = 256
T_DIL = 128
TM = 512
FF_CHUNK = 256
NEG_INF = float("-inf")
M_INIT = -1e30
SEL_PENALTY = 2.0 ** 101
LOG2E = math.log2(math.e)
ONES_ROWS = 16


def _cparams(sem):
    return pltpu.CompilerParams(dimension_semantics=sem, vmem_limit_bytes=VMEM_LIMIT)


def _dot(a, b):
    return jnp.dot(a, b, preferred_element_type=F32)


def _dot_nt(a, b):
    return lax.dot_general(a, b, (((1,), (1,)), ((), ())), preferred_element_type=F32)


def _bucket_np(dist):
    n = np.maximum(dist, 0)
    nf = np.maximum(n, 1).astype(np.float32)
    max_exact = N_BUCKETS // 2
    large = max_exact + (np.log(nf / max_exact) / math.log(MAX_DISTANCE / max_exact)
                         * (N_BUCKETS - max_exact)).astype(np.int32)
    return np.where(n < max_exact, n, np.minimum(large, N_BUCKETS - 1)).astype(np.int32)


def _toeplitz_idx(n_masked, n_delta, t, max_dist):
    key = np.arange(t)[:, None]
    query = np.arange(t)[None, :]
    out = []
    for delta in range(-n_masked, n_delta):
        dist = t * delta + query - key
        ok = (dist >= 0) & (dist <= max_dist)
        out.append(np.where(ok, _bucket_np(dist), -1))
    return np.stack(out).astype(np.int32)


def _dilated_idx(d):
    t = T_DIL
    r = np.arange(t)[:, None]
    c = np.arange(2 * t)[None, :]
    dist0 = r - c
    ok0 = (dist0 >= 0) & (c < t)
    dist1 = t + r - c
    ok1 = (dist1 >= 0) & (dist1 <= t)
    return np.stack([np.where(ok0, _bucket_np(dist0 * d), -1),
                     np.where(ok1, _bucket_np(dist1 * d), -1)]).astype(np.int32)


def _cmp_idx(seq):
    ncb = seq // CMP_STRIDE - 1
    q = np.arange(seq)[:, None]
    c = np.arange(seq // CMP_STRIDE)[None, :]
    dist = q - (c * CMP_STRIDE + CMP_LEN - 1)
    ok = (dist >= 0) & (c < ncb)
    return np.where(ok, _bucket_np(dist), -1).astype(np.int32).reshape(seq // T_ATT, T_ATT, seq // CMP_STRIDE)


def _bias_kernel(tbl_ref, idx_ref, o_ref):
    h = pl.program_id(0)
    idx = idx_ref[0]
    out = jnp.full(idx.shape, NEG_INF, F32)
    for b in range(N_BUCKETS):
        out = jnp.where(idx == b, tbl_ref[h, b], out)
    o_ref[0, 0] = out


def _bias_table(tbl, idx):
    n_h = tbl.shape[0]
    n, r, c = idx.shape
    return pl.pallas_call(
        _bias_kernel,
        grid=(n_h, n),
        in_specs=[pl.BlockSpec(memory_space=pltpu.SMEM),
                  pl.BlockSpec((1, r, c), lambda h, i: (i, 0, 0))],
        out_specs=pl.BlockSpec((1, 1, r, c), lambda h, i: (h, i, 0, 0)),
        out_shape=jax.ShapeDtypeStruct((n_h, n, r, c), F32),
        compiler_params=_cparams(("arbitrary", "arbitrary")),
        name="bias_table",
    )(tbl, jnp.asarray(idx))


def _inproj_kernel(x_ref, w_ref, *o_refs, widths, chunk):
    xb = x_ref[...].astype(BF16)
    off = 0
    for o_ref, width in zip(o_refs, widths):
        for c0 in range(0, width, chunk):
            o_ref[:, c0:c0 + chunk] = _dot(xb, w_ref[:, off + c0:off + c0 + chunk]).astype(o_ref.dtype)
        off += width


def _inproj(x, w, widths, dtypes, chunk):
    m = x.shape[0]
    n = w.shape[1]
    assert sum(widths) == n and all(wd % chunk == 0 for wd in widths)
    return pl.pallas_call(
        functools.partial(_inproj_kernel, widths=widths, chunk=chunk),
        grid=(m // TM,),
        in_specs=[pl.BlockSpec((TM, D_MODEL), lambda i: (i, 0)),
                  pl.BlockSpec((D_MODEL, n), lambda i: (0, 0))],
        out_specs=[pl.BlockSpec((TM, wd), lambda i: (i, 0)) for wd in widths],
        out_shape=[jax.ShapeDtypeStruct((m, wd), dt) for wd, dt in zip(widths, dtypes)],
        compiler_params=_cparams(("arbitrary",)),
        name="in_proj",
    )(x, w)


def _tree_reduce_rows(x, combine, reduce_fn):
    parts = [x[r:r + 8] for r in range(0, x.shape[0], 8)]
    while len(parts) > 1:
        parts = [combine(parts[k], parts[k + 1]) for k in range(0, len(parts) - 1, 2)] + \
                ([parts[-1]] if len(parts) % 2 else [])
    return reduce_fn(parts[0], axis=0, keepdims=True)


def _flash_init(streams):
    return tuple((jnp.full((1, q.shape[0]), M_INIT, F32), jnp.zeros((vt_rows.stop - vt_rows.start, q.shape[0]), F32))
                 for q, _, _, _, vt_rows in streams)


def _normalise(acc):
    d_v = acc.shape[0] - ONES_ROWS
    return acc[:d_v] / acc[d_v:d_v + 1]


def _flash_scores(streams, key_start, n_keys):
    keys = pl.ds(pl.multiple_of(key_start, T_ATT), n_keys)
    return tuple(_dot_nt(k_ref[0, keys, k_cols], q) for q, k_ref, k_cols, _, _ in streams)


def _flash_update(streams, score_fns, key_start, n_keys, i, raw, state):
    keys = pl.ds(pl.multiple_of(key_start, T_ATT), n_keys)
    scores = [fn(i, s) for fn, s in zip(score_fns, raw)]
    m_new = [jnp.maximum(m, _tree_reduce_rows(s, jnp.maximum, jnp.max)) for s, (m, _) in zip(scores, state)]
    probs = [jnp.exp2(s - mn) for s, mn in zip(scores, m_new)]
    out = []
    for (_, _, _, vt_ref, vt_rows), p, mn, (m, acc) in zip(streams, probs, m_new, state):
        acc = jnp.exp2(m - mn) * acc + _dot(vt_ref[vt_rows, keys], p.astype(BF16))
        out.append((mn, acc))
    return tuple(out)


def _flash_finish(state):
    return [_normalise(acc) for _, acc in state]


def _flash_balanced(pi, n_d, qa, qb, k_refs, k_cols, vt_ref, vt_rows, score, key_extra=None):
    kt = 2 * T_ATT
    t_q = qa[0].shape[0]
    n_streams = len(qa)
    n_slots = n_d // 2 + 1
    max_a = n_d // 4
    n_a = pi // 2 + 1
    tile_a, tile_b = pi, n_d - 1 - pi

    def slot(s):
        if s == 0:
            return True, tile_a, n_a - 1
        if s >= max_a:
            return False, tile_b, s - n_a
        is_a = s < n_a
        return is_a, jnp.where(is_a, tile_a, tile_b), jnp.where(is_a, n_a - 1 - s, s - n_a)

    def pick(is_a, a, b):
        return (a if is_a else b) if isinstance(is_a, bool) else jnp.where(is_a, a, b)

    def keys_of(i):
        return pl.ds(pl.multiple_of(i * kt, kt), kt)

    def raw_scores(s):
        is_a, _, i = slot(s)
        extra = None if key_extra is None else key_extra(i)
        out = []
        for n in range(n_streams):
            keys = k_refs[n][0, keys_of(i), k_cols[n]]
            if extra is not None:
                keys = jnp.concatenate([keys, extra], axis=1)
            out.append(_dot_nt(keys, pick(is_a, qa[n], qb[n])))
        return out

    state = [(jnp.full((1, t_q), M_INIT, F32), jnp.zeros((vt_rows[n].stop - vt_rows[n].start, t_q), F32))
             for n in range(n_streams)]
    out_a = [jnp.zeros((acc.shape[0] - ONES_ROWS, t_q), F32) for _, acc in state]
    pending = None
    raw = raw_scores(0)

    def fold(state, pending):
        alphas, probs, i_prev = pending
        return [(m, alphas[n] * acc + _dot(vt_ref[vt_rows[n], keys_of(i_prev)], probs[n]))
                for n, (m, acc) in enumerate(state)]

    for s in range(n_slots):
        raw_next = raw_scores(s + 1) if s + 1 < n_slots else None
        is_a, q_tile, i = slot(s)
        if pending is not None:
            state = fold(state, pending)
        if 1 <= s <= max_a:
            switch = s == n_a
            out_a = [jnp.where(switch, _normalise(acc), o) for (_, acc), o in zip(state, out_a)]
            state = [(jnp.where(switch, M_INIT, m), jnp.where(switch, 0.0, acc)) for m, acc in state]
        diag = s == 0 or s == n_slots - 1
        scores = [score(n, q_tile, i, raw[n], diag) for n in range(n_streams)]
        m_new = [jnp.maximum(m, _tree_reduce_rows(sc, jnp.maximum, jnp.max)) for sc, (m, _) in zip(scores, state)]
        probs = [jnp.exp2(sc - mn).astype(BF16) for sc, mn in zip(scores, m_new)]
        alphas = [jnp.exp2(m - mn) for (m, _), mn in zip(state, m_new)]
        state = [(mn, acc) for mn, (_, acc) in zip(m_new, state)]
        pending = (alphas, probs, i)
        raw = raw_next
    state = fold(state, pending)
    return out_a, [_normalise(acc) for _, acc in state]


def _pair_bias(tb_ref, h, q_tile, i, s):
    t = T_ATT
    d0 = q_tile - 2 * i
    return jnp.concatenate([s[:t] + tb_ref[h, d0 + 1], s[t:] + tb_ref[h, d0]], axis=0)


def _fill_transposed(vt_ref, v_ref, d_v):
    n = v_ref.shape[1]
    n_heads = vt_ref.shape[0] // (d_v + ONES_ROWS)
    step = 2 * T_ATT
    for c0 in range(0, n, step):
        v_t = v_ref[0, c0:c0 + step, :].astype(F32).T.astype(BF16)
        for h in range(n_heads):
            r0 = h * (d_v + ONES_ROWS)
            vt_ref[r0:r0 + d_v, c0:c0 + step] = v_t[h * d_v:(h + 1) * d_v]
            vt_ref[r0 + d_v:r0 + d_v + ONES_ROWS, c0:c0 + step] = jnp.ones((ONES_ROWS, step), BF16)


def _value_rows(h, d_v):
    return slice(h * (d_v + ONES_ROWS), (h + 1) * (d_v + ONES_ROWS))


def _lane_lo(rows):
    return lax.broadcasted_iota(jnp.int32, (rows, LANES), 1) < (LANES // 2)


def _diff_kernel(lam_ref, g_ref, q0a_ref, q0b_ref, q1a_ref, q1b_ref, k0_ref, k1_ref, v_ref, tb_ref,
                 oa_ref, ob_ref, vt_ref, *, lam_init, n_d):
    h = pl.program_id(0)
    pi = pl.program_id(2)

    @pl.when(pi == 0)
    def _():
        _fill_transposed(vt_ref, v_ref, A_V)

    lp = lam_ref[...]
    lam = (jnp.exp(jnp.sum(lp[0:1] * lp[1:2], axis=-1, keepdims=True))
           - jnp.exp(jnp.sum(lp[2:3] * lp[3:4], axis=-1, keepdims=True)) + lam_init)
    keep = jnp.right_shift(lax.broadcasted_iota(jnp.int32, (T_ATT, LANES), 1), HALF_LANE_SHIFT) == h % 2

    def prep(q_ref):
        q = q_ref[0] * (A_QK ** -0.5)
        return jnp.where(keep, q, jnp.zeros_like(q))

    full = slice(0, LANES)

    def score(n, q_tile, i, s, diag):
        return _pair_bias(tb_ref, 0, q_tile, i, s)

    out_a, out_b = _flash_balanced(pi, n_d, [prep(q0a_ref), prep(q1a_ref)], [prep(q0b_ref), prep(q1b_ref)],
                                   [k0_ref, k1_ref], [full, full], vt_ref, [_value_rows(0, A_V)] * 2, score)
    for outs, o_ref in ((out_a, oa_ref), (out_b, ob_ref)):
        d = (outs[0] - lam * outs[1]).T
        ms = jnp.mean(d * d, axis=-1, keepdims=True)
        o_ref[0] = (d * lax.rsqrt(ms + RMS_EPS) * g_ref[...] * (1.0 - lam_init)).astype(o_ref.dtype)


def _join_halves(o_a, o_b):
    return jnp.concatenate([o_a, o_b], axis=1)


def _diff_attention(h3, tb, lam_p, subln_g, lam_init):
    b, s, _ = h3.shape
    t = T_ATT
    n_d = s // t
    n_p = n_d // 2
    q_spec = lambda col0, second: pl.BlockSpec(
        (1, t, LANES), lambda h, bi, pi: (bi, (n_d - 1 - pi) if second else pi, col0 + h // 2))
    kv_spec = lambda col_fn: pl.BlockSpec((1, s, LANES), lambda h, bi, pi: (bi, 0, col_fn(h)))
    o_a, o_b = pl.pallas_call(
        functools.partial(_diff_kernel, lam_init=lam_init, n_d=n_d),
        grid=(A_HEADS, b, n_p),
        in_specs=[pl.BlockSpec((4, A_QK), lambda h, bi, pi: (0, 0)),
                  pl.BlockSpec((1, A_V), lambda h, bi, pi: (0, 0)),
                  q_spec(0, False), q_spec(0, True), q_spec(2, False), q_spec(2, True),
                  kv_spec(lambda h: 4 + h // 2), kv_spec(lambda h: 6 + h // 2), kv_spec(lambda h: 8 + h),
                  pl.BlockSpec((1, n_d + 1, t, t), lambda h, bi, pi: (h, 0, 0, 0))],
        out_specs=[pl.BlockSpec((1, t, A_V), lambda h, bi, pi: (bi, pi, h)),
                   pl.BlockSpec((1, t, A_V), lambda h, bi, pi: (bi, n_p - 1 - pi, h))],
        out_shape=[jax.ShapeDtypeStruct((b, s // 2, A_HEADS * A_V), BF16)] * 2,
        scratch_shapes=[pltpu.VMEM((A_V + ONES_ROWS, s), BF16)],
        compiler_params=_cparams(("arbitrary", "arbitrary", "arbitrary")),
        name="diff_attention",
    )(lam_p, subln_g.reshape(1, A_V), h3, h3, h3, h3, h3, h3, h3, tb)
    return _join_halves(o_a, o_b)


def _dilated_kernel(q_ref, k_ref, v_ref, tb_ref, o_ref, lse_ref, *, d, residues):
    t = T_DIL
    n_blocks = q_ref.shape[1] // (d * t)
    lo = _lane_lo(t)
    scale = B_DIM ** -0.5

    def block(idx, carry):
        r = pl.program_id(2) * residues + idx // n_blocks
        bi = idx % n_blocks
        var = jnp.minimum(bi, 1)
        qrows = pl.ds(r + d * t * bi, t, stride=d)
        krows = pl.ds(r + d * t * jnp.maximum(bi - 1, 0), 2 * t, stride=d)
        q = (q_ref[0, qrows, :] * scale).astype(BF16)
        kk = k_ref[0, krows, :].astype(BF16)
        vv = v_ref[0, krows, :].astype(BF16)
        o_half, l_half = [], []
        for half in range(2):
            keep = lo if half == 0 else jnp.logical_not(lo)
            qm = jnp.where(keep, q, jnp.zeros_like(q))
            s = _dot_nt(qm, kk) + tb_ref[half, var]
            m = jnp.max(s, axis=-1, keepdims=True)
            p = jnp.exp(s - m)
            l = jnp.sum(p, axis=-1, keepdims=True)
            o_half.append(_dot(p.astype(BF16), vv) / l)
            l_half.append(m + jnp.log(l))
        o_ref[0, qrows, :] = jnp.where(lo, o_half[0], o_half[1])
        lse_ref[0, qrows, :] = jnp.where(lo, l_half[0], l_half[1])
        return carry

    lax.fori_loop(0, residues * n_blocks, block, 0, unroll=4)


def _dilated_attention(hd3, tb, pair_idx, d):
    b, s, _ = hd3.shape
    hd = B_HEADS * B_DIM
    base = pair_idx * 3 * (hd // LANES)
    seq = lambda which: pl.BlockSpec((1, s, LANES), lambda bi, hp, r: (bi, 0, base + which * (hd // LANES) + hp))
    out = pl.BlockSpec((1, s, LANES), lambda bi, hp, r: (bi, 0, hp))
    n_blocks = s // (d * T_DIL)
    residues = min(d, max(1, 8 // n_blocks))
    assert d % residues == 0 and (residues * n_blocks) % 4 == 0
    o, lse = pl.pallas_call(
        functools.partial(_dilated_kernel, d=d, residues=residues),
        grid=(b, hd // LANES, d // residues),
        in_specs=[seq(0), seq(1), seq(2),
                  pl.BlockSpec((2, 2, T_DIL, 2 * T_DIL), lambda bi, hp, r: (hp, 0, 0, 0))],
        out_specs=[out, out],
        out_shape=[jax.ShapeDtypeStruct((b, s, hd), F32)] * 2,
        compiler_params=_cparams(("arbitrary", "arbitrary", "arbitrary")),
        name="dilated_attention",
    )(hd3, hd3, hd3, tb)
    return o.reshape(b * s, hd), lse.reshape(b * s, hd)


def _residual_ln(x, y, g, beta):
    z = ALPHA * x + y
    mu = jnp.mean(z, axis=-1, keepdims=True)
    zc = z - mu
    var = jnp.mean(zc * zc, axis=-1, keepdims=True)
    return zc * lax.rsqrt(var + LN_EPS) * g + beta


def _even_out_kernel(oa_ref, o0_ref, o1_ref, o2_ref, l0_ref, l1_ref, l2_ref, x_ref, w_ref, g_ref, b_ref, out_ref):
    l0, l1, l2 = l0_ref[...], l1_ref[...], l2_ref[...]
    mx = jnp.maximum(jnp.maximum(l0, l1), l2)
    e0, e1, e2 = jnp.exp(l0 - mx), jnp.exp(l1 - mx), jnp.exp(l2 - mx)
    den = e0 + e1 + e2
    ob = (e0 / den) * o0_ref[...] + (e1 / den) * o1_ref[...] + (e2 / den) * o2_ref[...]
    n_a = A_HEADS * A_V
    y = _dot(oa_ref[...], w_ref[0:n_a, :]) + _dot(ob.astype(BF16), w_ref[n_a:, :])
    out_ref[...] = _residual_ln(x_ref[...], y, g_ref[...], b_ref[...])


def _even_out(oa, obs, lses, x, w_out, g, beta):
    m = x.shape[0]
    hd = B_HEADS * B_DIM
    row = lambda width: pl.BlockSpec((TM, width), lambda i: (i, 0))
    const = lambda shape: pl.BlockSpec(shape, lambda i: (0, 0))
    return pl.pallas_call(
        _even_out_kernel,
        grid=(m // TM,),
        in_specs=[row(A_HEADS * A_V)] + [row(hd)] * 6 + [row(D_MODEL), const(w_out.shape),
                                                        const((1, D_MODEL)), const((1, D_MODEL))],
        out_specs=row(D_MODEL),
        out_shape=jax.ShapeDtypeStruct((m, D_MODEL), F32),
        compiler_params=_cparams(("arbitrary",)),
        name="even_out",
    )(oa, *obs, *lses, x, w_out, g.reshape(1, -1), beta.reshape(1, -1))


def _gelu(x):
    return 0.5 * x * (1.0 + jnp.tanh(math.sqrt(2.0 / math.pi) * (x + 0.044715 * (x * x * x))))


def _ffn_kernel(x_ref, wu_ref, cw_ref, cb_ref, wd_ref, g_ref, b_ref, out_ref, u_ref, gs_ref, tail_ref,
                *, tiles_per_seq):
    halo = 8
    n_chunks = D_FF // FF_CHUNK

    @pl.when(pl.program_id(0) % tiles_per_seq == 0)
    def _():
        tail_ref[...] = jnp.zeros(tail_ref.shape, F32)

    x = x_ref[...]
    xb = x.astype(BF16)

    def up(c):
        cols = slice(c * FF_CHUNK, (c + 1) * FF_CHUNK)
        return _dot(xb, wu_ref[:, cols]), _dot(xb, wu_ref[:, D_FF + c * FF_CHUNK:D_FF + (c + 1) * FF_CHUNK])

    def activate(c, a, gate):
        cols = slice(c * FF_CHUNK, (c + 1) * FF_CHUNK)
        gs = gs_ref.at[c % 2]
        gs[0:halo, :] = tail_ref[:, cols]
        gs[halo:, :] = gate
        tail_ref[:, cols] = gate[TM - halo:, :]
        conv = (gs[pl.ds(halo - 2, TM), :] * cw_ref[0:1, cols] + gs[pl.ds(halo - 1, TM), :] * cw_ref[1:2, cols]
                + gate * cw_ref[2:3, cols] + cb_ref[:, cols])
        u_ref[:, cols] = (_gelu(conv) * a).astype(BF16)

    nxt = up(0)
    for c in range(n_chunks):
        cur = nxt
        if c + 1 < n_chunks:
            nxt = up(c + 1)
        activate(c, *cur)
    out_ref[...] = _residual_ln(x, _dot(u_ref[...], wd_ref[...]), g_ref[...], b_ref[...])


def _ffn(x, w_up, conv_w, conv_b, w_down, g, beta, seq):
    m = x.shape[0]
    const = lambda shape: pl.BlockSpec(shape, lambda i: (0, 0))
    return pl.pallas_call(
        functools.partial(_ffn_kernel, tiles_per_seq=seq // TM),
        grid=(m // TM,),
        in_specs=[pl.BlockSpec((TM, D_MODEL), lambda i: (i, 0)), const(w_up.shape), const(conv_w.shape),
                  const((1, D_FF)), const(w_down.shape), const((1, D_MODEL)), const((1, D_MODEL))],
        out_specs=pl.BlockSpec((TM, D_MODEL), lambda i: (i, 0)),
        out_shape=jax.ShapeDtypeStruct((m, D_MODEL), F32),
        scratch_shapes=[pltpu.VMEM((TM, D_FF), BF16), pltpu.VMEM((2, TM + 8, FF_CHUNK), F32),
                        pltpu.VMEM((8, D_FF), F32)],
        compiler_params=_cparams(("arbitrary",)),
        name="conv_ffn",
    )(x, w_up, conv_w, conv_b.reshape(1, -1), w_down, g.reshape(1, -1), beta.reshape(1, -1))


def _compress_kernel(ch_ref, pe_ref, w1_ref, w2_ref, o_ref):
    half = CMP_STRIDE * C_DIM
    ch = ch_ref[0, 0]
    a = _dot((ch + pe_ref[0, 0:1, :]).astype(BF16), w1_ref[0, 0:half, :])
    b = _dot((ch + pe_ref[0, 1:2, :]).astype(BF16), w1_ref[0, half:, :])
    n = ch.shape[0]
    hid = _gelu(a + pltpu.roll(b, n - 1, 0))
    o_ref[0, 0] = _dot(hid.astype(BF16), w2_ref[0]).astype(o_ref.dtype)


def _compress(chunks, pe, w1, w2d):
    b, _, n, width = chunks.shape
    return pl.pallas_call(
        _compress_kernel,
        grid=(b, 4),
        in_specs=[pl.BlockSpec((1, 1, n, width), lambda bi, j: (bi, j, 0, 0)),
                  pl.BlockSpec((1, 2, width), lambda bi, j: (j // 2, 0, 0)),
                  pl.BlockSpec((1, 2 * width, CMP_HIDDEN), lambda bi, j: (j // 2, 0, 0)),
                  pl.BlockSpec((1, CMP_HIDDEN, LANES), lambda bi, j: (j // 2, 0, 0))],
        out_specs=pl.BlockSpec((1, 1, n, LANES), lambda bi, j: (bi, j, 0, 0)),
        out_shape=jax.ShapeDtypeStruct((b, 4, n, LANES), BF16),
        compiler_params=_cparams(("arbitrary", "arbitrary")),
        name="nsa_compress",
    )(chunks, pe, w1, w2d)


def _cmp_kernel(q_ref, kc_ref, vc_ref, tb_ref, o_ref, sel_ref, pg_ref, sc_ref, pen_ref):
    t = T_ATT
    qi = pl.program_id(1)
    n_c = kc_ref.shape[2]
    n_sel = n_c // (SLC_BLOCK // CMP_STRIDE)
    lo = _lane_lo(t)
    kc = kc_ref[0, 0]
    vc = vc_ref[0, 0]
    pg = jnp.zeros((t, n_c), F32)
    for pair in range(2):
        cols = slice(pair * LANES, (pair + 1) * LANES)
        q = q_ref[0, :, cols]
        o_half = []
        for half in range(2):
            keep = lo if half == 0 else jnp.logical_not(lo)
            qm = jnp.where(keep, q * (C_DIM ** -0.5), jnp.zeros_like(q))
            s = _dot_nt(qm, kc) + tb_ref[2 * pair + half]
            m = jnp.maximum(jnp.max(s, axis=-1, keepdims=True), M_INIT)
            p = jnp.exp2(s - m)
            den = jnp.sum(p, axis=-1, keepdims=True)
            p = p / jnp.where(den > 0, den, 1.0)
            o_half.append(_dot(p.astype(BF16), vc))
            pg = pg + p
        o_ref[0, :, cols] = jnp.where(lo, o_half[0], o_half[1])
    pad = 8
    pg_t = pg.T
    for c in range(t // LANES):
        pg_ref[c, 0:pad, :] = jnp.zeros((pad, LANES), F32)
        pg_ref[c, pad:pad + n_c, :] = pg_t[:, c * LANES:(c + 1) * LANES]
        pg_ref[c, pad + n_c:, :] = jnp.zeros((pad, LANES), F32)
    r = SLC_BLOCK // CMP_STRIDE
    tap = lambda k: jnp.concatenate(
        [pg_ref[c, pl.ds(pad + k, n_sel, stride=r), :] for c in range(t // LANES)], axis=1)
    score = (0.5 * tap(-1) + ((tap(0) + tap(1)) + tap(2))) + 0.5 * tap(3)
    jb = lax.broadcasted_iota(jnp.int32, (n_sel, t), 0)
    qblk = jnp.right_shift(qi * t + lax.broadcasted_iota(jnp.int32, (n_sel, t), 1), SLC_SHIFT)
    forced = (jb == 0) | (jb == qblk) | (jb == qblk - 1)
    sc = jnp.where(forced, jnp.inf, jnp.where(jb <= qblk, score, NEG_INF))
    sc_ref[...] = sc
    pen_ref[...] = jnp.zeros(pen_ref.shape, F32)
    groups = range(0, n_sel, 8)
    sc_g = [sc_ref[g0:g0 + 8, :] for g0 in groups]
    sub = lax.broadcasted_iota(jnp.int32, (8, t), 0)
    cnt_g = [jnp.zeros((8, t), jnp.int32) for _ in groups]
    for i in range(n_sel):
        row = jnp.broadcast_to(sc_ref[pl.ds(i, 1), :], (8, t))
        for k, g0 in enumerate(groups):
            if i < g0:
                beats = row >= sc_g[k]
            elif i >= g0 + 8:
                beats = row > sc_g[k]
            else:
                beats = (row > sc_g[k]) | ((row == sc_g[k]) & (i - g0 < sub))
            cnt_g[k] = cnt_g[k] + beats.astype(jnp.int32)
    for k, g0 in enumerate(groups):
        pen_ref[g0:g0 + 8, :] = jnp.where(cnt_g[k] < SLC_TOP_N, 0.0, -SEL_PENALTY)
    sel_ref[0, 0] = pen_ref[...].T.astype(sel_ref.dtype)


def _cmp_attention(hb3, kvc, tb):
    b, s, _ = hb3.shape
    t = T_ATT
    n_c = s // CMP_STRIDE
    n_sel = s // SLC_BLOCK
    assert n_sel <= LANES
    gw = (C_HEADS // C_KV_GROUPS) * C_DIM
    return pl.pallas_call(
        _cmp_kernel,
        grid=(C_KV_GROUPS, s // t, b),
        in_specs=[pl.BlockSpec((1, t, gw), lambda g, qi, bi: (bi, qi, g)),
                  pl.BlockSpec((1, 1, n_c, LANES), lambda g, qi, bi: (bi, g, 0, 0)),
                  pl.BlockSpec((1, 1, n_c, LANES), lambda g, qi, bi: (bi, 2 + g, 0, 0)),
                  pl.BlockSpec((4, t, n_c), lambda g, qi, bi: (g, qi, 0))],
        out_specs=[pl.BlockSpec((1, t, gw), lambda g, qi, bi: (bi, qi, g)),
                   pl.BlockSpec((1, 1, t, LANES), lambda g, qi, bi: (bi, g, qi, 0))],
        out_shape=[jax.ShapeDtypeStruct((b, s, C_HEADS * C_DIM), F32),
                   jax.ShapeDtypeStruct((b, C_KV_GROUPS, s, LANES), BF16)],
        scratch_shapes=[pltpu.VMEM((t // LANES, n_c + 16, LANES), F32), pltpu.VMEM((n_sel, t), F32),
                        pltpu.VMEM((LANES, t), F32)],
        compiler_params=_cparams(("arbitrary", "arbitrary", "arbitrary")),
        name="nsa_cmp_attention",
    )(hb3, kvc, kvc, tb)


def _head_pair_queries(q_ref):
    lo = _lane_lo(T_ATT)
    q = q_ref[0] * (C_DIM ** -0.5)
    zero = jnp.zeros_like(q)
    return [jnp.where(lo, q, zero), jnp.where(lo, zero, q)]


def _slc_kernel(qa_ref, qb_ref, k_ref, v_ref, tb_ref, sela_ref, selb_ref, oa_ref, ob_ref, vt_ref, *, n_d):
    t = T_ATT
    pi = pl.program_id(2)

    @pl.when(pi == 0)
    def _():
        _fill_transposed(vt_ref, v_ref, C_DIM)

    key_blk =jnp.right_shift(lax.broadcasted_iota(jnp.int32, (2 * t, LANES), 0), SLC_SHIFT)
    blk_slot = lax.broadcasted_iota(jnp.int32, (2 * t, LANES), 1)

    def one_hot_block(i):
        return jnp.where(blk_slot == key_blk + i * (2 * t // SLC_BLOCK), 1.0, 0.0).astype(BF16)

    def with_penalty(q_ref, sel_ref):
        return [jnp.concatenate([q, sel_ref[0, 0]], axis=1) for q in _head_pair_queries(q_ref)]

    def score(n, q_tile, i, s, diag):
        return _pair_bias(tb_ref, n, q_tile, i, s)

    full = slice(0, LANES)
    out_a, out_b = _flash_balanced(pi, n_d, with_penalty(qa_ref, sela_ref), with_penalty(qb_ref, selb_ref),
                                   [k_ref, k_ref], [full, full], vt_ref, [_value_rows(0, C_DIM)] * 2, score,
                                   key_extra=one_hot_block)
    oa_ref[0] = jnp.concatenate(out_a, axis=0).T
    ob_ref[0] = jnp.concatenate(out_b, axis=0).T


def _win_kernel(q_ref, k_ref, v_ref, tb_ref, o_ref, vt_ref, *, n_prev, tiles_per_step):
    t = T_ATT
    step = pl.program_id(2)

    @pl.when(step == 0)
    def _():
        _fill_transposed(vt_ref, v_ref, C_DIM)

    full = slice(0, LANES)
    n_keys = (n_prev + 1) * t
    lo = _lane_lo(t)
    work = []
    for u_tile in range(tiles_per_step):
        qi = step * tiles_per_step + u_tile
        q = q_ref[0, u_tile * t:(u_tile + 1) * t, :] * (C_DIM ** -0.5)
        zero = jnp.zeros_like(q)
        streams = [(qh, k_ref, full, vt_ref, _value_rows(0, C_DIM))
                   for qh in (jnp.where(lo, q, zero), jnp.where(lo, zero, q))]
        j0 = jnp.maximum(qi - n_prev, 0)
        work.append((qi, j0, streams, _flash_scores(streams, j0 * t, n_keys)))
    for u_tile, (qi, j0, streams, raw) in enumerate(work):
        def score_fn(h, qi=qi, j0=j0):
            def fn(_, s):
                return jnp.concatenate([s[u * t:(u + 1) * t] + tb_ref[h, qi - j0 - u + n_prev]
                                        for u in range(n_prev + 1)], axis=0)
            return fn

        state = _flash_update(streams, [score_fn(0), score_fn(1)], j0 * t, n_keys, 0, raw, _flash_init(streams))
        o_ref[0, u_tile * t:(u_tile + 1) * t, :] = jnp.concatenate(_flash_finish(state), axis=0).T


def _slc_attention(hb3, tb, k_blk, v_blk, sel):
    b, s, _ = hb3.shape
    t = T_ATT
    n_d = s // t
    n_p = n_d // 2
    o_a, o_b = pl.pallas_call(
        functools.partial(_slc_kernel, n_d=n_d),
        grid=(C_HEADS // 2, b, n_p),
        in_specs=[pl.BlockSpec((1, t, LANES), lambda hp, bi, pi: (bi, pi, hp)),
                  pl.BlockSpec((1, t, LANES), lambda hp, bi, pi: (bi, n_d - 1 - pi, hp)),
                  pl.BlockSpec((1, s, LANES), lambda hp, bi, pi: (bi, 0, k_blk + hp // 2)),
                  pl.BlockSpec((1, s, LANES), lambda hp, bi, pi: (bi, 0, v_blk + hp // 2)),
                  pl.BlockSpec((2, n_d + 1, t, t), lambda hp, bi, pi: (hp, 0, 0, 0)),
                  pl.BlockSpec((1, 1, t, LANES), lambda hp, bi, pi: (bi, hp // 2, pi, 0)),
                  pl.BlockSpec((1, 1, t, LANES), lambda hp, bi, pi: (bi, hp // 2, n_d - 1 - pi, 0))],
        out_specs=[pl.BlockSpec((1, t, LANES), lambda hp, bi, pi: (bi, pi, hp)),
                   pl.BlockSpec((1, t, LANES), lambda hp, bi, pi: (bi, n_p - 1 - pi, hp))],
        out_shape=[jax.ShapeDtypeStruct((b, s // 2, C_HEADS * C_DIM), F32)] * 2,
        scratch_shapes=[pltpu.VMEM((C_DIM + ONES_ROWS, s), BF16)],
        compiler_params=_cparams(("arbitrary", "arbitrary", "arbitrary")),
        name="nsa_selected",
    )(hb3, hb3, hb3, hb3, tb, sel, sel)
    return _join_halves(o_a, o_b)


def _win_attention(hb3, tb, k_blk, v_blk, n_prev):
    b, s, _ = hb3.shape
    t = T_ATT
    assert s >= (n_prev + 1) * t
    tiles = 2
    return pl.pallas_call(
        functools.partial(_win_kernel, n_prev=n_prev, tiles_per_step=tiles),
        grid=(C_HEADS // 2, b, s // (tiles * t)),
        in_specs=[pl.BlockSpec((1, tiles * t, LANES), lambda hp, bi, qi: (bi, qi, hp)),
                  pl.BlockSpec((1, s, LANES), lambda hp, bi, qi: (bi, 0, k_blk + hp // 2)),
                  pl.BlockSpec((1, s, LANES), lambda hp, bi, qi: (bi, 0, v_blk + hp // 2)),
                  pl.BlockSpec((2, 2 * n_prev + 1, t, t), lambda hp, bi, qi: (hp, 0, 0, 0))],
        out_specs=pl.BlockSpec((1, tiles * t, LANES), lambda hp, bi, qi: (bi, qi, hp)),
        out_shape=jax.ShapeDtypeStruct((b, s, C_HEADS * C_DIM), F32),
        scratch_shapes=[pltpu.VMEM((C_DIM + ONES_ROWS, s), BF16)],
        compiler_params=_cparams(("arbitrary", "arbitrary", "arbitrary")),
        name="nsa_window",
    )(hb3, hb3, hb3, tb)


def _rms(x, g):
    return x * lax.rsqrt(jnp.mean(x * x, axis=-1, keepdims=True) + RMS_EPS) * g


def _mla_prep_kernel(cq_ref, ckv_ref, kr1_ref, kr2_ref, cos_ref, sin_ref, gq_ref, gkv_ref,
                     wq1_ref, wq2_ref, wk_ref, wv_ref, q_ref, k_ref, v_ref):
    cqn = _rms(cq_ref[...], gq_ref[...]).astype(BF16)
    c = _rms(ckv_ref[...], gkv_ref[...]).astype(BF16)
    cos, sin = cos_ref[...], sin_ref[...]
    k_rope = kr1_ref[...] * cos + kr2_ref[...] * sin
    scale = (D_NOPE + D_ROPE) ** -0.5 * LOG2E
    for h in range(D_HEADS):
        cols = slice(h * LANES, (h + 1) * LANES)
        q = _dot(cqn, wq1_ref[:, cols]) * cos + _dot(cqn, wq2_ref[:, cols]) * sin
        q_ref[:, cols] = (q * scale).astype(q_ref.dtype)
        k_ref[:, cols] = (_dot(c, wk_ref[:, cols]) + k_rope).astype(k_ref.dtype)
    v_ref[...] = _dot(c, wv_ref[...]).astype(v_ref.dtype)


def _mla_prep(hf, cos, sin, gq, gkv, wq1, wq2, wk, wv, seq):
    m = hf.shape[0]
    n_seq = seq // TM
    const = lambda shape: pl.BlockSpec(shape, lambda i: (0, 0))
    hq = D_HEADS * LANES
    return pl.pallas_call(
        _mla_prep_kernel,
        grid=(m // TM,),
        in_specs=[pl.BlockSpec((TM, D_Q_LORA), lambda i: (i, 1)),
                  pl.BlockSpec((TM, D_KV_LORA), lambda i: (i, 3)),
                  pl.BlockSpec((TM, LANES), lambda i: (i, 8)),
                  pl.BlockSpec((TM, LANES), lambda i: (i, 9)),
                  pl.BlockSpec((TM, LANES), lambda i: (i % n_seq, 0)),
                  pl.BlockSpec((TM, LANES), lambda i: (i % n_seq, 0)),
                  const((1, D_Q_LORA)), const((1, D_KV_LORA)),
                  const(wq1.shape), const(wq2.shape), const(wk.shape), const(wv.shape)],
        out_specs=[pl.BlockSpec((TM, hq), lambda i: (i, 0)), pl.BlockSpec((TM, hq), lambda i: (i, 0)),
                   pl.BlockSpec((TM, D_HEADS * D_V), lambda i: (i, 0))],
        out_shape=[jax.ShapeDtypeStruct((m, hq), BF16), jax.ShapeDtypeStruct((m, hq), BF16),
                   jax.ShapeDtypeStruct((m, D_HEADS * D_V), BF16)],
        compiler_params=_cparams(("arbitrary",)),
        name="mla_prep",
    )(hf, hf, hf, hf, cos, sin, gq.reshape(1, -1), gkv.reshape(1, -1), wq1, wq2, wk, wv)


def _mla_kernel(qa_ref, qb_ref, k_ref, v_ref, oa_ref, ob_ref, vt_ref, *, n_d):
    t = T_ATT
    pi = pl.program_id(2)

    @pl.when(pi == 0)
    def _():
        _fill_transposed(vt_ref, v_ref, D_V)

    head_cols = [slice(half * LANES, (half + 1) * LANES) for half in range(2)]
    key_minus_query = (lax.broadcasted_iota(jnp.int32, (2 * t, t), 0)
                       - lax.broadcasted_iota(jnp.int32, (2 * t, t), 1))

    def score(n, q_tile, i, s, diag):
        return jnp.where(key_minus_query <= (q_tile - 2 * i) * t, s, NEG_INF) if diag else s

    out_a, out_b = _flash_balanced(pi, n_d, [qa_ref[0, :, c] for c in head_cols], [qb_ref[0, :, c] for c in head_cols],
                                   [k_ref, k_ref], head_cols, vt_ref,
                                   [_value_rows(half, D_V) for half in range(2)], score)
    oa_ref[0] = jnp.concatenate(out_a, axis=0).T.astype(oa_ref.dtype)
    ob_ref[0] = jnp.concatenate(out_b, axis=0).T.astype(ob_ref.dtype)


def _mla_attention(q3, k3, v3):
    b, s, _ = q3.shape
    t = T_ATT
    n_d = s // t
    n_p = n_d // 2
    o_a, o_b = pl.pallas_call(
        functools.partial(_mla_kernel, n_d=n_d),
        grid=(D_HEADS // 2, b, n_p),
        in_specs=[pl.BlockSpec((1, t, 2 * LANES), lambda hp, bi, pi: (bi, pi, hp)),
                  pl.BlockSpec((1, t, 2 * LANES), lambda hp, bi, pi: (bi, n_d - 1 - pi, hp)),
                  pl.BlockSpec((1, s, 2 * LANES), lambda hp, bi, pi: (bi, 0, hp)),
                  pl.BlockSpec((1, s, LANES), lambda hp, bi, pi: (bi, 0, hp))],
        out_specs=[pl.BlockSpec((1, t, LANES), lambda hp, bi, pi: (bi, pi, hp)),
                   pl.BlockSpec((1, t, LANES), lambda hp, bi, pi: (bi, n_p - 1 - pi, hp))],
        out_shape=[jax.ShapeDtypeStruct((b, s // 2, D_HEADS * D_V), BF16)] * 2,
        scratch_shapes=[pltpu.VMEM((2 * (D_V + ONES_ROWS), s), BF16)],
        compiler_params=_cparams(("arbitrary", "arbitrary", "arbitrary")),
        name="mla_attention",
    )(q3, q3, k3, v3)
    return _join_halves(o_a, o_b)


def _odd_out_kernel(gate_ref, oc_ref, os_ref, ow_ref, od_ref, x_ref, w_ref, g_ref, b_ref, out_ref):
    gates = 1.0 / (1.0 + jnp.exp(-gate_ref[...]))
    lo = _lane_lo(TM)
    n_c = C_HEADS * C_DIM
    y = _dot(od_ref[...], w_ref[n_c:, :])
    for blk in range(n_c // LANES):
        cols = slice(blk * LANES, (blk + 1) * LANES)
        acc = jnp.zeros((TM, LANES), F32)
        for br, o_ref in enumerate((oc_ref, os_ref, ow_ref)):
            c0 = br * C_HEADS + 2 * blk
            gexp = jnp.where(lo, gates[:, c0:c0 + 1], gates[:, c0 + 1:c0 + 2])
            acc = acc + gexp * o_ref[:, cols]
        y = y + _dot(acc.astype(BF16), w_ref[cols, :])
    out_ref[...] = _residual_ln(x_ref[...], y, g_ref[...], b_ref[...])


def _odd_out(hf, o_cmp, o_slc, o_win, o_d, x, w_out, g, beta):
    m = x.shape[0]
    n_c = C_HEADS * C_DIM
    row = lambda width: pl.BlockSpec((TM, width), lambda i: (i, 0))
    const = lambda shape: pl.BlockSpec(shape, lambda i: (0, 0))
    return pl.pallas_call(
        _odd_out_kernel,
        grid=(m // TM,),
        in_specs=[pl.BlockSpec((TM, LANES), lambda i: (i, 2)),
                  row(n_c), row(n_c), row(n_c), row(D_HEADS * D_V), row(D_MODEL),
                  const(w_out.shape), const((1, D_MODEL)), const((1, D_MODEL))],
        out_specs=row(D_MODEL),
        out_shape=jax.ShapeDtypeStruct((m, D_MODEL), F32),
        compiler_params=_cparams(("arbitrary",)),
        name="odd_out",
    )(hf, o_cmp, o_slc, o_win, o_d, x, w_out, g.reshape(1, -1), beta.reshape(1, -1))


ODD_BF16_W = 1536
ODD_F32_W = 1280


def _odd_in_columns():
    q_c = C_HEADS * C_DIM
    kv = lambda br, which, g: q_c + ((br * 2 + which) * C_KV_GROUPS + g) * C_DIM + np.arange(C_DIM)
    gate0 = q_c + 3 * 2 * C_KV_GROUPS * C_DIM
    cq0 = gate0 + 3 * C_HEADS
    ckv0 = cq0 + D_Q_LORA
    kr0 = ckv0 + D_KV_LORA
    zeros = lambda n: np.full(n, -1)
    cols = [np.arange(q_c)]
    for br in (1, 2):
        for which in (0, 1):
            for g in range(C_KV_GROUPS):
                cols += [kv(br, which, g), kv(br, which, g)]
    assert sum(len(c) for c in cols) == ODD_BF16_W
    for which in (0, 1):
        cols += [kv(0, which, 0), kv(0, which, 1)]
    cols += [gate0 + np.arange(3 * C_HEADS), zeros(LANES - 3 * C_HEADS)]
    cols += [cq0 + np.arange(D_Q_LORA), ckv0 + np.arange(D_KV_LORA)]
    half = D_ROPE // 2
    kr = kr0 + np.arange(D_ROPE)
    cols += [zeros(D_NOPE), kr, zeros(LANES - D_NOPE - D_ROPE)]
    cols += [zeros(D_NOPE), kr[half:], kr[:half], zeros(LANES - D_NOPE - D_ROPE)]
    cols = np.concatenate(cols)
    assert len(cols) == ODD_BF16_W + ODD_F32_W
    return cols


def _gather_cols(w, cols):
    return jnp.where(jnp.asarray(cols >= 0)[None, :], w[:, np.maximum(cols, 0)], 0.0)


def _mla_weight_columns():
    dq = D_NOPE + D_ROPE
    half = D_ROPE // 2
    zeros = lambda n: np.full(n, -1)
    q1, q2, k1 = [], [], []
    for h in range(D_HEADS):
        rope0 = h * dq + D_NOPE
        q1 += [h * dq + np.arange(dq), zeros(LANES - dq)]
        q2 += [zeros(D_NOPE), rope0 + half + np.arange(half), rope0 + np.arange(half), zeros(LANES - dq)]
        k1 += [h * D_NOPE + np.arange(D_NOPE), zeros(LANES - D_NOPE)]
    return np.concatenate(q1), np.concatenate(q2), np.concatenate(k1)


def _rope_tables(seq):
    half = D_ROPE // 2
    inv = ROPE_THETA ** (-jnp.arange(half, dtype=F32) / half)
    ang = jnp.arange(seq).astype(F32)[:, None] * inv
    cos, sin = jnp.cos(ang), jnp.sin(ang)
    pad = jnp.zeros((seq, LANES - D_NOPE - D_ROPE), F32)
    cos_t = jnp.concatenate([jnp.ones((seq, D_NOPE), F32), cos, cos, pad], axis=1)
    sin_t = jnp.concatenate([jnp.zeros((seq, D_NOPE), F32), -sin, sin, pad], axis=1)
    return cos_t, sin_t


def kernel(x, rel_bias, ev_w_in, ev_w_out, ev_lambda, ev_subln, od_w_in, od_w_out, od_cmp_pe, od_cmp_w1, od_cmp_w2, od_q_norm, od_kv_norm, od_w_uq, od_w_uk, od_w_uv, ffn_w_up, ffn_conv_w, ffn_conv_b, ffn_w_down, ln_g, ln_b):
    b, s, _ = x.shape
    m = b * s
    assert s % 1024 == 0 and s // B_PAIRS[-1][1] >= 2 * T_DIL and s // SLC_BLOCK <= LANES
    n_d = s // T_ATT

    tb_t = rel_bias.T.astype(F32)
    tb_log2 = tb_t[:C_HEADS] * LOG2E
    n_prev_win = -(-(WIN_SIZE - 1) // T_ATT)
    tb_causal = _bias_table(tb_log2, _toeplitz_idx(1, n_d, T_ATT, s))
    tb_win = _bias_table(tb_log2, _toeplitz_idx(n_prev_win, n_prev_win + 1, T_ATT, WIN_SIZE - 1))
    tb_cmp = _bias_table(tb_log2, _cmp_idx(s)).reshape(C_HEADS, s, s // CMP_STRIDE)
    tb_dil = [_bias_table(tb_t[A_HEADS + i * B_HEADS:A_HEADS + (i + 1) * B_HEADS], _dilated_idx(d))
              for i, (_, d) in enumerate(B_PAIRS)]
    cos_t, sin_t = _rope_tables(s)
    odd_cols = _odd_in_columns()
    q1_cols, q2_cols, k1_cols = _mla_weight_columns()

    xf = x.reshape(m, D_MODEL)
    for l in range(DEPTH):
        i = l // 2
        if l % 2 == 0:
            n_a = EVEN_IN - EVEN_DILATED
            w_in = ev_w_in[i] * jnp.where(jnp.arange(EVEN_IN) < 2 * A_HEADS * A_QK, LOG2E, 1.0)
            h_a, h_b = _inproj(xf, w_in.astype(BF16), (n_a, EVEN_DILATED), (BF16, F32), 768)
            h3 = h_a.reshape(b, s, n_a)
            hd3 = h_b.reshape(b, s, EVEN_DILATED)
            lam_init = 0.8 - 0.6 * math.exp(-0.3 * l)
            o_a = _diff_attention(h3, tb_causal, ev_lambda[i].astype(F32), ev_subln[i], lam_init)
            obs, lses = [], []
            for p_idx, (_, d) in enumerate(B_PAIRS):
                o, lse = _dilated_attention(hd3, tb_dil[p_idx], p_idx, d)
                obs.append(o)
                lses.append(lse)
            xf = _even_out(o_a.reshape(m, -1), obs, lses, xf, ev_w_out[i].astype(BF16), ln_g[l, 0], ln_b[l, 0])
        else:
            w_in = _gather_cols(od_w_in[i], odd_cols) * jnp.where(jnp.arange(len(odd_cols)) < C_HEADS * C_DIM, LOG2E, 1.0)
            hb, hf = _inproj(xf, w_in.astype(BF16), (ODD_BF16_W, ODD_F32_W), (BF16, F32), 256)
            hb3 = hb.reshape(b, s, ODD_BF16_W)
            n_ch = s // CMP_STRIDE
            chunks = (hf[:, :4 * C_DIM].reshape(b, n_ch, CMP_STRIDE, 4, C_DIM)
                      .transpose(0, 3, 1, 2, 4).reshape(b, 4, n_ch, CMP_STRIDE * C_DIM))
            pe = od_cmp_pe[i].reshape(2, 2, CMP_STRIDE * C_DIM)
            w2d = jnp.concatenate([od_cmp_w2[i], od_cmp_w2[i]], axis=-1).astype(BF16)
            kvc = _compress(chunks, pe, od_cmp_w1[i].astype(BF16), w2d)
            o_cmp, sel = _cmp_attention(hb3, kvc, tb_cmp)
            o_slc = _slc_attention(hb3, tb_causal, 4, 6, sel)
            o_win = _win_attention(hb3, tb_win, 8, 10, n_prev_win)
            q_d, k_d, v_d = _mla_prep(hf, cos_t, sin_t, od_q_norm[i], od_kv_norm[i],
                                      _gather_cols(od_w_uq[i], q1_cols).astype(BF16),
                                      _gather_cols(od_w_uq[i], q2_cols).astype(BF16),
                                      _gather_cols(od_w_uk[i], k1_cols).astype(BF16),
                                      od_w_uv[i].astype(BF16), s)
            o_d = _mla_attention(q_d.reshape(b, s, -1), k_d.reshape(b, s, -1), v_d.reshape(b, s, -1))
            xf = _odd_out(hf, o_cmp.reshape(m, -1), o_slc.reshape(m, -1), o_win.reshape(m, -1),
                          o_d.reshape(m, -1), xf, od_w_out[i].astype(BF16), ln_g[l, 0], ln_b[l, 0])
        xf = _ffn(xf, ffn_w_up[l].astype(BF16), ffn_conv_w[l], ffn_conv_b[l], ffn_w_down[l].astype(BF16),
                  ln_g[l, 1], ln_b[l, 1], s)
    return xf.reshape(b, s, D_MODEL)
```

```python
import functools
import math

import numpy as np
import jax
import jax.numpy as jnp
from jax import lax
from jax.experimental import pallas as pl
from jax.experimental.pallas import tpu as pltpu

F32 = jnp.float32
BF16 = jnp.bfloat16

D_MODEL = 1024
DEPTH = 4
LN_EPS = 1e-5
RMS_EPS = 1e-5
N_BUCKETS = 32
MAX_DISTANCE = 2048
A_HEADS, A_QK, A_V = 4, 64, 128
B_PAIRS = ((128, 1), (512, 4), (2048, 16))
B_HEADS, B_DIM = 4, 64
C_HEADS, C_KV_GROUPS, C_DIM = 8, 2, 64
CMP_LEN, CMP_STRIDE, CMP_HIDDEN = 32, 16, 256
SLC_BLOCK, SLC_TOP_N = 64, 16
SLC_SHIFT = 6
WIN_SIZE = 512
D_HEADS, D_Q_LORA, D_KV_LORA, D_NOPE, D_ROPE, D_V = 8, 384, 256, 64, 32, 64
ROPE_THETA = 10000.0
D_FF = 2816
EVEN_DILATED = len(B_PAIRS) * 3 * B_HEADS * B_DIM
EVEN_IN = 2 * A_HEADS * A_QK * 2 + A_HEADS * A_V + EVEN_DILATED
ALPHA = (2 * DEPTH) ** 0.25

LANES = 128
HALF_LANE_SHIFT = 6
VMEM_LIMIT = 56 * 1024 * 1024
T_ATT = 256
MLA_HEADS_PER_STEP = 4
T_DIL = 128
TM = 512
FF_CHUNK = 256
NEG_INF = float("-inf")
M_INIT = -1e30
SEL_PENALTY = 2.0 ** 101
LOG2E = math.log2(math.e)
ONES_ROWS = 16


def _cparams(sem, flags=None):
    return pltpu.CompilerParams(dimension_semantics=sem, vmem_limit_bytes=VMEM_LIMIT, flags=flags)


def _dot(a, b):
    return jnp.dot(a, b, preferred_element_type=F32)


def _dot_nt(a, b):
    return lax.dot_general(a, b, (((1,), (1,)), ((), ())), preferred_element_type=F32)


def _bucket_np(dist):
    n = np.maximum(dist, 0)
    nf = np.maximum(n, 1).astype(np.float32)
    max_exact = N_BUCKETS // 2
    large = max_exact + (np.log(nf / max_exact) / math.log(MAX_DISTANCE / max_exact)
                         * (N_BUCKETS - max_exact)).astype(np.int32)
    return np.where(n < max_exact, n, np.minimum(large, N_BUCKETS - 1)).astype(np.int32)


def _toeplitz_idx(n_masked, n_delta, t, max_dist):
    key = np.arange(t)[:, None]
    query = np.arange(t)[None, :]
    out = []
    for delta in range(-n_masked, n_delta):
        dist = t * delta + query - key
        ok = (dist >= 0) & (dist <= max_dist)
        out.append(np.where(ok, _bucket_np(dist), -1))
    return np.stack(out).astype(np.int32)


def _dilated_idx(d):
    t = T_DIL
    r = np.arange(t)[:, None]
    c = np.arange(2 * t)[None, :]
    dist0 = r - c
    ok0 = (dist0 >= 0) & (c < t)
    dist1 = t + r - c
    ok1 = (dist1 >= 0) & (dist1 <= t)
    return np.stack([np.where(ok0, _bucket_np(dist0 * d), -1),
                     np.where(ok1, _bucket_np(dist1 * d), -1)]).astype(np.int32)


def _cmp_idx(seq):
    ncb = seq // CMP_STRIDE - 1
    q = np.arange(seq)[:, None]
    c = np.arange(seq // CMP_STRIDE)[None, :]
    dist = q - (c * CMP_STRIDE + CMP_LEN - 1)
    ok = (dist >= 0) & (c < ncb)
    return np.where(ok, _bucket_np(dist), -1).astype(np.int32).reshape(seq // T_ATT, T_ATT, seq // CMP_STRIDE)


def _bias_kernel(tbl_ref, idx_ref, o_ref):
    idx = idx_ref[0]
    rows = idx.shape[0]
    row = jnp.broadcast_to(tbl_ref[0], (rows, LANES))
    for c0 in range(0, idx.shape[1], LANES):
        ix = idx[:, c0:c0 + LANES]
        val = jnp.take_along_axis(row, jnp.maximum(ix, 0), axis=1)
        o_ref[0, 0, :, c0:c0 + LANES] = jnp.where(ix < 0, NEG_INF, val)


def _bias_table(tbl, idx):
    n_h = tbl.shape[0]
    n, r, c = idx.shape
    tbl_rows = jnp.pad(tbl, ((0, 0), (0, LANES - N_BUCKETS))).reshape(n_h, 1, LANES)
    return pl.pallas_call(
        _bias_kernel,
        grid=(n_h, n),
        in_specs=[pl.BlockSpec((1, 1, LANES), lambda h, i: (h, 0, 0)),
                  pl.BlockSpec((1, r, c), lambda h, i: (i, 0, 0))],
        out_specs=pl.BlockSpec((1, 1, r, c), lambda h, i: (h, i, 0, 0)),
        out_shape=jax.ShapeDtypeStruct((n_h, n, r, c), F32),
        compiler_params=_cparams(("arbitrary", "arbitrary")),
        name="bias_table",
    )(tbl_rows, jnp.asarray(idx))


def _inproj_kernel(x_ref, w_ref, *o_refs, widths, chunk):
    xb = x_ref[...].astype(BF16)
    off = 0
    for o_ref, width in zip(o_refs, widths):
        for c0 in range(0, width, chunk):
            o_ref[:, c0:c0 + chunk] = _dot(xb, w_ref[:, off + c0:off + c0 + chunk]).astype(o_ref.dtype)
        off += width


def _inproj(x, w, widths, dtypes, chunk):
    m = x.shape[0]
    n = w.shape[1]
    assert sum(widths) == n and all(wd % chunk == 0 for wd in widths)
    return pl.pallas_call(
        functools.partial(_inproj_kernel, widths=widths, chunk=chunk),
        grid=(m // TM,),
        in_specs=[pl.BlockSpec((TM, D_MODEL), lambda i: (i, 0)),
                  pl.BlockSpec((D_MODEL, n), lambda i: (0, 0))],
        out_specs=[pl.BlockSpec((TM, wd), lambda i: (i, 0)) for wd in widths],
        out_shape=[jax.ShapeDtypeStruct((m, wd), dt) for wd, dt in zip(widths, dtypes)],
        compiler_params=_cparams(("arbitrary",)),
        name="in_proj",
    )(x, w)


def _tree_reduce_rows(x, combine, reduce_fn):
    parts = [x[r:r + 8] for r in range(0, x.shape[0], 8)]
    while len(parts) > 1:
        parts = [combine(parts[k], parts[k + 1]) for k in range(0, len(parts) - 1, 2)] + \
                ([parts[-1]] if len(parts) % 2 else [])
    return reduce_fn(parts[0], axis=0, keepdims=True)


def _flash_init(streams):
    return tuple((jnp.full((1, q.shape[0]), M_INIT, F32), jnp.zeros((vt_rows.stop - vt_rows.start, q.shape[0]), F32))
                 for q, _, _, _, vt_rows in streams)


def _normalise(acc):
    d_v = acc.shape[0] - ONES_ROWS
    return acc[:d_v] / acc[d_v:d_v + 1]


def _flash_scores(streams, key_start, n_keys):
    keys = pl.ds(pl.multiple_of(key_start, T_ATT), n_keys)
    return tuple(_dot_nt(k_ref[0, keys, k_cols], q) for q, k_ref, k_cols, _, _ in streams)


def _flash_update(streams, score_fns, key_start, n_keys, i, raw, state):
    keys = pl.ds(pl.multiple_of(key_start, T_ATT), n_keys)
    scores = [fn(i, s) for fn, s in zip(score_fns, raw)]
    m_new = [jnp.maximum(m, _tree_reduce_rows(s, jnp.maximum, jnp.max)) for s, (m, _) in zip(scores, state)]
    probs = [jnp.exp2(s - mn) for s, mn in zip(scores, m_new)]
    out = []
    for (_, _, _, vt_ref, vt_rows), p, mn, (m, acc) in zip(streams, probs, m_new, state):
        acc = jnp.exp2(m - mn) * acc + _dot(vt_ref[vt_rows, keys], p.astype(BF16))
        out.append((mn, acc))
    return tuple(out)


def _flash_finish(state):
    return [_normalise(acc) for _, acc in state]


def _flash_balanced(pi, n_d, qa, qb, k_refs, k_cols, vt_ref, vt_rows, score, key_extra=None):
    kt = 2 * T_ATT
    t_q = qa[0].shape[0]
    n_streams = len(qa)
    n_slots = n_d // 2 + 1
    max_a = n_d // 4
    n_a = pi // 2 + 1
    tile_a, tile_b = pi, n_d - 1 - pi

    def slot(s):
        if s == 0:
            return True, tile_a, n_a - 1
        if s >= max_a:
            return False, tile_b, s - n_a
        is_a = s < n_a
        return is_a, jnp.where(is_a, tile_a, tile_b), jnp.where(is_a, n_a - 1 - s, s - n_a)

    def pick(is_a, a, b):
        return (a if is_a else b) if isinstance(is_a, bool) else jnp.where(is_a, a, b)

    def keys_of(i):
        return pl.ds(pl.multiple_of(i * kt, kt), kt)

    def raw_scores(s):
        is_a, _, i = slot(s)
        extra = None if key_extra is None else key_extra(i)
        out = []
        for n in range(n_streams):
            keys = k_refs[n][0, keys_of(i), k_cols[n]]
            if extra is not None:
                keys = jnp.concatenate([keys, extra], axis=1)
            out.append(_dot_nt(keys, pick(is_a, qa[n], qb[n])))
        return out

    state = [(jnp.full((1, t_q), M_INIT, F32), jnp.zeros((vt_rows[n].stop - vt_rows[n].start, t_q), F32))
             for n in range(n_streams)]
    out_a = [jnp.zeros((acc.shape[0] - ONES_ROWS, t_q), F32) for _, acc in state]
    pending = None
    raw = raw_scores(0)

    def fold(state, pending):
        alphas, probs, i_prev = pending
        return [(m, alphas[n] * acc + _dot(vt_ref[vt_rows[n], keys_of(i_prev)], probs[n]))
                for n, (m, acc) in enumerate(state)]

    for s in range(n_slots):
        raw_next = raw_scores(s + 1) if s + 1 < n_slots else None
        is_a, q_tile, i = slot(s)
        if pending is not None:
            state = fold(state, pending)
        if 1 <= s <= max_a:
            switch = s == n_a
            out_a = [jnp.where(switch, _normalise(acc), o) for (_, acc), o in zip(state, out_a)]
            state = [(jnp.where(switch, M_INIT, m), jnp.where(switch, 0.0, acc)) for m, acc in state]
        diag = s == 0 or s == n_slots - 1
        scores = [score(n, q_tile, i, raw[n], diag) for n in range(n_streams)]
        m_new = [jnp.maximum(m, _tree_reduce_rows(sc, jnp.maximum, jnp.max)) for sc, (m, _) in zip(scores, state)]
        probs = [jnp.exp2(sc - mn).astype(BF16) for sc, mn in zip(scores, m_new)]
        alphas = [jnp.exp2(m - mn) for (m, _), mn in zip(state, m_new)]
        state = [(mn, acc) for mn, (_, acc) in zip(m_new, state)]
        pending = (alphas, probs, i)
        raw = raw_next
    state = fold(state, pending)
    return out_a, [_normalise(acc) for _, acc in state]


def _pair_bias(tb_ref, h, q_tile, i, s):
    t = T_ATT
    d0 = q_tile - 2 * i
    return jnp.concatenate([s[:t] + tb_ref[h, d0 + 1], s[t:] + tb_ref[h, d0]], axis=0)


def _fill_transposed(vt_ref, v_ref, d_v):
    n = v_ref.shape[1]
    n_heads = vt_ref.shape[0] // (d_v + ONES_ROWS)
    step = 2 * T_ATT
    for c0 in range(0, n, step):
        v_t = v_ref[0, c0:c0 + step, :].astype(F32).T.astype(BF16)
        for h in range(n_heads):
            r0 = h * (d_v + ONES_ROWS)
            vt_ref[r0:r0 + d_v, c0:c0 + step] = v_t[h * d_v:(h + 1) * d_v]
            vt_ref[r0 + d_v:r0 + d_v + ONES_ROWS, c0:c0 + step] = jnp.ones((ONES_ROWS, step), BF16)


def _value_rows(h, d_v):
    return slice(h * (d_v + ONES_ROWS), (h + 1) * (d_v + ONES_ROWS))


def _lane_lo(rows):
    return lax.broadcasted_iota(jnp.int32, (rows, LANES), 1) < (LANES // 2)


def _diff_kernel(lam_ref, g_ref, q0a_ref, q0b_ref, q1a_ref, q1b_ref, k0_ref, k1_ref, v_ref, tb_ref,
                 oa_ref, ob_ref, vt_ref, *, lam_init, n_d):
    pi = pl.program_id(2)

    @pl.when(pi == 0)
    def _():
        _fill_transposed(vt_ref, v_ref, A_V)

    lp = lam_ref[...]
    lam = (jnp.exp(jnp.sum(lp[0:1] * lp[1:2], axis=-1, keepdims=True))
           - jnp.exp(jnp.sum(lp[2:3] * lp[3:4], axis=-1, keepdims=True)) + lam_init)
    lo = _lane_lo(T_ATT)

    def prep(q_ref):
        q = q_ref[0] * (A_QK ** -0.5)
        zero = jnp.zeros_like(q)
        return [jnp.where(lo, q, zero), jnp.where(lo, zero, q)]

    full = slice(0, LANES)

    def score(n, q_tile, i, s, diag):
        return _pair_bias(tb_ref, n % 2, q_tile, i, s)

    out_a, out_b = _flash_balanced(pi, n_d, prep(q0a_ref) + prep(q1a_ref), prep(q0b_ref) + prep(q1b_ref),
                                   [k0_ref, k0_ref, k1_ref, k1_ref], [full] * 4, vt_ref,
                                   [_value_rows(0, A_V), _value_rows(1, A_V)] * 2, score)
    for outs, o_ref in ((out_a, oa_ref), (out_b, ob_ref)):
        for head in range(2):
            d = (outs[head] - lam * outs[2 + head]).T
            ms = jnp.mean(d * d, axis=-1, keepdims=True)
            o_ref[0, :, head * A_V:(head + 1) * A_V] = (d * lax.rsqrt(ms + RMS_EPS) * g_ref[...]
                                                        * (1.0 - lam_init)).astype(o_ref.dtype)


def _join_halves(o_a, o_b):
    return jnp.concatenate([o_a, o_b], axis=1)


def _diff_attention(h3, tb, lam_p, subln_g, lam_init):
    b, s, _ = h3.shape
    t = T_ATT
    n_d = s // t
    n_p = n_d // 2
    q_spec = lambda col0, second: pl.BlockSpec(
        (1, t, LANES), lambda hp, bi, pi: (bi, (n_d - 1 - pi) if second else pi, col0 + hp))
    k_spec = lambda col0: pl.BlockSpec((1, s, LANES), lambda hp, bi, pi: (bi, 0, col0 + hp))
    o_a, o_b = pl.pallas_call(
        functools.partial(_diff_kernel, lam_init=lam_init, n_d=n_d),
        grid=(A_HEADS // 2, b, n_p),
        in_specs=[pl.BlockSpec((4, A_QK), lambda hp, bi, pi: (0, 0)),
                  pl.BlockSpec((1, A_V), lambda hp, bi, pi: (0, 0)),
                  q_spec(0, False), q_spec(0, True), q_spec(2, False), q_spec(2, True),
                  k_spec(4), k_spec(6),
                  pl.BlockSpec((1, s, 2 * A_V), lambda hp, bi, pi: (bi, 0, 4 + hp)),
                  pl.BlockSpec((2, n_d + 1, t, t), lambda hp, bi, pi: (hp, 0, 0, 0))],
        out_specs=[pl.BlockSpec((1, t, 2 * A_V), lambda hp, bi, pi: (bi, pi, hp)),
                   pl.BlockSpec((1, t, 2 * A_V), lambda hp, bi, pi: (bi, n_p - 1 - pi, hp))],
        out_shape=[jax.ShapeDtypeStruct((b, s // 2, A_HEADS * A_V), BF16)] * 2,
        scratch_shapes=[pltpu.VMEM((2 * (A_V + ONES_ROWS), s), BF16)],
        compiler_params=_cparams(("arbitrary", "arbitrary", "arbitrary")),
        name="diff_attention",
    )(lam_p, subln_g.reshape(1, A_V), h3, h3, h3, h3, h3, h3, h3, tb)
    return _join_halves(o_a, o_b)


def _dilated_kernel(q_ref, k_ref, v_ref, tb_ref, o_ref, lse_ref, *, d, residues):
    t = T_DIL
    n_blocks = q_ref.shape[1] // (d * t)
    lo = _lane_lo(t)
    scale = B_DIM ** -0.5

    def block(idx, carry):
        r = pl.program_id(2) * residues + idx // n_blocks
        bi = idx % n_blocks
        var = jnp.minimum(bi, 1)
        qrows = pl.ds(r + d * t * bi, t, stride=d)
        krows = pl.ds(r + d * t * jnp.maximum(bi - 1, 0), 2 * t, stride=d)
        q = (q_ref[0, qrows, :] * scale).astype(BF16)
        kk = k_ref[0, krows, :].astype(BF16)
        vv = v_ref[0, krows, :].astype(BF16)
        o_half, l_half = [], []
        for half in range(2):
            keep = lo if half == 0 else jnp.logical_not(lo)
            qm = jnp.where(keep, q, jnp.zeros_like(q))
            s = _dot_nt(qm, kk) + tb_ref[half, var]
            m = jnp.max(s, axis=-1, keepdims=True)
            p = jnp.exp(s - m)
            l = jnp.sum(p, axis=-1, keepdims=True)
            o_half.append(_dot(p.astype(BF16), vv) / l)
            l_half.append(m + jnp.log(l))
        o_ref[0, qrows, :] = jnp.where(lo, o_half[0], o_half[1])
        lse_ref[0, qrows, :] = jnp.where(lo, l_half[0], l_half[1])
        return carry

    lax.fori_loop(0, residues * n_blocks, block, 0, unroll=4)


def _dilated_attention(hd3, tb, pair_idx, d):
    b, s, _ = hd3.shape
    hd = B_HEADS * B_DIM
    base = pair_idx * 3 * (hd // LANES)
    seq = lambda which: pl.BlockSpec((1, s, LANES), lambda bi, hp, r: (bi, 0, base + which * (hd // LANES) + hp))
    out = pl.BlockSpec((1, s, LANES), lambda bi, hp, r: (bi, 0, hp))
    n_blocks = s // (d * T_DIL)
    residues = min(d, max(1, 8 // n_blocks))
    assert d % residues == 0 and (residues * n_blocks) % 4 == 0
    o, lse = pl.pallas_call(
        functools.partial(_dilated_kernel, d=d, residues=residues),
        grid=(b, hd // LANES, d // residues),
        in_specs=[seq(0), seq(1), seq(2),
                  pl.BlockSpec((2, 2, T_DIL, 2 * T_DIL), lambda bi, hp, r: (hp, 0, 0, 0))],
        out_specs=[out, out],
        out_shape=[jax.ShapeDtypeStruct((b, s, hd), F32)] * 2,
        compiler_params=_cparams(("arbitrary", "arbitrary", "arbitrary")),
        name="dilated_attention",
    )(hd3, hd3, hd3, tb)
    return o.reshape(b * s, hd), lse.reshape(b * s, hd)


def _residual_ln(x, y, g, beta):
    z = ALPHA * x + y
    mu = jnp.mean(z, axis=-1, keepdims=True)
    zc = z - mu
    var = jnp.mean(zc * zc, axis=-1, keepdims=True)
    return zc * lax.rsqrt(var + LN_EPS) * g + beta


def _even_out_kernel(oa_ref, o0_ref, o1_ref, o2_ref, l0_ref, l1_ref, l2_ref, x_ref, w_ref, g_ref, b_ref, out_ref):
    l0, l1, l2 = l0_ref[...], l1_ref[...], l2_ref[...]
    mx = jnp.maximum(jnp.maximum(l0, l1), l2)
    e0, e1, e2 = jnp.exp(l0 - mx), jnp.exp(l1 - mx), jnp.exp(l2 - mx)
    den = e0 + e1 + e2
    ob = (e0 / den) * o0_ref[...] + (e1 / den) * o1_ref[...] + (e2 / den) * o2_ref[...]
    n_a = A_HEADS * A_V
    y = _dot(oa_ref[...], w_ref[0:n_a, :]) + _dot(ob.astype(BF16), w_ref[n_a:, :])
    out_ref[...] = _residual_ln(x_ref[...], y, g_ref[...], b_ref[...])


def _even_out(oa, obs, lses, x, w_out, g, beta):
    m = x.shape[0]
    hd = B_HEADS * B_DIM
    row = lambda width: pl.BlockSpec((TM, width), lambda i: (i, 0))
    const = lambda shape: pl.BlockSpec(shape, lambda i: (0, 0))
    return pl.pallas_call(
        _even_out_kernel,
        grid=(m // TM,),
        in_specs=[row(A_HEADS * A_V)] + [row(hd)] * 6 + [row(D_MODEL), const(w_out.shape),
                                                        const((1, D_MODEL)), const((1, D_MODEL))],
        out_specs=row(D_MODEL),
        out_shape=jax.ShapeDtypeStruct((m, D_MODEL), F32),
        compiler_params=_cparams(("arbitrary",)),
        name="even_out",
    )(oa, *obs, *lses, x, w_out, g.reshape(1, -1), beta.reshape(1, -1))


def _gelu(x):
    return 0.5 * x * (1.0 + jnp.tanh(math.sqrt(2.0 / math.pi) * (x + 0.044715 * (x * x * x))))


def _ffn_kernel(x_ref, wu_ref, cw_ref, cb_ref, wd_ref, g_ref, b_ref, out_ref, u_ref, gs_ref, tail_ref,
                *, tiles_per_seq):
    halo = 8
    n_chunks = D_FF // FF_CHUNK

    @pl.when(pl.program_id(0) % tiles_per_seq == 0)
    def _():
        tail_ref[...] = jnp.zeros(tail_ref.shape, F32)

    x = x_ref[...]
    xb = x.astype(BF16)

    def up(c):
        cols = slice(c * FF_CHUNK, (c + 1) * FF_CHUNK)
        return _dot(xb, wu_ref[:, cols]), _dot(xb, wu_ref[:, D_FF + c * FF_CHUNK:D_FF + (c + 1) * FF_CHUNK])

    def activate(c, a, gate):
        cols = slice(c * FF_CHUNK, (c + 1) * FF_CHUNK)
        gs = gs_ref.at[c % 2]
        gs[0:halo, :] = tail_ref[:, cols]
        gs[halo:, :] = gate
        tail_ref[:, cols] = gate[TM - halo:, :]
        conv = (gs[pl.ds(halo - 2, TM), :] * cw_ref[0:1, cols] + gs[pl.ds(halo - 1, TM), :] * cw_ref[1:2, cols]
                + gate * cw_ref[2:3, cols] + cb_ref[:, cols])
        u_ref[:, cols] = (_gelu(conv) * a).astype(BF16)

    nxt = up(0)
    for c in range(n_chunks):
        cur = nxt
        if c + 1 < n_chunks:
            nxt = up(c + 1)
        activate(c, *cur)
    out_ref[...] = _residual_ln(x, _dot(u_ref[...], wd_ref[...]), g_ref[...], b_ref[...])


def _ffn(x, w_up, conv_w, conv_b, w_down, g, beta, seq):
    m = x.shape[0]
    const = lambda shape: pl.BlockSpec(shape, lambda i: (0, 0))
    return pl.pallas_call(
        functools.partial(_ffn_kernel, tiles_per_seq=seq // TM),
        grid=(m // TM,),
        in_specs=[pl.BlockSpec((TM, D_MODEL), lambda i: (i, 0)), const(w_up.shape), const(conv_w.shape),
                  const((1, D_FF)), const(w_down.shape), const((1, D_MODEL)), const((1, D_MODEL))],
        out_specs=pl.BlockSpec((TM, D_MODEL), lambda i: (i, 0)),
        out_shape=jax.ShapeDtypeStruct((m, D_MODEL), F32),
        scratch_shapes=[pltpu.VMEM((TM, D_FF), BF16), pltpu.VMEM((2, TM + 8, FF_CHUNK), F32),
                        pltpu.VMEM((8, D_FF), F32)],
        compiler_params=_cparams(("arbitrary",)),
        name="conv_ffn",
    )(x, w_up, conv_w, conv_b.reshape(1, -1), w_down, g.reshape(1, -1), beta.reshape(1, -1))


def _compress_kernel(ch_ref, pe_ref, w1_ref, w2_ref, o_ref):
    half = CMP_STRIDE * C_DIM
    ch = ch_ref[0, 0]
    a = _dot((ch + pe_ref[0, 0:1, :]).astype(BF16), w1_ref[0, 0:half, :])
    b = _dot((ch + pe_ref[0, 1:2, :]).astype(BF16), w1_ref[0, half:, :])
    n = ch.shape[0]
    hid = _gelu(a + pltpu.roll(b, n - 1, 0))
    o_ref[0, 0] = _dot(hid.astype(BF16), w2_ref[0]).astype(o_ref.dtype)


def _compress(chunks, pe, w1, w2d):
    b, _, n, width = chunks.shape
    return pl.pallas_call(
        _compress_kernel,
        grid=(b, 4),
        in_specs=[pl.BlockSpec((1, 1, n, width), lambda bi, j: (bi, j, 0, 0)),
                  pl.BlockSpec((1, 2, width), lambda bi, j: (j // 2, 0, 0)),
                  pl.BlockSpec((1, 2 * width, CMP_HIDDEN), lambda bi, j: (j // 2, 0, 0)),
                  pl.BlockSpec((1, CMP_HIDDEN, LANES), lambda bi, j: (j // 2, 0, 0))],
        out_specs=pl.BlockSpec((1, 1, n, LANES), lambda bi, j: (bi, j, 0, 0)),
        out_shape=jax.ShapeDtypeStruct((b, 4, n, LANES), BF16),
        compiler_params=_cparams(("arbitrary", "arbitrary")),
        name="nsa_compress",
    )(chunks, pe, w1, w2d)


def _cmp_kernel(q_ref, kc_ref, vc_ref, tb_ref, o_ref, sel_ref, pg_ref, sc_ref, pen_ref):
    t = T_ATT
    qi = pl.program_id(1)
    n_c = kc_ref.shape[2]
    n_sel = n_c // (SLC_BLOCK // CMP_STRIDE)
    lo = _lane_lo(t)
    kc = kc_ref[0, 0]
    vc = vc_ref[0, 0]
    raw = []
    for pair in range(2):
        q = q_ref[0, :, pair * LANES:(pair + 1) * LANES] * (C_DIM ** -0.5)
        zero = jnp.zeros_like(q)
        raw += [_dot_nt(jnp.where(lo, q, zero), kc), _dot_nt(jnp.where(lo, zero, q), kc)]
    probs = []
    for head, s in enumerate(raw):
        s = s + tb_ref[head]
        m = jnp.maximum(jnp.max(s, axis=-1, keepdims=True), M_INIT)
        p = jnp.exp2(s - m)
        den = jnp.sum(p, axis=-1, keepdims=True)
        probs.append(p / jnp.where(den > 0, den, 1.0))
    outs = [_dot(p.astype(BF16), vc) for p in probs]
    for pair in range(2):
        o_ref[0, :, pair * LANES:(pair + 1) * LANES] = jnp.where(lo, outs[2 * pair], outs[2 * pair + 1])
    pg = (probs[0] + probs[1]) + (probs[2] + probs[3])
    pad = 8
    pg_t = pg.T
    for c in range(t // LANES):
        pg_ref[c, 0:pad, :] = jnp.zeros((pad, LANES), F32)
        pg_ref[c, pad:pad + n_c, :] = pg_t[:, c * LANES:(c + 1) * LANES]
        pg_ref[c, pad + n_c:, :] = jnp.zeros((pad, LANES), F32)
    r = SLC_BLOCK // CMP_STRIDE
    tap = lambda k: jnp.concatenate(
        [pg_ref[c, pl.ds(pad + k, n_sel, stride=r), :] for c in range(t // LANES)], axis=1)
    score = (0.5 * tap(-1) + ((tap(0) + tap(1)) + tap(2))) + 0.5 * tap(3)
    jb = lax.broadcasted_iota(jnp.int32, (n_sel, t), 0)
    qblk = jnp.right_shift(qi * t + lax.broadcasted_iota(jnp.int32, (n_sel, t), 1), SLC_SHIFT)
    forced = (jb == 0) | (jb == qblk) | (jb == qblk - 1)
    sc = jnp.where(forced, jnp.inf, jnp.where(jb <= qblk, score, NEG_INF))
    sc_ref[...] = sc
    pen_ref[...] = jnp.zeros(pen_ref.shape, F32)
    groups = range(0, n_sel, 8)
    sc_g = [sc_ref[g0:g0 + 8, :] for g0 in groups]
    sub = lax.broadcasted_iota(jnp.int32, (8, t), 0)
    cnt_g = [jnp.zeros((8, t), jnp.int32) for _ in groups]
    for i in range(n_sel):
        row = jnp.broadcast_to(sc_ref[pl.ds(i, 1), :], (8, t))
        for k, g0 in enumerate(groups):
            if i < g0:
                beats = row >= sc_g[k]
            elif i >= g0 + 8:
                beats = row > sc_g[k]
            else:
                beats = (row > sc_g[k]) | ((row == sc_g[k]) & (i - g0 < sub))
            cnt_g[k] = cnt_g[k] + beats.astype(jnp.int32)
    for k, g0 in enumerate(groups):
        pen_ref[g0:g0 + 8, :] = jnp.where(cnt_g[k] < SLC_TOP_N, 0.0, -SEL_PENALTY)
    sel_ref[0, 0] = pen_ref[...].T.astype(sel_ref.dtype)


def _cmp_attention(hb3, kvc, tb):
    b, s, _ = hb3.shape
    t = T_ATT
    n_c = s // CMP_STRIDE
    n_sel = s // SLC_BLOCK
    assert n_sel <= LANES
    gw = (C_HEADS // C_KV_GROUPS) * C_DIM
    return pl.pallas_call(
        _cmp_kernel,
        grid=(C_KV_GROUPS, s // t, b),
        in_specs=[pl.BlockSpec((1, t, gw), lambda g, qi, bi: (bi, qi, g)),
                  pl.BlockSpec((1, 1, n_c, LANES), lambda g, qi, bi: (bi, g, 0, 0)),
                  pl.BlockSpec((1, 1, n_c, LANES), lambda g, qi, bi: (bi, 2 + g, 0, 0)),
                  pl.BlockSpec((4, t, n_c), lambda g, qi, bi: (g, qi, 0))],
        out_specs=[pl.BlockSpec((1, t, gw), lambda g, qi, bi: (bi, qi, g)),
                   pl.BlockSpec((1, 1, t, LANES), lambda g, qi, bi: (bi, g, qi, 0))],
        out_shape=[jax.ShapeDtypeStruct((b, s, C_HEADS * C_DIM), F32),
                   jax.ShapeDtypeStruct((b, C_KV_GROUPS, s, LANES), BF16)],
        scratch_shapes=[pltpu.VMEM((t // LANES, n_c + 16, LANES), F32), pltpu.VMEM((n_sel, t), F32),
                        pltpu.VMEM((LANES, t), F32)],
        compiler_params=_cparams(("arbitrary", "arbitrary", "arbitrary")),
        name="nsa_cmp_attention",
    )(hb3, kvc, kvc, tb)


def _head_pair_queries(q_ref):
    lo = _lane_lo(T_ATT)
    q = q_ref[0] * (C_DIM ** -0.5)
    zero = jnp.zeros_like(q)
    return [jnp.where(lo, q, zero), jnp.where(lo, zero, q)]


def _slc_kernel(qa_ref, qb_ref, k_ref, v_ref, tb_ref, sela_ref, selb_ref, oa_ref, ob_ref, vt_ref, *, n_d):
    t = T_ATT
    pi = pl.program_id(2)

    @pl.when(pi == 0)
    def _():
        _fill_transposed(vt_ref, v_ref, C_DIM)

    key_blk =jnp.right_shift(lax.broadcasted_iota(jnp.int32, (2 * t, LANES), 0), SLC_SHIFT)
    blk_slot = lax.broadcasted_iota(jnp.int32, (2 * t, LANES), 1)

    def one_hot_block(i):
        return jnp.where(blk_slot == key_blk + i * (2 * t // SLC_BLOCK), 1.0, 0.0).astype(BF16)

    def with_penalty(q_ref, sel_ref):
        return [jnp.concatenate([q, sel_ref[0, 0]], axis=1) for q in _head_pair_queries(q_ref)]

    def score(n, q_tile, i, s, diag):
        return _pair_bias(tb_ref, n, q_tile, i, s)

    full = slice(0, LANES)
    out_a, out_b = _flash_balanced(pi, n_d, with_penalty(qa_ref, sela_ref), with_penalty(qb_ref, selb_ref),
                                   [k_ref, k_ref], [full, full], vt_ref, [_value_rows(0, C_DIM)] * 2, score,
                                   key_extra=one_hot_block)
    oa_ref[0] = jnp.concatenate(out_a, axis=0).T
    ob_ref[0] = jnp.concatenate(out_b, axis=0).T


def _win_kernel(q_ref, k_ref, v_ref, tb_ref, o_ref, vt_ref, *, n_prev, tiles_per_step):
    t = T_ATT
    step = pl.program_id(2)

    @pl.when(step == 0)
    def _():
        _fill_transposed(vt_ref, v_ref, C_DIM)

    full = slice(0, LANES)
    n_keys = (n_prev + 1) * t
    lo = _lane_lo(t)
    work = []
    for u_tile in range(tiles_per_step):
        qi = step * tiles_per_step + u_tile
        q = q_ref[0, u_tile * t:(u_tile + 1) * t, :] * (C_DIM ** -0.5)
        zero = jnp.zeros_like(q)
        streams = [(qh, k_ref, full, vt_ref, _value_rows(0, C_DIM))
                   for qh in (jnp.where(lo, q, zero), jnp.where(lo, zero, q))]
        j0 = jnp.maximum(qi - n_prev, 0)
        work.append((qi, j0, streams, _flash_scores(streams, j0 * t, n_keys)))
    for u_tile, (qi, j0, streams, raw) in enumerate(work):
        def score_fn(h, qi=qi, j0=j0):
            def fn(_, s):
                return jnp.concatenate([s[u * t:(u + 1) * t] + tb_ref[h, qi - j0 - u + n_prev]
                                        for u in range(n_prev + 1)], axis=0)
            return fn

        state = _flash_update(streams, [score_fn(0), score_fn(1)], j0 * t, n_keys, 0, raw, _flash_init(streams))
        o_ref[0, u_tile * t:(u_tile + 1) * t, :] = jnp.concatenate(_flash_finish(state), axis=0).T


def _slc_attention(hb3, tb, k_blk, v_blk, sel):
    b, s, _ = hb3.shape
    t = T_ATT
    n_d = s // t
    n_p = n_d // 2
    o_a, o_b = pl.pallas_call(
        functools.partial(_slc_kernel, n_d=n_d),
        grid=(C_HEADS // 2, b, n_p),
        in_specs=[pl.BlockSpec((1, t, LANES), lambda hp, bi, pi: (bi, pi, hp)),
                  pl.BlockSpec((1, t, LANES), lambda hp, bi, pi: (bi, n_d - 1 - pi, hp)),
                  pl.BlockSpec((1, s, LANES), lambda hp, bi, pi: (bi, 0, k_blk + hp // 2)),
                  pl.BlockSpec((1, s, LANES), lambda hp, bi, pi: (bi, 0, v_blk + hp // 2)),
                  pl.BlockSpec((2, n_d + 1, t, t), lambda hp, bi, pi: (hp, 0, 0, 0)),
                  pl.BlockSpec((1, 1, t, LANES), lambda hp, bi, pi: (bi, hp // 2, pi, 0)),
                  pl.BlockSpec((1, 1, t, LANES), lambda hp, bi, pi: (bi, hp // 2, n_d - 1 - pi, 0))],
        out_specs=[pl.BlockSpec((1, t, LANES), lambda hp, bi, pi: (bi, pi, hp)),
                   pl.BlockSpec((1, t, LANES), lambda hp, bi, pi: (bi, n_p - 1 - pi, hp))],
        out_shape=[jax.ShapeDtypeStruct((b, s // 2, C_HEADS * C_DIM), F32)] * 2,
        scratch_shapes=[pltpu.VMEM((C_DIM + ONES_ROWS, s), BF16)],
        compiler_params=_cparams(("arbitrary", "arbitrary", "arbitrary")),
        name="nsa_selected",
    )(hb3, hb3, hb3, hb3, tb, sel, sel)
    return _join_halves(o_a, o_b)


def _win_attention(hb3, tb, k_blk, v_blk, n_prev):
    b, s, _ = hb3.shape
    t = T_ATT
    assert s >= (n_prev + 1) * t
    tiles = 2
    return pl.pallas_call(
        functools.partial(_win_kernel, n_prev=n_prev, tiles_per_step=tiles),
        grid=(C_HEADS // 2, b, s // (tiles * t)),
        in_specs=[pl.BlockSpec((1, tiles * t, LANES), lambda hp, bi, qi: (bi, qi, hp)),
                  pl.BlockSpec((1, s, LANES), lambda hp, bi, qi: (bi, 0, k_blk + hp // 2)),
                  pl.BlockSpec((1, s, LANES), lambda hp, bi, qi: (bi, 0, v_blk + hp // 2)),
                  pl.BlockSpec((2, 2 * n_prev + 1, t, t), lambda hp, bi, qi: (hp, 0, 0, 0))],
        out_specs=pl.BlockSpec((1, tiles * t, LANES), lambda hp, bi, qi: (bi, qi, hp)),
        out_shape=jax.ShapeDtypeStruct((b, s, C_HEADS * C_DIM), F32),
        scratch_shapes=[pltpu.VMEM((C_DIM + ONES_ROWS, s), BF16)],
        compiler_params=_cparams(("arbitrary", "arbitrary", "arbitrary")),
        name="nsa_window",
    )(hb3, hb3, hb3, tb)


def _rms(x, g):
    return x * lax.rsqrt(jnp.mean(x * x, axis=-1, keepdims=True) + RMS_EPS) * g


def _mla_prep_kernel(cq_ref, ckv_ref, kr1_ref, kr2_ref, cos_ref, sin_ref, gq_ref, gkv_ref,
                     wq1_ref, wq2_ref, wk_ref, wv_ref, q_ref, k_ref, v_ref):
    cqn = _rms(cq_ref[...], gq_ref[...]).astype(BF16)
    c = _rms(ckv_ref[...], gkv_ref[...]).astype(BF16)
    cos, sin = cos_ref[...], sin_ref[...]
    k_rope = kr1_ref[...] * cos + kr2_ref[...] * sin
    scale = (D_NOPE + D_ROPE) ** -0.5 * LOG2E
    for h in range(D_HEADS):
        cols = slice(h * LANES, (h + 1) * LANES)
        q = _dot(cqn, wq1_ref[:, cols]) * cos + _dot(cqn, wq2_ref[:, cols]) * sin
        q_ref[:, cols] = (q * scale).astype(q_ref.dtype)
        k_ref[:, cols] = (_dot(c, wk_ref[:, cols]) + k_rope).astype(k_ref.dtype)
    v_ref[...] = _dot(c, wv_ref[...]).astype(v_ref.dtype)


def _mla_prep(hf, cos, sin, gq, gkv, wq1, wq2, wk, wv, seq):
    m = hf.shape[0]
    n_seq = seq // TM
    const = lambda shape: pl.BlockSpec(shape, lambda i: (0, 0))
    hq = D_HEADS * LANES
    return pl.pallas_call(
        _mla_prep_kernel,
        grid=(m // TM,),
        in_specs=[pl.BlockSpec((TM, D_Q_LORA), lambda i: (i, 1)),
                  pl.BlockSpec((TM, D_KV_LORA), lambda i: (i, 3)),
                  pl.BlockSpec((TM, LANES), lambda i: (i, 8)),
                  pl.BlockSpec((TM, LANES), lambda i: (i, 9)),
                  pl.BlockSpec((TM, LANES), lambda i: (i % n_seq, 0)),
                  pl.BlockSpec((TM, LANES), lambda i: (i % n_seq, 0)),
                  const((1, D_Q_LORA)), const((1, D_KV_LORA)),
                  const(wq1.shape), const(wq2.shape), const(wk.shape), const(wv.shape)],
        out_specs=[pl.BlockSpec((TM, hq), lambda i: (i, 0)), pl.BlockSpec((TM, hq), lambda i: (i, 0)),
                   pl.BlockSpec((TM, D_HEADS * D_V), lambda i: (i, 0))],
        out_shape=[jax.ShapeDtypeStruct((m, hq), BF16), jax.ShapeDtypeStruct((m, hq), BF16),
                   jax.ShapeDtypeStruct((m, D_HEADS * D_V), BF16)],
        compiler_params=_cparams(("arbitrary",)),
        name="mla_prep",
    )(hf, hf, hf, hf, cos, sin, gq.reshape(1, -1), gkv.reshape(1, -1), wq1, wq2, wk, wv)


def _mla_kernel(qa_ref, qb_ref, k_ref, v_ref, oa_ref, ob_ref, vt_ref, *, n_d, n_heads):
    t = T_ATT
    pi = pl.program_id(2)

    @pl.when(pi == 0)
    def _():
        _fill_transposed(vt_ref, v_ref, D_V)

    head_cols = [slice(h * LANES, (h + 1) * LANES) for h in range(n_heads)]
    key_minus_query = (lax.broadcasted_iota(jnp.int32, (2 * t, t), 0)
                       - lax.broadcasted_iota(jnp.int32, (2 * t, t), 1))

    def score(n, q_tile, i, s, diag):
        return jnp.where(key_minus_query <= (q_tile - 2 * i) * t, s, NEG_INF) if diag else s

    out_a, out_b = _flash_balanced(pi, n_d, [qa_ref[0, :, c] for c in head_cols], [qb_ref[0, :, c] for c in head_cols],
                                   [k_ref] * n_heads, head_cols, vt_ref,
                                   [_value_rows(h, D_V) for h in range(n_heads)], score)
    oa_ref[0] = jnp.concatenate(out_a, axis=0).T.astype(oa_ref.dtype)
    ob_ref[0] = jnp.concatenate(out_b, axis=0).T.astype(ob_ref.dtype)


def _mla_attention(q3, k3, v3):
    b, s, _ = q3.shape
    t = T_ATT
    n_d = s // t
    n_p = n_d // 2
    nh = MLA_HEADS_PER_STEP
    o_a, o_b = pl.pallas_call(
        functools.partial(_mla_kernel, n_d=n_d, n_heads=nh),
        grid=(D_HEADS // nh, b, n_p),
        in_specs=[pl.BlockSpec((1, t, nh * LANES), lambda hp, bi, pi: (bi, pi, hp)),
                  pl.BlockSpec((1, t, nh * LANES), lambda hp, bi, pi: (bi, n_d - 1 - pi, hp)),
                  pl.BlockSpec((1, s, nh * LANES), lambda hp, bi, pi: (bi, 0, hp)),
                  pl.BlockSpec((1, s, nh * D_V), lambda hp, bi, pi: (bi, 0, hp))],
        out_specs=[pl.BlockSpec((1, t, nh * D_V), lambda hp, bi, pi: (bi, pi, hp)),
                   pl.BlockSpec((1, t, nh * D_V), lambda hp, bi, pi: (bi, n_p - 1 - pi, hp))],
        out_shape=[jax.ShapeDtypeStruct((b, s // 2, D_HEADS * D_V), BF16)] * 2,
        scratch_shapes=[pltpu.VMEM((nh * (D_V + ONES_ROWS), s), BF16)],
        compiler_params=_cparams(("arbitrary", "arbitrary", "arbitrary")),
        name="mla_attention",
    )(q3, q3, k3, v3)
    return _join_halves(o_a, o_b)


def _odd_out_kernel(gate_ref, oc_ref, os_ref, ow_ref, od_ref, x_ref, w_ref, g_ref, b_ref, out_ref):
    gates = 1.0 / (1.0 + jnp.exp(-gate_ref[...]))
    lo = _lane_lo(TM)
    n_c = C_HEADS * C_DIM
    y = _dot(od_ref[...], w_ref[n_c:, :])
    for blk in range(n_c // LANES):
        cols = slice(blk * LANES, (blk + 1) * LANES)
        acc = jnp.zeros((TM, LANES), F32)
        for br, o_ref in enumerate((oc_ref, os_ref, ow_ref)):
            c0 = br * C_HEADS + 2 * blk
            gexp = jnp.where(lo, gates[:, c0:c0 + 1], gates[:, c0 + 1:c0 + 2])
            acc = acc + gexp * o_ref[:, cols]
        y = y + _dot(acc.astype(BF16), w_ref[cols, :])
    out_ref[...] = _residual_ln(x_ref[...], y, g_ref[...], b_ref[...])


def _odd_out(hf, o_cmp, o_slc, o_win, o_d, x, w_out, g, beta):
    m = x.shape[0]
    n_c = C_HEADS * C_DIM
    row = lambda width: pl.BlockSpec((TM, width), lambda i: (i, 0))
    const = lambda shape: pl.BlockSpec(shape, lambda i: (0, 0))
    return pl.pallas_call(
        _odd_out_kernel,
        grid=(m // TM,),
        in_specs=[pl.BlockSpec((TM, LANES), lambda i: (i, 2)),
                  row(n_c), row(n_c), row(n_c), row(D_HEADS * D_V), row(D_MODEL),
                  const(w_out.shape), const((1, D_MODEL)), const((1, D_MODEL))],
        out_specs=row(D_MODEL),
        out_shape=jax.ShapeDtypeStruct((m, D_MODEL), F32),
        compiler_params=_cparams(("arbitrary",)),
        name="odd_out",
    )(hf, o_cmp, o_slc, o_win, o_d, x, w_out, g.reshape(1, -1), beta.reshape(1, -1))


ODD_BF16_W = 1536
ODD_F32_W = 1280


def _odd_in_columns():
    q_c = C_HEADS * C_DIM
    kv = lambda br, which, g: q_c + ((br * 2 + which) * C_KV_GROUPS + g) * C_DIM + np.arange(C_DIM)
    gate0 = q_c + 3 * 2 * C_KV_GROUPS * C_DIM
    cq0 = gate0 + 3 * C_HEADS
    ckv0 = cq0 + D_Q_LORA
    kr0 = ckv0 + D_KV_LORA
    zeros = lambda n: np.full(n, -1)
    cols = [np.arange(q_c)]
    for br in (1, 2):
        for which in (0, 1):
            for g in range(C_KV_GROUPS):
                cols += [kv(br, which, g), kv(br, which, g)]
    assert sum(len(c) for c in cols) == ODD_BF16_W
    for which in (0, 1):
        cols += [kv(0, which, 0), kv(0, which, 1)]
    cols += [gate0 + np.arange(3 * C_HEADS), zeros(LANES - 3 * C_HEADS)]
    cols += [cq0 + np.arange(D_Q_LORA), ckv0 + np.arange(D_KV_LORA)]
    half = D_ROPE // 2
    kr = kr0 + np.arange(D_ROPE)
    cols += [zeros(D_NOPE), kr, zeros(LANES - D_NOPE - D_ROPE)]
    cols += [zeros(D_NOPE), kr[half:], kr[:half], zeros(LANES - D_NOPE - D_ROPE)]
    cols = np.concatenate(cols)
    assert len(cols) == ODD_BF16_W + ODD_F32_W
    return cols


def _gather_cols(w, cols):
    return jnp.where(jnp.asarray(cols >= 0)[None, :], w[:, np.maximum(cols, 0)], 0.0)


def _mla_weight_columns():
    dq = D_NOPE + D_ROPE
    half = D_ROPE // 2
    zeros = lambda n: np.full(n, -1)
    q1, q2, k1 = [], [], []
    for h in range(D_HEADS):
        rope0 = h * dq + D_NOPE
        q1 += [h * dq + np.arange(dq), zeros(LANES - dq)]
        q2 += [zeros(D_NOPE), rope0 + half + np.arange(half), rope0 + np.arange(half), zeros(LANES - dq)]
        k1 += [h * D_NOPE + np.arange(D_NOPE), zeros(LANES - D_NOPE)]
    return np.concatenate(q1), np.concatenate(q2), np.concatenate(k1)


def _rope_tables(seq):
    half = D_ROPE // 2
    inv = ROPE_THETA ** (-jnp.arange(half, dtype=F32) / half)
    ang = jnp.arange(seq).astype(F32)[:, None] * inv
    cos, sin = jnp.cos(ang), jnp.sin(ang)
    pad = jnp.zeros((seq, LANES - D_NOPE - D_ROPE), F32)
    cos_t = jnp.concatenate([jnp.ones((seq, D_NOPE), F32), cos, cos, pad], axis=1)
    sin_t = jnp.concatenate([jnp.zeros((seq, D_NOPE), F32), -sin, sin, pad], axis=1)
    return cos_t, sin_t


def kernel(x, rel_bias, ev_w_in, ev_w_out, ev_lambda, ev_subln, od_w_in, od_w_out, od_cmp_pe, od_cmp_w1, od_cmp_w2, od_q_norm, od_kv_norm, od_w_uq, od_w_uk, od_w_uv, ffn_w_up, ffn_conv_w, ffn_conv_b, ffn_w_down, ln_g, ln_b):
    b, s, _ = x.shape
    m = b * s
    assert s % 1024 == 0 and s // B_PAIRS[-1][1] >= 2 * T_DIL and s // SLC_BLOCK <= LANES
    n_d = s // T_ATT

    tb_t = rel_bias.T.astype(F32)
    tb_log2 = tb_t[:C_HEADS] * LOG2E
    n_prev_win = -(-(WIN_SIZE - 1) // T_ATT)
    tb_causal = _bias_table(tb_log2, _toeplitz_idx(1, n_d, T_ATT, s))
    tb_win = _bias_table(tb_log2, _toeplitz_idx(n_prev_win, n_prev_win + 1, T_ATT, WIN_SIZE - 1))
    tb_cmp = _bias_table(tb_log2, _cmp_idx(s)).reshape(C_HEADS, s, s // CMP_STRIDE)
    tb_dil = [_bias_table(tb_t[A_HEADS + i * B_HEADS:A_HEADS + (i + 1) * B_HEADS], _dilated_idx(d))
              for i, (_, d) in enumerate(B_PAIRS)]
    cos_t, sin_t = _rope_tables(s)
    odd_cols = _odd_in_columns()
    q1_cols, q2_cols, k1_cols = _mla_weight_columns()

    xf = x.reshape(m, D_MODEL)
    for l in range(DEPTH):
        i = l // 2
        if l % 2 == 0:
            n_a = EVEN_IN - EVEN_DILATED
            w_in = ev_w_in[i] * jnp.where(jnp.arange(EVEN_IN) < 2 * A_HEADS * A_QK, LOG2E, 1.0)
            h_a, h_b = _inproj(xf, w_in.astype(BF16), (n_a, EVEN_DILATED), (BF16, F32), 768)
            h3 = h_a.reshape(b, s, n_a)
            hd3 = h_b.reshape(b, s, EVEN_DILATED)
            lam_init = 0.8 - 0.6 * math.exp(-0.3 * l)
            o_a = _diff_attention(h3, tb_causal, ev_lambda[i].astype(F32), ev_subln[i], lam_init)
            obs, lses = [], []
            for p_idx, (_, d) in enumerate(B_PAIRS):
                o, lse = _dilated_attention(hd3, tb_dil[p_idx], p_idx, d)
                obs.append(o)
                lses.append(lse)
            xf = _even_out(o_a.reshape(m, -1), obs, lses, xf, ev_w_out[i].astype(BF16), ln_g[l, 0], ln_b[l, 0])
        else:
            w_in = _gather_cols(od_w_in[i], odd_cols) * jnp.where(jnp.arange(len(odd_cols)) < C_HEADS * C_DIM, LOG2E, 1.0)
            hb, hf = _inproj(xf, w_in.astype(BF16), (ODD_BF16_W, ODD_F32_W), (BF16, F32), 256)
            hb3 = hb.reshape(b, s, ODD_BF16_W)
            n_ch = s // CMP_STRIDE
            chunks = (hf[:, :4 * C_DIM].reshape(b, n_ch, CMP_STRIDE, 4, C_DIM)
                      .transpose(0, 3, 1, 2, 4).reshape(b, 4, n_ch, CMP_STRIDE * C_DIM))
            pe = od_cmp_pe[i].reshape(2, 2, CMP_STRIDE * C_DIM)
            w2d = jnp.concatenate([od_cmp_w2[i], od_cmp_w2[i]], axis=-1).astype(BF16)
            kvc = _compress(chunks, pe, od_cmp_w1[i].astype(BF16), w2d)
            o_cmp, sel = _cmp_attention(hb3, kvc, tb_cmp)
            o_slc = _slc_attention(hb3, tb_causal, 4, 6, sel)
            o_win = _win_attention(hb3, tb_win, 8, 10, n_prev_win)
            q_d, k_d, v_d = _mla_prep(hf, cos_t, sin_t, od_q_norm[i], od_kv_norm[i],
                                      _gather_cols(od_w_uq[i], q1_cols).astype(BF16),
                                      _gather_cols(od_w_uq[i], q2_cols).astype(BF16),
                                      _gather_cols(od_w_uk[i], k1_cols).astype(BF16),
                                      od_w_uv[i].astype(BF16), s)
            o_d = _mla_attention(q_d.reshape(b, s, -1), k_d.reshape(b, s, -1), v_d.reshape(b, s, -1))
            xf = _odd_out(hf, o_cmp.reshape(m, -1), o_slc.reshape(m, -1), o_win.reshape(m, -1),
                          o_d.reshape(m, -1), xf, od_w_out[i].astype(BF16), ln_g[l, 0], ln_b[l, 0])
        xf = _ffn(xf, ffn_w_up[l].astype(BF16), ffn_conv_w[l], ffn_conv_b[l], ffn_w_down[l].astype(BF16),
                  ln_g[l, 1], ln_b[l, 1], s)
    return xf.reshape(b, s, D_MODEL)
```

```python
import functools
import math

import numpy as np
import jax
import jax.numpy as jnp
from jax import lax
from jax.experimental import pallas as pl
from jax.experimental.pallas import tpu as pltpu

F32 = jnp.float32
BF16 = jnp.bfloat16

D_MODEL = 1024
DEPTH = 4
LN_EPS = 1e-5
RMS_EPS = 1e-5
N_BUCKETS = 32
MAX_DISTANCE = 2048
A_HEADS, A_QK, A_V = 4, 64, 128
B_PAIRS = ((128, 1), (512, 4), (2048, 16))
B_HEADS, B_DIM = 4, 64
C_HEADS, C_KV_GROUPS, C_DIM = 8, 2, 64
CMP_LEN, CMP_STRIDE, CMP_HIDDEN = 32, 16, 256
SLC_BLOCK, SLC_TOP_N = 64, 16
SLC_SHIFT = 6
WIN_SIZE = 512
D_HEADS, D_Q_LORA, D_KV_LORA, D_NOPE, D_ROPE, D_V = 8, 384, 256, 64, 32, 64
ROPE_THETA = 10000.0
D_FF = 2816
EVEN_DILATED = len(B_PAIRS) * 3 * B_HEADS * B_DIM
EVEN_IN = 2 * A_HEADS * A_QK * 2 + A_HEADS * A_V + EVEN_DILATED
ALPHA = (2 * DEPTH) ** 0.25

LANES = 128
HALF_LANE_SHIFT = 6
VMEM_LIMIT = 56 * 1024 * 1024
T_ATT = 256
MLA_HEADS_PER_STEP = 4
T_DIL = 128
TM = 512
FF_CHUNK = 256
NEG_INF = float("-inf")
M_INIT = -1e30
SEL_PENALTY = 2.0 ** 101
LOG2E = math.log2(math.e)
ONES_ROWS = 16


def _cparams(sem, flags=None):
    return pltpu.CompilerParams(dimension_semantics=sem, vmem_limit_bytes=VMEM_LIMIT, flags=flags)


def _dot(a, b):
    return jnp.dot(a, b, preferred_element_type=F32)


def _dot_nt(a, b):
    return lax.dot_general(a, b, (((1,), (1,)), ((), ())), preferred_element_type=F32)


def _bucket_np(dist):
    n = np.maximum(dist, 0)
    nf = np.maximum(n, 1).astype(np.float32)
    max_exact = N_BUCKETS // 2
    large = max_exact + (np.log(nf / max_exact) / math.log(MAX_DISTANCE / max_exact)
                         * (N_BUCKETS - max_exact)).astype(np.int32)
    return np.where(n < max_exact, n, np.minimum(large, N_BUCKETS - 1)).astype(np.int32)


def _toeplitz_idx(n_masked, n_delta, t, max_dist):
    key = np.arange(t)[:, None]
    query = np.arange(t)[None, :]
    out = []
    for delta in range(-n_masked, n_delta):
        dist = t * delta + query - key
        ok = (dist >= 0) & (dist <= max_dist)
        out.append(np.where(ok, _bucket_np(dist), -1))
    return np.stack(out).astype(np.int32)


def _dilated_idx(d):
    t = T_DIL
    r = np.arange(t)[:, None]
    c = np.arange(2 * t)[None, :]
    dist0 = r - c
    ok0 = (dist0 >= 0) & (c < t)
    dist1 = t + r - c
    ok1 = (dist1 >= 0) & (dist1 <= t)
    return np.stack([np.where(ok0, _bucket_np(dist0 * d), -1),
                     np.where(ok1, _bucket_np(dist1 * d), -1)]).astype(np.int32)


def _cmp_idx(seq):
    ncb = seq // CMP_STRIDE - 1
    q = np.arange(seq)[:, None]
    c = np.arange(seq // CMP_STRIDE)[None, :]
    dist = q - (c * CMP_STRIDE + CMP_LEN - 1)
    ok = (dist >= 0) & (c < ncb)
    return np.where(ok, _bucket_np(dist), -1).astype(np.int32).reshape(seq // T_ATT, T_ATT, seq // CMP_STRIDE)


def _bias_kernel(tbl_ref, idx_ref, o_ref):
    idx = idx_ref[0]
    rows = idx.shape[0]
    row = jnp.broadcast_to(tbl_ref[0], (rows, LANES))
    for c0 in range(0, idx.shape[1], LANES):
        ix = idx[:, c0:c0 + LANES]
        val = jnp.take_along_axis(row, jnp.maximum(ix, 0), axis=1)
        o_ref[0, 0, :, c0:c0 + LANES] = jnp.where(ix < 0, NEG_INF, val)


def _bias_table(tbl, idx):
    n_h = tbl.shape[0]
    n, r, c = idx.shape
    tbl_rows = jnp.pad(tbl, ((0, 0), (0, LANES - N_BUCKETS))).reshape(n_h, 1, LANES)
    return pl.pallas_call(
        _bias_kernel,
        grid=(n_h, n),
        in_specs=[pl.BlockSpec((1, 1, LANES), lambda h, i: (h, 0, 0)),
                  pl.BlockSpec((1, r, c), lambda h, i: (i, 0, 0))],
        out_specs=pl.BlockSpec((1, 1, r, c), lambda h, i: (h, i, 0, 0)),
        out_shape=jax.ShapeDtypeStruct((n_h, n, r, c), F32),
        compiler_params=_cparams(("arbitrary", "arbitrary")),
        name="bias_table",
    )(tbl_rows, jnp.asarray(idx))


def _inproj_kernel(x_ref, w_ref, *o_refs, widths, chunk):
    xb = x_ref[...].astype(BF16)
    off = 0
    for o_ref, width in zip(o_refs, widths):
        for c0 in range(0, width, chunk):
            o_ref[:, c0:c0 + chunk] = _dot(xb, w_ref[:, off + c0:off + c0 + chunk]).astype(o_ref.dtype)
        off += width


def _inproj(x, w, widths, dtypes, chunk):
    m = x.shape[0]
    n = w.shape[1]
    assert sum(widths) == n and all(wd % chunk == 0 for wd in widths)
    return pl.pallas_call(
        functools.partial(_inproj_kernel, widths=widths, chunk=chunk),
        grid=(m // TM,),
        in_specs=[pl.BlockSpec((TM, D_MODEL), lambda i: (i, 0)),
                  pl.BlockSpec((D_MODEL, n), lambda i: (0, 0))],
        out_specs=[pl.BlockSpec((TM, wd), lambda i: (i, 0)) for wd in widths],
        out_shape=[jax.ShapeDtypeStruct((m, wd), dt) for wd, dt in zip(widths, dtypes)],
        compiler_params=_cparams(("arbitrary",)),
        name="in_proj",
    )(x, w)


def _tree_reduce_rows(x, combine, reduce_fn):
    parts = [x[r:r + 8] for r in range(0, x.shape[0], 8)]
    while len(parts) > 1:
        parts = [combine(parts[k], parts[k + 1]) for k in range(0, len(parts) - 1, 2)] + \
                ([parts[-1]] if len(parts) % 2 else [])
    return reduce_fn(parts[0], axis=0, keepdims=True)


def _flash_init(streams):
    return tuple((jnp.full((1, q.shape[0]), M_INIT, F32), jnp.zeros((vt_rows.stop - vt_rows.start, q.shape[0]), F32))
                 for q, _, _, _, vt_rows in streams)


def _normalise(acc):
    d_v = acc.shape[0] - ONES_ROWS
    return acc[:d_v] / acc[d_v:d_v + 1]


def _flash_scores(streams, key_start, n_keys):
    keys = pl.ds(pl.multiple_of(key_start, T_ATT), n_keys)
    return tuple(_dot_nt(k_ref[0, keys, k_cols], q) for q, k_ref, k_cols, _, _ in streams)


def _flash_update(streams, score_fns, key_start, n_keys, i, raw, state):
    keys = pl.ds(pl.multiple_of(key_start, T_ATT), n_keys)
    scores = [fn(i, s) for fn, s in zip(score_fns, raw)]
    m_new = [jnp.maximum(m, _tree_reduce_rows(s, jnp.maximum, jnp.max)) for s, (m, _) in zip(scores, state)]
    probs = [jnp.exp2(s - mn) for s, mn in zip(scores, m_new)]
    out = []
    for (_, _, _, vt_ref, vt_rows), p, mn, (m, acc) in zip(streams, probs, m_new, state):
        acc = jnp.exp2(m - mn) * acc + _dot(vt_ref[vt_rows, keys], p.astype(BF16))
        out.append((mn, acc))
    return tuple(out)


def _flash_finish(state):
    return [_normalise(acc) for _, acc in state]


def _flash_balanced(pi, n_d, qa, qb, k_refs, k_cols, vt_ref, vt_rows, score, key_extra=None):
    kt = 2 * T_ATT
    t_q = qa[0].shape[0]
    n_streams = len(qa)
    n_slots = n_d // 2 + 1
    max_a = n_d // 4
    n_a = pi // 2 + 1
    tile_a, tile_b = pi, n_d - 1 - pi

    def slot(s):
        if s == 0:
            return True, tile_a, n_a - 1
        if s >= max_a:
            return False, tile_b, s - n_a
        is_a = s < n_a
        return is_a, jnp.where(is_a, tile_a, tile_b), jnp.where(is_a, n_a - 1 - s, s - n_a)

    def pick(is_a, a, b):
        return (a if is_a else b) if isinstance(is_a, bool) else jnp.where(is_a, a, b)

    def keys_of(i):
        return pl.ds(pl.multiple_of(i * kt, kt), kt)

    def raw_scores(s):
        is_a, _, i = slot(s)
        extra = None if key_extra is None else key_extra(i)
        out = []
        for n in range(n_streams):
            keys = k_refs[n][0, keys_of(i), k_cols[n]]
            if extra is not None:
                keys = jnp.concatenate([keys, extra], axis=1)
            out.append(_dot_nt(keys, pick(is_a, qa[n], qb[n])))
        return out

    state = [(jnp.full((1, t_q), M_INIT, F32), jnp.zeros((vt_rows[n].stop - vt_rows[n].start, t_q), F32))
             for n in range(n_streams)]
    out_a = [jnp.zeros((acc.shape[0] - ONES_ROWS, t_q), F32) for _, acc in state]
    pending = None
    raw = raw_scores(0)

    def fold(state, pending):
        alphas, probs, i_prev = pending
        return [(m, alphas[n] * acc + _dot(vt_ref[vt_rows[n], keys_of(i_prev)], probs[n]))
                for n, (m, acc) in enumerate(state)]

    for s in range(n_slots):
        raw_next = raw_scores(s + 1) if s + 1 < n_slots else None
        is_a, q_tile, i = slot(s)
        if pending is not None:
            state = fold(state, pending)
        if 1 <= s <= max_a:
            switch = s == n_a
            out_a = [jnp.where(switch, _normalise(acc), o) for (_, acc), o in zip(state, out_a)]
            state = [(jnp.where(switch, M_INIT, m), jnp.where(switch, 0.0, acc)) for m, acc in state]
        diag = s == 0 or s == n_slots - 1
        scores = [score(n, q_tile, i, raw[n], diag) for n in range(n_streams)]
        m_new = [jnp.maximum(m, _tree_reduce_rows(sc, jnp.maximum, jnp.max)) for sc, (m, _) in zip(scores, state)]
        probs = [jnp.exp2(sc - mn).astype(BF16) for sc, mn in zip(scores, m_new)]
        alphas = [jnp.exp2(m - mn) for (m, _), mn in zip(state, m_new)]
        state = [(mn, acc) for mn, (_, acc) in zip(m_new, state)]
        pending = (alphas, probs, i)
        raw = raw_next
    state = fold(state, pending)
    return out_a, [_normalise(acc) for _, acc in state]


def _pair_bias(tb_ref, h, q_tile, i, s):
    t = T_ATT
    d0 = q_tile - 2 * i
    return jnp.concatenate([s[:t] + tb_ref[h, d0 + 1], s[t:] + tb_ref[h, d0]], axis=0)


def _fill_transposed(vt_ref, v_ref, d_v):
    n = v_ref.shape[1]
    n_heads = vt_ref.shape[0] // (d_v + ONES_ROWS)
    step = 2 * T_ATT
    for c0 in range(0, n, step):
        v_t = v_ref[0, c0:c0 + step, :].astype(F32).T.astype(BF16)
        for h in range(n_heads):
            r0 = h * (d_v + ONES_ROWS)
            vt_ref[r0:r0 + d_v, c0:c0 + step] = v_t[h * d_v:(h + 1) * d_v]
            vt_ref[r0 + d_v:r0 + d_v + ONES_ROWS, c0:c0 + step] = jnp.ones((ONES_ROWS, step), BF16)


def _value_rows(h, d_v):
    return slice(h * (d_v + ONES_ROWS), (h + 1) * (d_v + ONES_ROWS))


def _lane_lo(rows):
    return lax.broadcasted_iota(jnp.int32, (rows, LANES), 1) < (LANES // 2)


def _diff_kernel(lam_ref, g_ref, q0a_ref, q0b_ref, q1a_ref, q1b_ref, k0_ref, k1_ref, v_ref, tb_ref,
                 oa_ref, ob_ref, vt_ref, *, lam_init, n_d):
    pi = pl.program_id(2)

    @pl.when(pi == 0)
    def _():
        _fill_transposed(vt_ref, v_ref, A_V)

    lp = lam_ref[...]
    lam = (jnp.exp(jnp.sum(lp[0:1] * lp[1:2], axis=-1, keepdims=True))
           - jnp.exp(jnp.sum(lp[2:3] * lp[3:4], axis=-1, keepdims=True)) + lam_init)
    lo = _lane_lo(T_ATT)

    def prep(q_ref):
        q = q_ref[0] * (A_QK ** -0.5)
        zero = jnp.zeros_like(q)
        return [jnp.where(lo, q, zero), jnp.where(lo, zero, q)]

    full = slice(0, LANES)

    def score(n, q_tile, i, s, diag):
        return _pair_bias(tb_ref, n % 2, q_tile, i, s)

    out_a, out_b = _flash_balanced(pi, n_d, prep(q0a_ref) + prep(q1a_ref), prep(q0b_ref) + prep(q1b_ref),
                                   [k0_ref, k0_ref, k1_ref, k1_ref], [full] * 4, vt_ref,
                                   [_value_rows(0, A_V), _value_rows(1, A_V)] * 2, score)
    for outs, o_ref in ((out_a, oa_ref), (out_b, ob_ref)):
        for head in range(2):
            d = (outs[head] - lam * outs[2 + head]).T
            ms = jnp.mean(d * d, axis=-1, keepdims=True)
            o_ref[0, :, head * A_V:(head + 1) * A_V] = (d * lax.rsqrt(ms + RMS_EPS) * g_ref[...]
                                                        * (1.0 - lam_init)).astype(o_ref.dtype)


def _join_halves(o_a, o_b):
    return jnp.concatenate([o_a, o_b], axis=1)


def _diff_attention(h3, tb, lam_p, subln_g, lam_init):
    b, s, _ = h3.shape
    t = T_ATT
    n_d = s // t
    n_p = n_d // 2
    q_spec = lambda col0, second: pl.BlockSpec(
        (1, t, LANES), lambda hp, bi, pi: (bi, (n_d - 1 - pi) if second else pi, col0 + hp))
    k_spec = lambda col0: pl.BlockSpec((1, s, LANES), lambda hp, bi, pi: (bi, 0, col0 + hp))
    o_a, o_b = pl.pallas_call(
        functools.partial(_diff_kernel, lam_init=lam_init, n_d=n_d),
        grid=(A_HEADS // 2, b, n_p),
        in_specs=[pl.BlockSpec((4, A_QK), lambda hp, bi, pi: (0, 0)),
                  pl.BlockSpec((1, A_V), lambda hp, bi, pi: (0, 0)),
                  q_spec(0, False), q_spec(0, True), q_spec(2, False), q_spec(2, True),
                  k_spec(4), k_spec(6),
                  pl.BlockSpec((1, s, 2 * A_V), lambda hp, bi, pi: (bi, 0, 4 + hp)),
                  pl.BlockSpec((2, n_d + 1, t, t), lambda hp, bi, pi: (hp, 0, 0, 0))],
        out_specs=[pl.BlockSpec((1, t, 2 * A_V), lambda hp, bi, pi: (bi, pi, hp)),
                   pl.BlockSpec((1, t, 2 * A_V), lambda hp, bi, pi: (bi, n_p - 1 - pi, hp))],
        out_shape=[jax.ShapeDtypeStruct((b, s // 2, A_HEADS * A_V), BF16)] * 2,
        scratch_shapes=[pltpu.VMEM((2 * (A_V + ONES_ROWS), s), BF16)],
        compiler_params=_cparams(("arbitrary", "arbitrary", "arbitrary")),
        name="diff_attention",
    )(lam_p, subln_g.reshape(1, A_V), h3, h3, h3, h3, h3, h3, h3, tb)
    return _join_halves(o_a, o_b)


def _dilated_kernel(q_ref, k_ref, v_ref, tb_ref, o_ref, lse_ref, *, d, residues):
    t = T_DIL
    n_blocks = q_ref.shape[1] // (d * t)
    lo = _lane_lo(t)
    scale = B_DIM ** -0.5

    def block(idx, carry):
        r = pl.program_id(2) * residues + idx // n_blocks
        bi = idx % n_blocks
        var = jnp.minimum(bi, 1)
        qrows = pl.ds(r + d * t * bi, t, stride=d)
        krows = pl.ds(r + d * t * jnp.maximum(bi - 1, 0), 2 * t, stride=d)
        q = (q_ref[0, qrows, :] * scale).astype(BF16)
        kk = k_ref[0, krows, :].astype(BF16)
        vv = v_ref[0, krows, :].astype(BF16)
        o_half, l_half = [], []
        for half in range(2):
            keep = lo if half == 0 else jnp.logical_not(lo)
            qm = jnp.where(keep, q, jnp.zeros_like(q))
            s = _dot_nt(qm, kk) + tb_ref[half, var]
            m = jnp.max(s, axis=-1, keepdims=True)
            p = jnp.exp(s - m)
            l = jnp.sum(p, axis=-1, keepdims=True)
            o_half.append(_dot(p.astype(BF16), vv) / l)
            l_half.append(m + jnp.log(l))
        o_ref[0, qrows, :] = jnp.where(lo, o_half[0], o_half[1])
        lse_ref[0, qrows, :] = jnp.where(lo, l_half[0], l_half[1])
        return carry

    lax.fori_loop(0, residues * n_blocks, block, 0, unroll=4)


def _dilated_attention(hd3, tb, pair_idx, d):
    b, s, _ = hd3.shape
    hd = B_HEADS * B_DIM
    base = pair_idx * 3 * (hd // LANES)
    seq = lambda which: pl.BlockSpec((1, s, LANES), lambda bi, hp, r: (bi, 0, base + which * (hd // LANES) + hp))
    out = pl.BlockSpec((1, s, LANES), lambda bi, hp, r: (bi, 0, hp))
    n_blocks = s // (d * T_DIL)
    residues = min(d, max(1, 8 // n_blocks))
    assert d % residues == 0 and (residues * n_blocks) % 4 == 0
    o, lse = pl.pallas_call(
        functools.partial(_dilated_kernel, d=d, residues=residues),
        grid=(b, hd // LANES, d // residues),
        in_specs=[seq(0), seq(1), seq(2),
                  pl.BlockSpec((2, 2, T_DIL, 2 * T_DIL), lambda bi, hp, r: (hp, 0, 0, 0))],
        out_specs=[out, out],
        out_shape=[jax.ShapeDtypeStruct((b, s, hd), F32)] * 2,
        compiler_params=_cparams(("arbitrary", "arbitrary", "arbitrary")),
        name="dilated_attention",
    )(hd3, hd3, hd3, tb)
    return o.reshape(b * s, hd), lse.reshape(b * s, hd)


def _residual_ln(x, y, g, beta):
    z = ALPHA * x + y
    mu = jnp.mean(z, axis=-1, keepdims=True)
    zc = z - mu
    var = jnp.mean(zc * zc, axis=-1, keepdims=True)
    return zc * lax.rsqrt(var + LN_EPS) * g + beta


def _even_out_kernel(oa_ref, o0_ref, o1_ref, o2_ref, l0_ref, l1_ref, l2_ref, x_ref, w_ref, g_ref, b_ref, out_ref):
    l0, l1, l2 = l0_ref[...], l1_ref[...], l2_ref[...]
    mx = jnp.maximum(jnp.maximum(l0, l1), l2)
    e0, e1, e2 = jnp.exp(l0 - mx), jnp.exp(l1 - mx), jnp.exp(l2 - mx)
    den = e0 + e1 + e2
    ob = (e0 / den) * o0_ref[...] + (e1 / den) * o1_ref[...] + (e2 / den) * o2_ref[...]
    n_a = A_HEADS * A_V
    y = _dot(oa_ref[...], w_ref[0:n_a, :]) + _dot(ob.astype(BF16), w_ref[n_a:, :])
    out_ref[...] = _residual_ln(x_ref[...], y, g_ref[...], b_ref[...])


def _even_out(oa, obs, lses, x, w_out, g, beta):
    m = x.shape[0]
    hd = B_HEADS * B_DIM
    row = lambda width: pl.BlockSpec((TM, width), lambda i: (i, 0))
    const = lambda shape: pl.BlockSpec(shape, lambda i: (0, 0))
    return pl.pallas_call(
        _even_out_kernel,
        grid=(m // TM,),
        in_specs=[row(A_HEADS * A_V)] + [row(hd)] * 6 + [row(D_MODEL), const(w_out.shape),
                                                        const((1, D_MODEL)), const((1, D_MODEL))],
        out_specs=row(D_MODEL),
        out_shape=jax.ShapeDtypeStruct((m, D_MODEL), F32),
        compiler_params=_cparams(("arbitrary",)),
        name="even_out",
    )(oa, *obs, *lses, x, w_out, g.reshape(1, -1), beta.reshape(1, -1))


def _gelu(x):
    return 0.5 * x * (1.0 + jnp.tanh(math.sqrt(2.0 / math.pi) * (x + 0.044715 * (x * x * x))))


def _ffn_kernel(x_ref, wu_ref, cw_ref, cb_ref, wd_ref, g_ref, b_ref, out_ref, u_ref, gs_ref, tail_ref,
                *, tiles_per_seq):
    halo = 8
    n_chunks = D_FF // FF_CHUNK

    @pl.when(pl.program_id(0) % tiles_per_seq == 0)
    def _():
        tail_ref[...] = jnp.zeros(tail_ref.shape, F32)

    x = x_ref[...]
    xb = x.astype(BF16)

    def up(c):
        cols = slice(c * FF_CHUNK, (c + 1) * FF_CHUNK)
        return _dot(xb, wu_ref[:, cols]), _dot(xb, wu_ref[:, D_FF + c * FF_CHUNK:D_FF + (c + 1) * FF_CHUNK])

    def activate(c, a, gate):
        cols = slice(c * FF_CHUNK, (c + 1) * FF_CHUNK)
        gs = gs_ref.at[c % 2]
        gs[0:halo, :] = tail_ref[:, cols]
        gs[halo:, :] = gate
        tail_ref[:, cols] = gate[TM - halo:, :]
        conv = (gs[pl.ds(halo - 2, TM), :] * cw_ref[0:1, cols] + gs[pl.ds(halo - 1, TM), :] * cw_ref[1:2, cols]
                + gate * cw_ref[2:3, cols] + cb_ref[:, cols])
        u_ref[:, cols] = (_gelu(conv) * a).astype(BF16)

    nxt = up(0)
    for c in range(n_chunks):
        cur = nxt
        if c + 1 < n_chunks:
            nxt = up(c + 1)
        activate(c, *cur)
    out_ref[...] = _residual_ln(x, _dot(u_ref[...], wd_ref[...]), g_ref[...], b_ref[...])


def _ffn(x, w_up, conv_w, conv_b, w_down, g, beta, seq):
    m = x.shape[0]
    const = lambda shape: pl.BlockSpec(shape, lambda i: (0, 0))
    return pl.pallas_call(
        functools.partial(_ffn_kernel, tiles_per_seq=seq // TM),
        grid=(m // TM,),
        in_specs=[pl.BlockSpec((TM, D_MODEL), lambda i: (i, 0)), const(w_up.shape), const(conv_w.shape),
                  const((1, D_FF)), const(w_down.shape), const((1, D_MODEL)), const((1, D_MODEL))],
        out_specs=pl.BlockSpec((TM, D_MODEL), lambda i: (i, 0)),
        out_shape=jax.ShapeDtypeStruct((m, D_MODEL), F32),
        scratch_shapes=[pltpu.VMEM((TM, D_FF), BF16), pltpu.VMEM((2, TM + 8, FF_CHUNK), F32),
                        pltpu.VMEM((8, D_FF), F32)],
        compiler_params=_cparams(("arbitrary",)),
        name="conv_ffn",
    )(x, w_up, conv_w, conv_b.reshape(1, -1), w_down, g.reshape(1, -1), beta.reshape(1, -1))


def _compress_kernel(ch_ref, pe_ref, w1_ref, w2_ref, o_ref):
    half = CMP_STRIDE * C_DIM
    ch = ch_ref[0, 0]
    a = _dot((ch + pe_ref[0, 0:1, :]).astype(BF16), w1_ref[0, 0:half, :])
    b = _dot((ch + pe_ref[0, 1:2, :]).astype(BF16), w1_ref[0, half:, :])
    n = ch.shape[0]
    hid = _gelu(a + pltpu.roll(b, n - 1, 0))
    o_ref[0, 0] = _dot(hid.astype(BF16), w2_ref[0]).astype(o_ref.dtype)


def _compress(chunks, pe, w1, w2d):
    b, _, n, width = chunks.shape
    return pl.pallas_call(
        _compress_kernel,
        grid=(b, 4),
        in_specs=[pl.BlockSpec((1, 1, n, width), lambda bi, j: (bi, j, 0, 0)),
                  pl.BlockSpec((1, 2, width), lambda bi, j: (j // 2, 0, 0)),
                  pl.BlockSpec((1, 2 * width, CMP_HIDDEN), lambda bi, j: (j // 2, 0, 0)),
                  pl.BlockSpec((1, CMP_HIDDEN, LANES), lambda bi, j: (j // 2, 0, 0))],
        out_specs=pl.BlockSpec((1, 1, n, LANES), lambda bi, j: (bi, j, 0, 0)),
        out_shape=jax.ShapeDtypeStruct((b, 4, n, LANES), BF16),
        compiler_params=_cparams(("arbitrary", "arbitrary")),
        name="nsa_compress",
    )(chunks, pe, w1, w2d)


def _cmp_kernel(q_ref, kc_ref, vc_ref, tb_ref, o_ref, sel_ref, pg_ref, sc_ref, pen_ref, *, tiles):
    t = T_ATT
    n_c = kc_ref.shape[2]
    n_sel = n_c // (SLC_BLOCK // CMP_STRIDE)
    lo = _lane_lo(t)
    kc = kc_ref[0, 0]
    vc = vc_ref[0, 0]
    raw = []
    for u in range(tiles):
        for pair in range(2):
            q = q_ref[0, u * t:(u + 1) * t, pair * LANES:(pair + 1) * LANES] * (C_DIM ** -0.5)
            zero = jnp.zeros_like(q)
            raw += [_dot_nt(jnp.where(lo, q, zero), kc), _dot_nt(jnp.where(lo, zero, q), kc)]
    probs = []
    for n, s in enumerate(raw):
        u, head = divmod(n, 4)
        s = s + tb_ref[head, u * t:(u + 1) * t, :]
        m = jnp.maximum(jnp.max(s, axis=-1, keepdims=True), M_INIT)
        p = jnp.exp2(s - m)
        den = jnp.sum(p, axis=-1, keepdims=True)
        probs.append(p / jnp.where(den > 0, den, 1.0))
    outs = [_dot(p.astype(BF16), vc) for p in probs]
    for u in range(tiles):
        for pair in range(2):
            o_ref[0, u * t:(u + 1) * t, pair * LANES:(pair + 1) * LANES] = jnp.where(
                lo, outs[4 * u + 2 * pair], outs[4 * u + 2 * pair + 1])
    for u in range(tiles):
        _select_blocks((probs[4 * u] + probs[4 * u + 1]) + (probs[4 * u + 2] + probs[4 * u + 3]),
                       pl.program_id(1) * tiles + u, sel_ref.at[0, 0, u * t:(u + 1) * t, :],
                       pg_ref.at[u], sc_ref.at[u], pen_ref.at[u], n_c, n_sel)


def _select_blocks(pg, qi, sel_out, pg_ref, sc_ref, pen_ref, n_c, n_sel):
    t = T_ATT
    pad = 8
    pg_t = pg.T
    for c in range(t // LANES):
        pg_ref[c, 0:pad, :] = jnp.zeros((pad, LANES), F32)
        pg_ref[c, pad:pad + n_c, :] = pg_t[:, c * LANES:(c + 1) * LANES]
        pg_ref[c, pad + n_c:, :] = jnp.zeros((pad, LANES), F32)
    r = SLC_BLOCK // CMP_STRIDE
    tap = lambda k: jnp.concatenate(
        [pg_ref[c, pl.ds(pad + k, n_sel, stride=r), :] for c in range(t // LANES)], axis=1)
    score = (0.5 * tap(-1) + ((tap(0) + tap(1)) + tap(2))) + 0.5 * tap(3)
    jb = lax.broadcasted_iota(jnp.int32, (n_sel, t), 0)
    qblk = jnp.right_shift(qi * t + lax.broadcasted_iota(jnp.int32, (n_sel, t), 1), SLC_SHIFT)
    forced = (jb == 0) | (jb == qblk) | (jb == qblk - 1)
    sc_ref[...] = jnp.where(forced, jnp.inf, jnp.where(jb <= qblk, score, NEG_INF))
    pen_ref[...] = jnp.zeros(pen_ref.shape, F32)
    groups = range(0, n_sel, 8)
    sc_g = [sc_ref[g0:g0 + 8, :] for g0 in groups]
    sub = lax.broadcasted_iota(jnp.int32, (8, t), 0)
    cnt_g = [jnp.zeros((8, t), jnp.int32) for _ in groups]
    for i in range(n_sel):
        row = jnp.broadcast_to(sc_ref[pl.ds(i, 1), :], (8, t))
        for k, g0 in enumerate(groups):
            if i < g0:
                beats = row >= sc_g[k]
            elif i >= g0 + 8:
                beats = row > sc_g[k]
            else:
                beats = (row > sc_g[k]) | ((row == sc_g[k]) & (i - g0 < sub))
            cnt_g[k] = cnt_g[k] + beats.astype(jnp.int32)
    for k, g0 in enumerate(groups):
        pen_ref[g0:g0 + 8, :] = jnp.where(cnt_g[k] < SLC_TOP_N, 0.0, -SEL_PENALTY)
    sel_out[...] = pen_ref[...].T.astype(sel_out.dtype)


def _cmp_attention(hb3, kvc, tb):
    b, s, _ = hb3.shape
    t = T_ATT
    n_c = s // CMP_STRIDE
    n_sel = s // SLC_BLOCK
    assert n_sel <= LANES
    gw = (C_HEADS // C_KV_GROUPS) * C_DIM
    tiles = 2
    rows = tiles * t
    return pl.pallas_call(
        functools.partial(_cmp_kernel, tiles=tiles),
        grid=(C_KV_GROUPS, s // rows, b),
        in_specs=[pl.BlockSpec((1, rows, gw), lambda g, qi, bi: (bi, qi, g)),
                  pl.BlockSpec((1, 1, n_c, LANES), lambda g, qi, bi: (bi, g, 0, 0)),
                  pl.BlockSpec((1, 1, n_c, LANES), lambda g, qi, bi: (bi, 2 + g, 0, 0)),
                  pl.BlockSpec((4, rows, n_c), lambda g, qi, bi: (g, qi, 0))],
        out_specs=[pl.BlockSpec((1, rows, gw), lambda g, qi, bi: (bi, qi, g)),
                   pl.BlockSpec((1, 1, rows, LANES), lambda g, qi, bi: (bi, g, qi, 0))],
        out_shape=[jax.ShapeDtypeStruct((b, s, C_HEADS * C_DIM), F32),
                   jax.ShapeDtypeStruct((b, C_KV_GROUPS, s, LANES), BF16)],
        scratch_shapes=[pltpu.VMEM((tiles, t // LANES, n_c + 16, LANES), F32), pltpu.VMEM((tiles, n_sel, t), F32),
                        pltpu.VMEM((tiles, LANES, t), F32)],
        compiler_params=_cparams(("arbitrary", "arbitrary", "arbitrary")),
        name="nsa_cmp_attention",
    )(hb3, kvc, kvc, tb)


def _head_pair_queries(q_ref):
    lo = _lane_lo(T_ATT)
    q = q_ref[0] * (C_DIM ** -0.5)
    zero = jnp.zeros_like(q)
    return [jnp.where(lo, q, zero), jnp.where(lo, zero, q)]


def _slc_kernel(qa_ref, qb_ref, k_ref, v_ref, tb_ref, sela_ref, selb_ref, oa_ref, ob_ref, vt_ref, *, n_d):
    t = T_ATT
    pi = pl.program_id(2)

    @pl.when(pi == 0)
    def _():
        _fill_transposed(vt_ref, v_ref, C_DIM)

    key_blk =jnp.right_shift(lax.broadcasted_iota(jnp.int32, (2 * t, LANES), 0), SLC_SHIFT)
    blk_slot = lax.broadcasted_iota(jnp.int32, (2 * t, LANES), 1)

    def one_hot_block(i):
        return jnp.where(blk_slot == key_blk + i * (2 * t // SLC_BLOCK), 1.0, 0.0).astype(BF16)

    def with_penalty(q_ref, sel_ref):
        return [jnp.concatenate([q, sel_ref[0, 0]], axis=1) for q in _head_pair_queries(q_ref)]

    def score(n, q_tile, i, s, diag):
        return _pair_bias(tb_ref, n, q_tile, i, s)

    full = slice(0, LANES)
    out_a, out_b = _flash_balanced(pi, n_d, with_penalty(qa_ref, sela_ref), with_penalty(qb_ref, selb_ref),
                                   [k_ref, k_ref], [full, full], vt_ref, [_value_rows(0, C_DIM)] * 2, score,
                                   key_extra=one_hot_block)
    oa_ref[0] = jnp.concatenate(out_a, axis=0).T
    ob_ref[0] = jnp.concatenate(out_b, axis=0).T


def _win_kernel(q_ref, k_ref, v_ref, tb_ref, o_ref, vt_ref, *, n_prev, tiles_per_step):
    t = T_ATT
    step = pl.program_id(2)

    @pl.when(step == 0)
    def _():
        _fill_transposed(vt_ref, v_ref, C_DIM)

    full = slice(0, LANES)
    n_keys = (n_prev + 1) * t
    lo = _lane_lo(t)
    work = []
    for u_tile in range(tiles_per_step):
        qi = step * tiles_per_step + u_tile
        q = q_ref[0, u_tile * t:(u_tile + 1) * t, :] * (C_DIM ** -0.5)
        zero = jnp.zeros_like(q)
        streams = [(qh, k_ref, full, vt_ref, _value_rows(0, C_DIM))
                   for qh in (jnp.where(lo, q, zero), jnp.where(lo, zero, q))]
        j0 = jnp.maximum(qi - n_prev, 0)
        work.append((qi, j0, streams, _flash_scores(streams, j0 * t, n_keys)))
    for u_tile, (qi, j0, streams, raw) in enumerate(work):
        def score_fn(h, qi=qi, j0=j0):
            def fn(_, s):
                return jnp.concatenate([s[u * t:(u + 1) * t] + tb_ref[h, qi - j0 - u + n_prev]
                                        for u in range(n_prev + 1)], axis=0)
            return fn

        state = _flash_update(streams, [score_fn(0), score_fn(1)], j0 * t, n_keys, 0, raw, _flash_init(streams))
        o_ref[0, u_tile * t:(u_tile + 1) * t, :] = jnp.concatenate(_flash_finish(state), axis=0).T


def _slc_attention(hb3, tb, k_blk, v_blk, sel):
    b, s, _ = hb3.shape
    t = T_ATT
    n_d = s // t
    n_p = n_d // 2
    o_a, o_b = pl.pallas_call(
        functools.partial(_slc_kernel, n_d=n_d),
        grid=(C_HEADS // 2, b, n_p),
        in_specs=[pl.BlockSpec((1, t, LANES), lambda hp, bi, pi: (bi, pi, hp)),
                  pl.BlockSpec((1, t, LANES), lambda hp, bi, pi: (bi, n_d - 1 - pi, hp)),
                  pl.BlockSpec((1, s, LANES), lambda hp, bi, pi: (bi, 0, k_blk + hp // 2)),
                  pl.BlockSpec((1, s, LANES), lambda hp, bi, pi: (bi, 0, v_blk + hp // 2)),
                  pl.BlockSpec((2, n_d + 1, t, t), lambda hp, bi, pi: (hp, 0, 0, 0)),
                  pl.BlockSpec((1, 1, t, LANES), lambda hp, bi, pi: (bi, hp // 2, pi, 0)),
                  pl.BlockSpec((1, 1, t, LANES), lambda hp, bi, pi: (bi, hp // 2, n_d - 1 - pi, 0))],
        out_specs=[pl.BlockSpec((1, t, LANES), lambda hp, bi, pi: (bi, pi, hp)),
                   pl.BlockSpec((1, t, LANES), lambda hp, bi, pi: (bi, n_p - 1 - pi, hp))],
        out_shape=[jax.ShapeDtypeStruct((b, s // 2, C_HEADS * C_DIM), F32)] * 2,
        scratch_shapes=[pltpu.VMEM((C_DIM + ONES_ROWS, s), BF16)],
        compiler_params=_cparams(("arbitrary", "arbitrary", "arbitrary")),
        name="nsa_selected",
    )(hb3, hb3, hb3, hb3, tb, sel, sel)
    return _join_halves(o_a, o_b)


def _win_attention(hb3, tb, k_blk, v_blk, n_prev):
    b, s, _ = hb3.shape
    t = T_ATT
    assert s >= (n_prev + 1) * t
    tiles = 8
    return pl.pallas_call(
        functools.partial(_win_kernel, n_prev=n_prev, tiles_per_step=tiles),
        grid=(C_HEADS // 2, b, s // (tiles * t)),
        in_specs=[pl.BlockSpec((1, tiles * t, LANES), lambda hp, bi, qi: (bi, qi, hp)),
                  pl.BlockSpec((1, s, LANES), lambda hp, bi, qi: (bi, 0, k_blk + hp // 2)),
                  pl.BlockSpec((1, s, LANES), lambda hp, bi, qi: (bi, 0, v_blk + hp // 2)),
                  pl.BlockSpec((2, 2 * n_prev + 1, t, t), lambda hp, bi, qi: (hp, 0, 0, 0))],
        out_specs=pl.BlockSpec((1, tiles * t, LANES), lambda hp, bi, qi: (bi, qi, hp)),
        out_shape=jax.ShapeDtypeStruct((b, s, C_HEADS * C_DIM), F32),
        scratch_shapes=[pltpu.VMEM((C_DIM + ONES_ROWS, s), BF16)],
        compiler_params=_cparams(("arbitrary", "arbitrary", "arbitrary")),
        name="nsa_window",
    )(hb3, hb3, hb3, tb)


def _rms(x, g):
    return x * lax.rsqrt(jnp.mean(x * x, axis=-1, keepdims=True) + RMS_EPS) * g


def _mla_prep_kernel(cq_ref, ckv_ref, kr1_ref, kr2_ref, cos_ref, sin_ref, gq_ref, gkv_ref,
                     wq1_ref, wq2_ref, wk_ref, wv_ref, q_ref, k_ref, v_ref):
    cqn = _rms(cq_ref[...], gq_ref[...]).astype(BF16)
    c = _rms(ckv_ref[...], gkv_ref[...]).astype(BF16)
    cos, sin = cos_ref[...], sin_ref[...]
    k_rope = kr1_ref[...] * cos + kr2_ref[...] * sin
    scale = (D_NOPE + D_ROPE) ** -0.5 * LOG2E
    for h in range(D_HEADS):
        cols = slice(h * LANES, (h + 1) * LANES)
        q = _dot(cqn, wq1_ref[:, cols]) * cos + _dot(cqn, wq2_ref[:, cols]) * sin
        q_ref[:, cols] = (q * scale).astype(q_ref.dtype)
        k_ref[:, cols] = (_dot(c, wk_ref[:, cols]) + k_rope).astype(k_ref.dtype)
    v_ref[...] = _dot(c, wv_ref[...]).astype(v_ref.dtype)


def _mla_prep(hf, cos, sin, gq, gkv, wq1, wq2, wk, wv, seq):
    m = hf.shape[0]
    n_seq = seq // TM
    const = lambda shape: pl.BlockSpec(shape, lambda i: (0, 0))
    hq = D_HEADS * LANES
    return pl.pallas_call(
        _mla_prep_kernel,
        grid=(m // TM,),
        in_specs=[pl.BlockSpec((TM, D_Q_LORA), lambda i: (i, 1)),
                  pl.BlockSpec((TM, D_KV_LORA), lambda i: (i, 3)),
                  pl.BlockSpec((TM, LANES), lambda i: (i, 8)),
                  pl.BlockSpec((TM, LANES), lambda i: (i, 9)),
                  pl.BlockSpec((TM, LANES), lambda i: (i % n_seq, 0)),
                  pl.BlockSpec((TM, LANES), lambda i: (i % n_seq, 0)),
                  const((1, D_Q_LORA)), const((1, D_KV_LORA)),
                  const(wq1.shape), const(wq2.shape), const(wk.shape), const(wv.shape)],
        out_specs=[pl.BlockSpec((TM, hq), lambda i: (i, 0)), pl.BlockSpec((TM, hq), lambda i: (i, 0)),
                   pl.BlockSpec((TM, D_HEADS * D_V), lambda i: (i, 0))],
        out_shape=[jax.ShapeDtypeStruct((m, hq), BF16), jax.ShapeDtypeStruct((m, hq), BF16),
                   jax.ShapeDtypeStruct((m, D_HEADS * D_V), BF16)],
        compiler_params=_cparams(("arbitrary",)),
        name="mla_prep",
    )(hf, hf, hf, hf, cos, sin, gq.reshape(1, -1), gkv.reshape(1, -1), wq1, wq2, wk, wv)


def _mla_kernel(qa_ref, qb_ref, k_ref, v_ref, oa_ref, ob_ref, vt_ref, *, n_d, n_heads):
    t = T_ATT
    pi = pl.program_id(2)

    @pl.when(pi == 0)
    def _():
        _fill_transposed(vt_ref, v_ref, D_V)

    head_cols = [slice(h * LANES, (h + 1) * LANES) for h in range(n_heads)]
    key_minus_query = (lax.broadcasted_iota(jnp.int32, (2 * t, t), 0)
                       - lax.broadcasted_iota(jnp.int32, (2 * t, t), 1))

    def score(n, q_tile, i, s, diag):
        return jnp.where(key_minus_query <= (q_tile - 2 * i) * t, s, NEG_INF) if diag else s

    out_a, out_b = _flash_balanced(pi, n_d, [qa_ref[0, :, c] for c in head_cols], [qb_ref[0, :, c] for c in head_cols],
                                   [k_ref] * n_heads, head_cols, vt_ref,
                                   [_value_rows(h, D_V) for h in range(n_heads)], score)
    oa_ref[0] = jnp.concatenate(out_a, axis=0).T.astype(oa_ref.dtype)
    ob_ref[0] = jnp.concatenate(out_b, axis=0).T.astype(ob_ref.dtype)


def _mla_attention(q3, k3, v3):
    b, s, _ = q3.shape
    t = T_ATT
    n_d = s // t
    n_p = n_d // 2
    nh = MLA_HEADS_PER_STEP
    o_a, o_b = pl.pallas_call(
        functools.partial(_mla_kernel, n_d=n_d, n_heads=nh),
        grid=(D_HEADS // nh, b, n_p),
        in_specs=[pl.BlockSpec((1, t, nh * LANES), lambda hp, bi, pi: (bi, pi, hp)),
                  pl.BlockSpec((1, t, nh * LANES), lambda hp, bi, pi: (bi, n_d - 1 - pi, hp)),
                  pl.BlockSpec((1, s, nh * LANES), lambda hp, bi, pi: (bi, 0, hp)),
                  pl.BlockSpec((1, s, nh * D_V), lambda hp, bi, pi: (bi, 0, hp))],
        out_specs=[pl.BlockSpec((1, t, nh * D_V), lambda hp, bi, pi: (bi, pi, hp)),
                   pl.BlockSpec((1, t, nh * D_V), lambda hp, bi, pi: (bi, n_p - 1 - pi, hp))],
        out_shape=[jax.ShapeDtypeStruct((b, s // 2, D_HEADS * D_V), BF16)] * 2,
        scratch_shapes=[pltpu.VMEM((nh * (D_V + ONES_ROWS), s), BF16)],
        compiler_params=_cparams(("arbitrary", "arbitrary", "arbitrary")),
        name="mla_attention",
    )(q3, q3, k3, v3)
    return _join_halves(o_a, o_b)


def _odd_out_kernel(gate_ref, oc_ref, os_ref, ow_ref, od_ref, x_ref, w_ref, g_ref, b_ref, out_ref):
    gates = 1.0 / (1.0 + jnp.exp(-gate_ref[...]))
    lo = _lane_lo(TM)
    n_c = C_HEADS * C_DIM
    y = _dot(od_ref[...], w_ref[n_c:, :])
    for blk in range(n_c // LANES):
        cols = slice(blk * LANES, (blk + 1) * LANES)
        acc = jnp.zeros((TM, LANES), F32)
        for br, o_ref in enumerate((oc_ref, os_ref, ow_ref)):
            c0 = br * C_HEADS + 2 * blk
            gexp = jnp.where(lo, gates[:, c0:c0 + 1], gates[:, c0 + 1:c0 + 2])
            acc = acc + gexp * o_ref[:, cols]
        y = y + _dot(acc.astype(BF16), w_ref[cols, :])
    out_ref[...] = _residual_ln(x_ref[...], y, g_ref[...], b_ref[...])


def _odd_out(hf, o_cmp, o_slc, o_win, o_d, x, w_out, g, beta):
    m = x.shape[0]
    n_c = C_HEADS * C_DIM
    row = lambda width: pl.BlockSpec((TM, width), lambda i: (i, 0))
    const = lambda shape: pl.BlockSpec(shape, lambda i: (0, 0))
    return pl.pallas_call(
        _odd_out_kernel,
        grid=(m // TM,),
        in_specs=[pl.BlockSpec((TM, LANES), lambda i: (i, 2)),
                  row(n_c), row(n_c), row(n_c), row(D_HEADS * D_V), row(D_MODEL),
                  const(w_out.shape), const((1, D_MODEL)), const((1, D_MODEL))],
        out_specs=row(D_MODEL),
        out_shape=jax.ShapeDtypeStruct((m, D_MODEL), F32),
        compiler_params=_cparams(("arbitrary",)),
        name="odd_out",
    )(hf, o_cmp, o_slc, o_win, o_d, x, w_out, g.reshape(1, -1), beta.reshape(1, -1))


ODD_BF16_W = 1536
ODD_F32_W = 1280


def _odd_in_columns():
    q_c = C_HEADS * C_DIM
    kv = lambda br, which, g: q_c + ((br * 2 + which) * C_KV_GROUPS + g) * C_DIM + np.arange(C_DIM)
    gate0 = q_c + 3 * 2 * C_KV_GROUPS * C_DIM
    cq0 = gate0 + 3 * C_HEADS
    ckv0 = cq0 + D_Q_LORA
    kr0 = ckv0 + D_KV_LORA
    zeros = lambda n: np.full(n, -1)
    cols = [np.arange(q_c)]
    for br in (1, 2):
        for which in (0, 1):
            for g in range(C_KV_GROUPS):
                cols += [kv(br, which, g), kv(br, which, g)]
    assert sum(len(c) for c in cols) == ODD_BF16_W
    for which in (0, 1):
        cols += [kv(0, which, 0), kv(0, which, 1)]
    cols += [gate0 + np.arange(3 * C_HEADS), zeros(LANES - 3 * C_HEADS)]
    cols += [cq0 + np.arange(D_Q_LORA), ckv0 + np.arange(D_KV_LORA)]
    half = D_ROPE // 2
    kr = kr0 + np.arange(D_ROPE)
    cols += [zeros(D_NOPE), kr, zeros(LANES - D_NOPE - D_ROPE)]
    cols += [zeros(D_NOPE), kr[half:], kr[:half], zeros(LANES - D_NOPE - D_ROPE)]
    cols = np.concatenate(cols)
    assert len(cols) == ODD_BF16_W + ODD_F32_W
    return cols


def _gather_cols(w, cols):
    return jnp.where(jnp.asarray(cols >= 0)[None, :], w[:, np.maximum(cols, 0)], 0.0)


def _mla_weight_columns():
    dq = D_NOPE + D_ROPE
    half = D_ROPE // 2
    zeros = lambda n: np.full(n, -1)
    q1, q2, k1 = [], [], []
    for h in range(D_HEADS):
        rope0 = h * dq + D_NOPE
        q1 += [h * dq + np.arange(dq), zeros(LANES - dq)]
        q2 += [zeros(D_NOPE), rope0 + half + np.arange(half), rope0 + np.arange(half), zeros(LANES - dq)]
        k1 += [h * D_NOPE + np.arange(D_NOPE), zeros(LANES - D_NOPE)]
    return np.concatenate(q1), np.concatenate(q2), np.concatenate(k1)


def _rope_tables(seq):
    half = D_ROPE // 2
    inv = ROPE_THETA ** (-jnp.arange(half, dtype=F32) / half)
    ang = jnp.arange(seq).astype(F32)[:, None] * inv
    cos, sin = jnp.cos(ang), jnp.sin(ang)
    pad = jnp.zeros((seq, LANES - D_NOPE - D_ROPE), F32)
    cos_t = jnp.concatenate([jnp.ones((seq, D_NOPE), F32), cos, cos, pad], axis=1)
    sin_t = jnp.concatenate([jnp.zeros((seq, D_NOPE), F32), -sin, sin, pad], axis=1)
    return cos_t, sin_t


def kernel(x, rel_bias, ev_w_in, ev_w_out, ev_lambda, ev_subln, od_w_in, od_w_out, od_cmp_pe, od_cmp_w1, od_cmp_w2, od_q_norm, od_kv_norm, od_w_uq, od_w_uk, od_w_uv, ffn_w_up, ffn_conv_w, ffn_conv_b, ffn_w_down, ln_g, ln_b):
    b, s, _ = x.shape
    m = b * s
    assert s % 1024 == 0 and s // B_PAIRS[-1][1] >= 2 * T_DIL and s // SLC_BLOCK <= LANES
    n_d = s // T_ATT

    tb_t = rel_bias.T.astype(F32)
    tb_log2 = tb_t[:C_HEADS] * LOG2E
    n_prev_win = -(-(WIN_SIZE - 1) // T_ATT)
    tb_causal = _bias_table(tb_log2, _toeplitz_idx(1, n_d, T_ATT, s))
    tb_win = _bias_table(tb_log2, _toeplitz_idx(n_prev_win, n_prev_win + 1, T_ATT, WIN_SIZE - 1))
    tb_cmp = _bias_table(tb_log2, _cmp_idx(s)).reshape(C_HEADS, s, s // CMP_STRIDE)
    tb_dil = [_bias_table(tb_t[A_HEADS + i * B_HEADS:A_HEADS + (i + 1) * B_HEADS], _dilated_idx(d))
              for i, (_, d) in enumerate(B_PAIRS)]
    cos_t, sin_t = _rope_tables(s)
    odd_cols = _odd_in_columns()
    q1_cols, q2_cols, k1_cols = _mla_weight_columns()

    xf = x.reshape(m, D_MODEL)
    for l in range(DEPTH):
        i = l // 2
        if l % 2 == 0:
            n_a = EVEN_IN - EVEN_DILATED
            w_in = ev_w_in[i] * jnp.where(jnp.arange(EVEN_IN) < 2 * A_HEADS * A_QK, LOG2E, 1.0)
            h_a, h_b = _inproj(xf, w_in.astype(BF16), (n_a, EVEN_DILATED), (BF16, F32), 768)
            h3 = h_a.reshape(b, s, n_a)
            hd3 = h_b.reshape(b, s, EVEN_DILATED)
            lam_init = 0.8 - 0.6 * math.exp(-0.3 * l)
            o_a = _diff_attention(h3, tb_causal, ev_lambda[i].astype(F32), ev_subln[i], lam_init)
            obs, lses = [], []
            for p_idx, (_, d) in enumerate(B_PAIRS):
                o, lse = _dilated_attention(hd3, tb_dil[p_idx], p_idx, d)
                obs.append(o)
                lses.append(lse)
            xf = _even_out(o_a.reshape(m, -1), obs, lses, xf, ev_w_out[i].astype(BF16), ln_g[l, 0], ln_b[l, 0])
        else:
            w_in = _gather_cols(od_w_in[i], odd_cols) * jnp.where(jnp.arange(len(odd_cols)) < C_HEADS * C_DIM, LOG2E, 1.0)
            hb, hf = _inproj(xf, w_in.astype(BF16), (ODD_BF16_W, ODD_F32_W), (BF16, F32), 256)
            hb3 = hb.reshape(b, s, ODD_BF16_W)
            n_ch = s // CMP_STRIDE
            chunks = (hf[:, :4 * C_DIM].reshape(b, n_ch, CMP_STRIDE, 4, C_DIM)
                      .transpose(0, 3, 1, 2, 4).reshape(b, 4, n_ch, CMP_STRIDE * C_DIM))
            pe = od_cmp_pe[i].reshape(2, 2, CMP_STRIDE * C_DIM)
            w2d = jnp.concatenate([od_cmp_w2[i], od_cmp_w2[i]], axis=-1).astype(BF16)
            kvc = _compress(chunks, pe, od_cmp_w1[i].astype(BF16), w2d)
            o_cmp, sel = _cmp_attention(hb3, kvc, tb_cmp)
            o_slc = _slc_attention(hb3, tb_causal, 4, 6, sel)
            o_win = _win_attention(hb3, tb_win, 8, 10, n_prev_win)
            q_d, k_d, v_d = _mla_prep(hf, cos_t, sin_t, od_q_norm[i], od_kv_norm[i],
                                      _gather_cols(od_w_uq[i], q1_cols).astype(BF16),
                                      _gather_cols(od_w_uq[i], q2_cols).astype(BF16),
                                      _gather_cols(od_w_uk[i], k1_cols).astype(BF16),
                                      od_w_uv[i].astype(BF16), s)
            o_d = _mla_attention(q_d.reshape(b, s, -1), k_d.reshape(b, s, -1), v_d.reshape(b, s, -1))
            xf = _odd_out(hf, o_cmp.reshape(m, -1), o_slc.reshape(m, -1), o_win.reshape(m, -1),
                          o_d.reshape(m, -1), xf, od_w_out[i].astype(BF16), ln_g[l, 0], ln_b[l, 0])
        xf = _ffn(xf, ffn_w_up[l].astype(BF16), ffn_conv_w[l], ffn_conv_b[l], ffn_w_down[l].astype(BF16),
                  ln_g[l, 1], ln_b[l, 1], s)
    return xf.reshape(b, s, D_MODEL)
```

```python
import functools
import math

import numpy as np
import jax
import jax.numpy as jnp
from jax import lax
from jax.experimental import pallas as pl
from jax.experimental.pallas import tpu as pltpu

F32 = jnp.float32
BF16 = jnp.bfloat16

D_MODEL = 1024
DEPTH = 4
LN_EPS = 1e-5
RMS_EPS = 1e-5
N_BUCKETS = 32
MAX_DISTANCE = 2048
A_HEADS, A_QK, A_V = 4, 64, 128
B_PAIRS = ((128, 1), (512, 4), (2048, 16))
B_HEADS, B_DIM = 4, 64
C_HEADS, C_KV_GROUPS, C_DIM = 8, 2, 64
CMP_LEN, CMP_STRIDE, CMP_HIDDEN = 32, 16, 256
SLC_BLOCK, SLC_TOP_N = 64, 16
SLC_SHIFT = 6
WIN_SIZE = 512
D_HEADS, D_Q_LORA, D_KV_LORA, D_NOPE, D_ROPE, D_V = 8, 384, 256, 64, 32, 64
ROPE_THETA = 10000.0
D_FF = 2816
EVEN_DILATED = len(B_PAIRS) * 3 * B_HEADS * B_DIM
EVEN_IN = 2 * A_HEADS * A_QK * 2 + A_HEADS * A_V + EVEN_DILATED
ALPHA = (2 * DEPTH) ** 0.25

LANES = 128
HALF_LANE_SHIFT = 6
VMEM_LIMIT = 56 * 1024 * 1024
T_ATT = 256
MLA_HEADS_PER_STEP = 4
T_DIL = 128
TM = 512
FF_CHUNK = 256
NEG_INF = float("-inf")
M_INIT = -1e30
SEL_PENALTY = 2.0 ** 101
LOG2E = math.log2(math.e)
ONES_ROWS = 16


def _cparams(sem, flags=None):
    return pltpu.CompilerParams(dimension_semantics=sem, vmem_limit_bytes=VMEM_LIMIT, flags=flags)


def _dot(a, b):
    return jnp.dot(a, b, preferred_element_type=F32)


def _dot_nt(a, b):
    return lax.dot_general(a, b, (((1,), (1,)), ((), ())), preferred_element_type=F32)


def _bucket_np(dist):
    n = np.maximum(dist, 0)
    nf = np.maximum(n, 1).astype(np.float32)
    max_exact = N_BUCKETS // 2
    large = max_exact + (np.log(nf / max_exact) / math.log(MAX_DISTANCE / max_exact)
                         * (N_BUCKETS - max_exact)).astype(np.int32)
    return np.where(n < max_exact, n, np.minimum(large, N_BUCKETS - 1)).astype(np.int32)


def _toeplitz_idx(n_masked, n_delta, t, max_dist):
    key = np.arange(t)[:, None]
    query = np.arange(t)[None, :]
    out = []
    for delta in range(-n_masked, n_delta):
        dist = t * delta + query - key
        ok = (dist >= 0) & (dist <= max_dist)
        out.append(np.where(ok, _bucket_np(dist), -1))
    return np.stack(out).astype(np.int32)


def _dilated_idx(d):
    t = T_DIL
    r = np.arange(t)[:, None]
    c = np.arange(2 * t)[None, :]
    dist0 = r - c
    ok0 = (dist0 >= 0) & (c < t)
    dist1 = t + r - c
    ok1 = (dist1 >= 0) & (dist1 <= t)
    return np.stack([np.where(ok0, _bucket_np(dist0 * d), -1),
                     np.where(ok1, _bucket_np(dist1 * d), -1)]).astype(np.int32)


def _cmp_idx(seq):
    ncb = seq // CMP_STRIDE - 1
    q = np.arange(seq)[:, None]
    c = np.arange(seq // CMP_STRIDE)[None, :]
    dist = q - (c * CMP_STRIDE + CMP_LEN - 1)
    ok = (dist >= 0) & (c < ncb)
    return np.where(ok, _bucket_np(dist), -1).astype(np.int32).reshape(seq // T_ATT, T_ATT, seq // CMP_STRIDE)


def _bias_kernel(tbl_ref, idx_ref, o_ref):
    idx = idx_ref[0]
    rows = idx.shape[0]
    row = jnp.broadcast_to(tbl_ref[0], (rows, LANES))
    for c0 in range(0, idx.shape[1], LANES):
        ix = idx[:, c0:c0 + LANES]
        val = jnp.take_along_axis(row, jnp.maximum(ix, 0), axis=1)
        o_ref[0, 0, :, c0:c0 + LANES] = jnp.where(ix < 0, NEG_INF, val)


def _bias_table(tbl, idx):
    n_h = tbl.shape[0]
    n, r, c = idx.shape
    tbl_rows = jnp.pad(tbl, ((0, 0), (0, LANES - N_BUCKETS))).reshape(n_h, 1, LANES)
    return pl.pallas_call(
        _bias_kernel,
        grid=(n_h, n),
        in_specs=[pl.BlockSpec((1, 1, LANES), lambda h, i: (h, 0, 0)),
                  pl.BlockSpec((1, r, c), lambda h, i: (i, 0, 0))],
        out_specs=pl.BlockSpec((1, 1, r, c), lambda h, i: (h, i, 0, 0)),
        out_shape=jax.ShapeDtypeStruct((n_h, n, r, c), F32),
        compiler_params=_cparams(("arbitrary", "arbitrary")),
        name="bias_table",
    )(tbl_rows, jnp.asarray(idx))


def _inproj_kernel(x_ref, w_ref, *o_refs, widths, chunk):
    xb = x_ref[...].astype(BF16)
    off = 0
    for o_ref, width in zip(o_refs, widths):
        for c0 in range(0, width, chunk):
            o_ref[:, c0:c0 + chunk] = _dot(xb, w_ref[:, off + c0:off + c0 + chunk]).astype(o_ref.dtype)
        off += width


def _inproj(x, w, widths, dtypes, chunk):
    m = x.shape[0]
    n = w.shape[1]
    assert sum(widths) == n and all(wd % chunk == 0 for wd in widths)
    return pl.pallas_call(
        functools.partial(_inproj_kernel, widths=widths, chunk=chunk),
        grid=(m // TM,),
        in_specs=[pl.BlockSpec((TM, D_MODEL), lambda i: (i, 0)),
                  pl.BlockSpec((D_MODEL, n), lambda i: (0, 0))],
        out_specs=[pl.BlockSpec((TM, wd), lambda i: (i, 0)) for wd in widths],
        out_shape=[jax.ShapeDtypeStruct((m, wd), dt) for wd, dt in zip(widths, dtypes)],
        compiler_params=_cparams(("arbitrary",)),
        name="in_proj",
    )(x, w)


def _tree_reduce_rows(x, combine, reduce_fn):
    parts = [x[r:r + 8] for r in range(0, x.shape[0], 8)]
    while len(parts) > 1:
        parts = [combine(parts[k], parts[k + 1]) for k in range(0, len(parts) - 1, 2)] + \
                ([parts[-1]] if len(parts) % 2 else [])
    return reduce_fn(parts[0], axis=0, keepdims=True)


def _flash_init(streams):
    return tuple((jnp.full((1, q.shape[0]), M_INIT, F32), jnp.zeros((vt_rows.stop - vt_rows.start, q.shape[0]), F32))
                 for q, _, _, _, vt_rows in streams)


def _normalise(acc):
    d_v = acc.shape[0] - ONES_ROWS
    return acc[:d_v] / acc[d_v:d_v + 1]


def _flash_scores(streams, key_start, n_keys):
    keys = pl.ds(pl.multiple_of(key_start, T_ATT), n_keys)
    return tuple(_dot_nt(k_ref[0, keys, k_cols], q) for q, k_ref, k_cols, _, _ in streams)


def _flash_update(streams, score_fns, key_start, n_keys, i, raw, state):
    keys = pl.ds(pl.multiple_of(key_start, T_ATT), n_keys)
    scores = [fn(i, s) for fn, s in zip(score_fns, raw)]
    m_new = [jnp.maximum(m, _tree_reduce_rows(s, jnp.maximum, jnp.max)) for s, (m, _) in zip(scores, state)]
    probs = [jnp.exp2(s - mn) for s, mn in zip(scores, m_new)]
    out = []
    for (_, _, _, vt_ref, vt_rows), p, mn, (m, acc) in zip(streams, probs, m_new, state):
        acc = jnp.exp2(m - mn) * acc + _dot(vt_ref[vt_rows, keys], p.astype(BF16))
        out.append((mn, acc))
    return tuple(out)


def _flash_finish(state):
    return [_normalise(acc) for _, acc in state]


def _flash_balanced(pi, n_d, qa, qb, k_refs, k_cols, vt_ref, vt_rows, score, key_extra=None):
    kt = 2 * T_ATT
    t_q = qa[0].shape[0]
    n_streams = len(qa)
    n_slots = n_d // 2 + 1
    max_a = n_d // 4
    n_a = pi // 2 + 1
    tile_a, tile_b = pi, n_d - 1 - pi

    def slot(s):
        if s == 0:
            return True, tile_a, n_a - 1
        if s >= max_a:
            return False, tile_b, s - n_a
        is_a = s < n_a
        return is_a, jnp.where(is_a, tile_a, tile_b), jnp.where(is_a, n_a - 1 - s, s - n_a)

    def pick(is_a, a, b):
        return (a if is_a else b) if isinstance(is_a, bool) else jnp.where(is_a, a, b)

    def keys_of(i):
        return pl.ds(pl.multiple_of(i * kt, kt), kt)

    def raw_scores(s):
        is_a, _, i = slot(s)
        extra = None if key_extra is None else key_extra(i)
        out = []
        for n in range(n_streams):
            keys = k_refs[n][0, keys_of(i), k_cols[n]]
            if extra is not None:
                keys = jnp.concatenate([keys, extra], axis=1)
            out.append(_dot_nt(keys, pick(is_a, qa[n], qb[n])))
        return out

    state = [(jnp.full((1, t_q), M_INIT, F32), jnp.zeros((vt_rows[n].stop - vt_rows[n].start, t_q), F32))
             for n in range(n_streams)]
    out_a = [jnp.zeros((acc.shape[0] - ONES_ROWS, t_q), F32) for _, acc in state]
    pending = None
    raw = raw_scores(0)

    def fold(state, pending):
        alphas, probs, i_prev = pending
        return [(m, alphas[n] * acc + _dot(vt_ref[vt_rows[n], keys_of(i_prev)], probs[n]))
                for n, (m, acc) in enumerate(state)]

    for s in range(n_slots):
        raw_next = raw_scores(s + 1) if s + 1 < n_slots else None
        is_a, q_tile, i = slot(s)
        if pending is not None:
            state = fold(state, pending)
        if 1 <= s <= max_a:
            switch = s == n_a
            out_a = [jnp.where(switch, _normalise(acc), o) for (_, acc), o in zip(state, out_a)]
            state = [(jnp.where(switch, M_INIT, m), jnp.where(switch, 0.0, acc)) for m, acc in state]
        diag = s == 0 or s == n_slots - 1
        scores = [score(n, q_tile, i, raw[n], diag) for n in range(n_streams)]
        m_new = [jnp.maximum(m, _tree_reduce_rows(sc, jnp.maximum, jnp.max)) for sc, (m, _) in zip(scores, state)]
        probs = [jnp.exp2(sc - mn).astype(BF16) for sc, mn in zip(scores, m_new)]
        alphas = [jnp.exp2(m - mn) for (m, _), mn in zip(state, m_new)]
        state = [(mn, acc) for mn, (_, acc) in zip(m_new, state)]
        pending = (alphas, probs, i)
        raw = raw_next
    state = fold(state, pending)
    return out_a, [_normalise(acc) for _, acc in state]


def _pair_bias(tb_ref, h, q_tile, i, s):
    t = T_ATT
    d0 = q_tile - 2 * i
    return jnp.concatenate([s[:t] + tb_ref[h, d0 + 1], s[t:] + tb_ref[h, d0]], axis=0)


def _fill_transposed(vt_ref, v_ref, d_v):
    n = v_ref.shape[1]
    n_heads = vt_ref.shape[0] // (d_v + ONES_ROWS)
    step = 2 * T_ATT
    for c0 in range(0, n, step):
        v_t = v_ref[0, c0:c0 + step, :].astype(F32).T.astype(BF16)
        for h in range(n_heads):
            r0 = h * (d_v + ONES_ROWS)
            vt_ref[r0:r0 + d_v, c0:c0 + step] = v_t[h * d_v:(h + 1) * d_v]
            vt_ref[r0 + d_v:r0 + d_v + ONES_ROWS, c0:c0 + step] = jnp.ones((ONES_ROWS, step), BF16)


def _value_rows(h, d_v):
    return slice(h * (d_v + ONES_ROWS), (h + 1) * (d_v + ONES_ROWS))


def _lane_lo(rows):
    return lax.broadcasted_iota(jnp.int32, (rows, LANES), 1) < (LANES // 2)


def _diff_kernel(lam_ref, g_ref, q0a_ref, q0b_ref, q1a_ref, q1b_ref, k0_ref, k1_ref, v_ref, tb_ref,
                 oa_ref, ob_ref, vt_ref, *, lam_init, n_d):
    pi = pl.program_id(2)

    @pl.when(pi == 0)
    def _():
        _fill_transposed(vt_ref, v_ref, A_V)

    lp = lam_ref[...]
    lam = (jnp.exp(jnp.sum(lp[0:1] * lp[1:2], axis=-1, keepdims=True))
           - jnp.exp(jnp.sum(lp[2:3] * lp[3:4], axis=-1, keepdims=True)) + lam_init)
    lo = _lane_lo(T_ATT)

    def prep(q_ref):
        q = q_ref[0] * (A_QK ** -0.5)
        zero = jnp.zeros_like(q)
        return [jnp.where(lo, q, zero), jnp.where(lo, zero, q)]

    full = slice(0, LANES)

    def score(n, q_tile, i, s, diag):
        return _pair_bias(tb_ref, n % 2, q_tile, i, s)

    out_a, out_b = _flash_balanced(pi, n_d, prep(q0a_ref) + prep(q1a_ref), prep(q0b_ref) + prep(q1b_ref),
                                   [k0_ref, k0_ref, k1_ref, k1_ref], [full] * 4, vt_ref,
                                   [_value_rows(0, A_V), _value_rows(1, A_V)] * 2, score)
    for outs, o_ref in ((out_a, oa_ref), (out_b, ob_ref)):
        for head in range(2):
            d = (outs[head] - lam * outs[2 + head]).T
            ms = jnp.mean(d * d, axis=-1, keepdims=True)
            o_ref[0, :, head * A_V:(head + 1) * A_V] = (d * lax.rsqrt(ms + RMS_EPS) * g_ref[...]
                                                        * (1.0 - lam_init)).astype(o_ref.dtype)


def _join_halves(o_a, o_b):
    return jnp.concatenate([o_a, o_b], axis=1)


def _diff_attention(h3, tb, lam_p, subln_g, lam_init):
    b, s, _ = h3.shape
    t = T_ATT
    n_d = s // t
    n_p = n_d // 2
    q_spec = lambda col0, second: pl.BlockSpec(
        (1, t, LANES), lambda hp, bi, pi: (bi, (n_d - 1 - pi) if second else pi, col0 + hp))
    k_spec = lambda col0: pl.BlockSpec((1, s, LANES), lambda hp, bi, pi: (bi, 0, col0 + hp))
    o_a, o_b = pl.pallas_call(
        functools.partial(_diff_kernel, lam_init=lam_init, n_d=n_d),
        grid=(A_HEADS // 2, b, n_p),
        in_specs=[pl.BlockSpec((4, A_QK), lambda hp, bi, pi: (0, 0)),
                  pl.BlockSpec((1, A_V), lambda hp, bi, pi: (0, 0)),
                  q_spec(0, False), q_spec(0, True), q_spec(2, False), q_spec(2, True),
                  k_spec(4), k_spec(6),
                  pl.BlockSpec((1, s, 2 * A_V), lambda hp, bi, pi: (bi, 0, 4 + hp)),
                  pl.BlockSpec((2, n_d + 1, t, t), lambda hp, bi, pi: (hp, 0, 0, 0))],
        out_specs=[pl.BlockSpec((1, t, 2 * A_V), lambda hp, bi, pi: (bi, pi, hp)),
                   pl.BlockSpec((1, t, 2 * A_V), lambda hp, bi, pi: (bi, n_p - 1 - pi, hp))],
        out_shape=[jax.ShapeDtypeStruct((b, s // 2, A_HEADS * A_V), BF16)] * 2,
        scratch_shapes=[pltpu.VMEM((2 * (A_V + ONES_ROWS), s), BF16)],
        compiler_params=_cparams(("arbitrary", "arbitrary", "arbitrary")),
        name="diff_attention",
    )(lam_p, subln_g.reshape(1, A_V), h3, h3, h3, h3, h3, h3, h3, tb)
    return _join_halves(o_a, o_b)


def _dilated_kernel(q_ref, k_ref, v_ref, tb_ref, o_ref, lse_ref, *, d, residues):
    t = T_DIL
    n_blocks = q_ref.shape[1] // (d * t)
    lo = _lane_lo(t)
    scale = B_DIM ** -0.5

    def block(idx, carry):
        r = pl.program_id(2) * residues + idx // n_blocks
        bi = idx % n_blocks
        var = jnp.minimum(bi, 1)
        qrows = pl.ds(r + d * t * bi, t, stride=d)
        krows = pl.ds(r + d * t * jnp.maximum(bi - 1, 0), 2 * t, stride=d)
        q = (q_ref[0, qrows, :] * scale).astype(BF16)
        kk = k_ref[0, krows, :].astype(BF16)
        vv = v_ref[0, krows, :].astype(BF16)
        o_half, l_half = [], []
        for half in range(2):
            keep = lo if half == 0 else jnp.logical_not(lo)
            qm = jnp.where(keep, q, jnp.zeros_like(q))
            s = _dot_nt(qm, kk) + tb_ref[half, var]
            m = jnp.max(s, axis=-1, keepdims=True)
            p = jnp.exp(s - m)
            l = jnp.sum(p, axis=-1, keepdims=True)
            o_half.append(_dot(p.astype(BF16), vv) / l)
            l_half.append(m + jnp.log(l))
        o_ref[0, qrows, :] = jnp.where(lo, o_half[0], o_half[1])
        lse_ref[0, qrows, :] = jnp.where(lo, l_half[0], l_half[1])
        return carry

    lax.fori_loop(0, residues * n_blocks, block, 0, unroll=8)


def _dilated_attention(hd3, tb, pair_idx, d):
    b, s, _ = hd3.shape
    hd = B_HEADS * B_DIM
    base = pair_idx * 3 * (hd // LANES)
    seq = lambda which: pl.BlockSpec((1, s, LANES), lambda bi, hp, r: (bi, 0, base + which * (hd // LANES) + hp))
    out = pl.BlockSpec((1, s, LANES), lambda bi, hp, r: (bi, 0, hp))
    n_blocks = s // (d * T_DIL)
    residues = min(d, max(1, 8 // n_blocks))
    assert d % residues == 0 and (residues * n_blocks) % 4 == 0
    o, lse = pl.pallas_call(
        functools.partial(_dilated_kernel, d=d, residues=residues),
        grid=(b, hd // LANES, d // residues),
        in_specs=[seq(0), seq(1), seq(2),
                  pl.BlockSpec((2, 2, T_DIL, 2 * T_DIL), lambda bi, hp, r: (hp, 0, 0, 0))],
        out_specs=[out, out],
        out_shape=[jax.ShapeDtypeStruct((b, s, hd), F32)] * 2,
        compiler_params=_cparams(("arbitrary", "arbitrary", "arbitrary")),
        name="dilated_attention",
    )(hd3, hd3, hd3, tb)
    return o.reshape(b * s, hd), lse.reshape(b * s, hd)


def _residual_ln(x, y, g, beta):
    z = ALPHA * x + y
    mu = jnp.mean(z, axis=-1, keepdims=True)
    zc = z - mu
    var = jnp.mean(zc * zc, axis=-1, keepdims=True)
    return zc * lax.rsqrt(var + LN_EPS) * g + beta


def _even_out_kernel(oa_ref, o0_ref, o1_ref, o2_ref, l0_ref, l1_ref, l2_ref, x_ref, w_ref, g_ref, b_ref, out_ref):
    l0, l1, l2 = l0_ref[...], l1_ref[...], l2_ref[...]
    mx = jnp.maximum(jnp.maximum(l0, l1), l2)
    e0, e1, e2 = jnp.exp(l0 - mx), jnp.exp(l1 - mx), jnp.exp(l2 - mx)
    den = e0 + e1 + e2
    ob = (e0 / den) * o0_ref[...] + (e1 / den) * o1_ref[...] + (e2 / den) * o2_ref[...]
    n_a = A_HEADS * A_V
    y = _dot(oa_ref[...], w_ref[0:n_a, :]) + _dot(ob.astype(BF16), w_ref[n_a:, :])
    out_ref[...] = _residual_ln(x_ref[...], y, g_ref[...], b_ref[...])


def _even_out(oa, obs, lses, x, w_out, g, beta):
    m = x.shape[0]
    hd = B_HEADS * B_DIM
    row = lambda width: pl.BlockSpec((TM, width), lambda i: (i, 0))
    const = lambda shape: pl.BlockSpec(shape, lambda i: (0, 0))
    return pl.pallas_call(
        _even_out_kernel,
        grid=(m // TM,),
        in_specs=[row(A_HEADS * A_V)] + [row(hd)] * 6 + [row(D_MODEL), const(w_out.shape),
                                                        const((1, D_MODEL)), const((1, D_MODEL))],
        out_specs=row(D_MODEL),
        out_shape=jax.ShapeDtypeStruct((m, D_MODEL), F32),
        compiler_params=_cparams(("arbitrary",)),
        name="even_out",
    )(oa, *obs, *lses, x, w_out, g.reshape(1, -1), beta.reshape(1, -1))


def _gelu(x):
    return 0.5 * x * (1.0 + jnp.tanh(math.sqrt(2.0 / math.pi) * (x + 0.044715 * (x * x * x))))


def _ffn_kernel(x_ref, wu_ref, cw_ref, cb_ref, wd_ref, g_ref, b_ref, out_ref, u_ref, gs_ref, tail_ref,
                *, tiles_per_seq):
    halo = 8
    n_chunks = D_FF // FF_CHUNK

    @pl.when(pl.program_id(0) % tiles_per_seq == 0)
    def _():
        tail_ref[...] = jnp.zeros(tail_ref.shape, F32)

    x = x_ref[...]
    xb = x.astype(BF16)

    def up(c):
        cols = slice(c * FF_CHUNK, (c + 1) * FF_CHUNK)
        return _dot(xb, wu_ref[:, cols]), _dot(xb, wu_ref[:, D_FF + c * FF_CHUNK:D_FF + (c + 1) * FF_CHUNK])

    def activate(c, a, gate):
        cols = slice(c * FF_CHUNK, (c + 1) * FF_CHUNK)
        gs = gs_ref.at[c % 2]
        gs[0:halo, :] = tail_ref[:, cols]
        gs[halo:, :] = gate
        tail_ref[:, cols] = gate[TM - halo:, :]
        conv = (gs[pl.ds(halo - 2, TM), :] * cw_ref[0:1, cols] + gs[pl.ds(halo - 1, TM), :] * cw_ref[1:2, cols]
                + gate * cw_ref[2:3, cols] + cb_ref[:, cols])
        u_ref[:, cols] = (_gelu(conv) * a).astype(BF16)

    nxt = up(0)
    for c in range(n_chunks):
        cur = nxt
        if c + 1 < n_chunks:
            nxt = up(c + 1)
        activate(c, *cur)
    out_ref[...] = _residual_ln(x, _dot(u_ref[...], wd_ref[...]), g_ref[...], b_ref[...])


def _ffn(x, w_up, conv_w, conv_b, w_down, g, beta, seq):
    m = x.shape[0]
    const = lambda shape: pl.BlockSpec(shape, lambda i: (0, 0))
    return pl.pallas_call(
        functools.partial(_ffn_kernel, tiles_per_seq=seq // TM),
        grid=(m // TM,),
        in_specs=[pl.BlockSpec((TM, D_MODEL), lambda i: (i, 0)), const(w_up.shape), const(conv_w.shape),
                  const((1, D_FF)), const(w_down.shape), const((1, D_MODEL)), const((1, D_MODEL))],
        out_specs=pl.BlockSpec((TM, D_MODEL), lambda i: (i, 0)),
        out_shape=jax.ShapeDtypeStruct((m, D_MODEL), F32),
        scratch_shapes=[pltpu.VMEM((TM, D_FF), BF16), pltpu.VMEM((2, TM + 8, FF_CHUNK), F32),
                        pltpu.VMEM((8, D_FF), F32)],
        compiler_params=_cparams(("arbitrary",)),
        name="conv_ffn",
    )(x, w_up, conv_w, conv_b.reshape(1, -1), w_down, g.reshape(1, -1), beta.reshape(1, -1))


def _compress_kernel(ch_ref, pe_ref, w1_ref, w2_ref, o_ref):
    half = CMP_STRIDE * C_DIM
    ch = ch_ref[0, 0]
    a = _dot((ch + pe_ref[0, 0:1, :]).astype(BF16), w1_ref[0, 0:half, :])
    b = _dot((ch + pe_ref[0, 1:2, :]).astype(BF16), w1_ref[0, half:, :])
    n = ch.shape[0]
    hid = _gelu(a + pltpu.roll(b, n - 1, 0))
    o_ref[0, 0] = _dot(hid.astype(BF16), w2_ref[0]).astype(o_ref.dtype)


def _compress(chunks, pe, w1, w2d):
    b, _, n, width = chunks.shape
    return pl.pallas_call(
        _compress_kernel,
        grid=(b, 4),
        in_specs=[pl.BlockSpec((1, 1, n, width), lambda bi, j: (bi, j, 0, 0)),
                  pl.BlockSpec((1, 2, width), lambda bi, j: (j // 2, 0, 0)),
                  pl.BlockSpec((1, 2 * width, CMP_HIDDEN), lambda bi, j: (j // 2, 0, 0)),
                  pl.BlockSpec((1, CMP_HIDDEN, LANES), lambda bi, j: (j // 2, 0, 0))],
        out_specs=pl.BlockSpec((1, 1, n, LANES), lambda bi, j: (bi, j, 0, 0)),
        out_shape=jax.ShapeDtypeStruct((b, 4, n, LANES), BF16),
        compiler_params=_cparams(("arbitrary", "arbitrary")),
        name="nsa_compress",
    )(chunks, pe, w1, w2d)


def _cmp_kernel(q_ref, kc_ref, vc_ref, tb_ref, o_ref, sel_ref, pg_ref, sc_ref, pen_ref, *, tiles):
    t = T_ATT
    n_c = kc_ref.shape[2]
    n_sel = n_c // (SLC_BLOCK // CMP_STRIDE)
    lo = _lane_lo(t)
    kc = kc_ref[0, 0]
    vc = vc_ref[0, 0]
    raw = []
    for u in range(tiles):
        for pair in range(2):
            q = q_ref[0, u * t:(u + 1) * t, pair * LANES:(pair + 1) * LANES] * (C_DIM ** -0.5)
            zero = jnp.zeros_like(q)
            raw += [_dot_nt(jnp.where(lo, q, zero), kc), _dot_nt(jnp.where(lo, zero, q), kc)]
    probs = []
    for n, s in enumerate(raw):
        u, head = divmod(n, 4)
        s = s + tb_ref[head, u * t:(u + 1) * t, :]
        m = jnp.maximum(jnp.max(s, axis=-1, keepdims=True), M_INIT)
        p = jnp.exp2(s - m)
        den = jnp.sum(p, axis=-1, keepdims=True)
        probs.append(p / jnp.where(den > 0, den, 1.0))
    outs = [_dot(p.astype(BF16), vc) for p in probs]
    for u in range(tiles):
        for pair in range(2):
            o_ref[0, u * t:(u + 1) * t, pair * LANES:(pair + 1) * LANES] = jnp.where(
                lo, outs[4 * u + 2 * pair], outs[4 * u + 2 * pair + 1])
    for u in range(tiles):
        _select_blocks((probs[4 * u] + probs[4 * u + 1]) + (probs[4 * u + 2] + probs[4 * u + 3]),
                       pl.program_id(1) * tiles + u, sel_ref.at[0, 0, u * t:(u + 1) * t, :],
                       pg_ref.at[u], sc_ref.at[u], pen_ref.at[u], n_c, n_sel)


def _select_blocks(pg, qi, sel_out, pg_ref, sc_ref, pen_ref, n_c, n_sel):
    t = T_ATT
    pad = 8
    pg_t = pg.T
    for c in range(t // LANES):
        pg_ref[c, 0:pad, :] = jnp.zeros((pad, LANES), F32)
        pg_ref[c, pad:pad + n_c, :] = pg_t[:, c * LANES:(c + 1) * LANES]
        pg_ref[c, pad + n_c:, :] = jnp.zeros((pad, LANES), F32)
    r = SLC_BLOCK // CMP_STRIDE
    tap = lambda k: jnp.concatenate(
        [pg_ref[c, pl.ds(pad + k, n_sel, stride=r), :] for c in range(t // LANES)], axis=1)
    score = (0.5 * tap(-1) + ((tap(0) + tap(1)) + tap(2))) + 0.5 * tap(3)
    jb = lax.broadcasted_iota(jnp.int32, (n_sel, t), 0)
    qblk = jnp.right_shift(qi * t + lax.broadcasted_iota(jnp.int32, (n_sel, t), 1), SLC_SHIFT)
    forced = (jb == 0) | (jb == qblk) | (jb == qblk - 1)
    sc_ref[...] = jnp.where(forced, jnp.inf, jnp.where(jb <= qblk, score, NEG_INF))
    pen_ref[...] = jnp.zeros(pen_ref.shape, F32)
    groups = range(0, n_sel, 8)
    sc_g = [sc_ref[g0:g0 + 8, :] for g0 in groups]
    sub = lax.broadcasted_iota(jnp.int32, (8, t), 0)
    cnt_g = [jnp.zeros((8, t), jnp.int32) for _ in groups]
    for i in range(n_sel):
        row = jnp.broadcast_to(sc_ref[pl.ds(i, 1), :], (8, t))
        for k, g0 in enumerate(groups):
            if i < g0:
                beats = row >= sc_g[k]
            elif i >= g0 + 8:
                beats = row > sc_g[k]
            else:
                beats = (row > sc_g[k]) | ((row == sc_g[k]) & (i - g0 < sub))
            cnt_g[k] = cnt_g[k] + beats.astype(jnp.int32)
    for k, g0 in enumerate(groups):
        pen_ref[g0:g0 + 8, :] = jnp.where(cnt_g[k] < SLC_TOP_N, 0.0, -SEL_PENALTY)
    sel_out[...] = pen_ref[...].T.astype(sel_out.dtype)


def _cmp_attention(hb3, kvc, tb):
    b, s, _ = hb3.shape
    t = T_ATT
    n_c = s // CMP_STRIDE
    n_sel = s // SLC_BLOCK
    assert n_sel <= LANES
    gw = (C_HEADS // C_KV_GROUPS) * C_DIM
    tiles = 2
    rows = tiles * t
    return pl.pallas_call(
        functools.partial(_cmp_kernel, tiles=tiles),
        grid=(C_KV_GROUPS, s // rows, b),
        in_specs=[pl.BlockSpec((1, rows, gw), lambda g, qi, bi: (bi, qi, g)),
                  pl.BlockSpec((1, 1, n_c, LANES), lambda g, qi, bi: (bi, g, 0, 0)),
                  pl.BlockSpec((1, 1, n_c, LANES), lambda g, qi, bi: (bi, 2 + g, 0, 0)),
                  pl.BlockSpec((4, rows, n_c), lambda g, qi, bi: (g, qi, 0))],
        out_specs=[pl.BlockSpec((1, rows, gw), lambda g, qi, bi: (bi, qi, g)),
                   pl.BlockSpec((1, 1, rows, LANES), lambda g, qi, bi: (bi, g, qi, 0))],
        out_shape=[jax.ShapeDtypeStruct((b, s, C_HEADS * C_DIM), F32),
                   jax.ShapeDtypeStruct((b, C_KV_GROUPS, s, LANES), BF16)],
        scratch_shapes=[pltpu.VMEM((tiles, t // LANES, n_c + 16, LANES), F32), pltpu.VMEM((tiles, n_sel, t), F32),
                        pltpu.VMEM((tiles, LANES, t), F32)],
        compiler_params=_cparams(("arbitrary", "arbitrary", "arbitrary")),
        name="nsa_cmp_attention",
    )(hb3, kvc, kvc, tb)


def _head_pair_queries(q_ref):
    lo = _lane_lo(T_ATT)
    q = q_ref[0] * (C_DIM ** -0.5)
    zero = jnp.zeros_like(q)
    return [jnp.where(lo, q, zero), jnp.where(lo, zero, q)]


def _slc_kernel(qa_ref, qb_ref, k_ref, v_ref, tb_ref, sela_ref, selb_ref, oa_ref, ob_ref, vt_ref, *, n_d):
    t = T_ATT
    pi = pl.program_id(2)
    hg = C_HEADS // C_KV_GROUPS

    @pl.when(pi == 0)
    def _():
        _fill_transposed(vt_ref, v_ref, C_DIM)

    key_blk = jnp.right_shift(lax.broadcasted_iota(jnp.int32, (2 * t, LANES), 0), SLC_SHIFT)
    blk_slot = lax.broadcasted_iota(jnp.int32, (2 * t, LANES), 1)

    def one_hot_block(i):
        return jnp.where(blk_slot == key_blk + i * (2 * t // SLC_BLOCK), 1.0, 0.0).astype(BF16)

    lo = _lane_lo(t)

    def with_penalty(q_ref, sel_ref):
        out = []
        for pair in range(hg // 2):
            q = q_ref[0, :, pair * LANES:(pair + 1) * LANES] * (C_DIM ** -0.5)
            zero = jnp.zeros_like(q)
            out += [jnp.concatenate([qh, sel_ref[0, 0]], axis=1) for qh in (jnp.where(lo, q, zero), jnp.where(lo, zero, q))]
        return out

    def score(n, q_tile, i, s, diag):
        return _pair_bias(tb_ref, n, q_tile, i, s)

    full = slice(0, LANES)
    out_a, out_b = _flash_balanced(pi, n_d, with_penalty(qa_ref, sela_ref), with_penalty(qb_ref, selb_ref),
                                   [k_ref] * hg, [full] * hg, vt_ref, [_value_rows(0, C_DIM)] * hg, score,
                                   key_extra=one_hot_block)
    oa_ref[0] = jnp.concatenate(out_a, axis=0).T
    ob_ref[0] = jnp.concatenate(out_b, axis=0).T


def _win_kernel(q_ref, k_ref, v_ref, tb_ref, o_ref, vt_ref, *, n_prev, tiles_per_step):
    t = T_ATT
    step = pl.program_id(2)

    @pl.when(step == 0)
    def _():
        _fill_transposed(vt_ref, v_ref, C_DIM)

    full = slice(0, LANES)
    n_keys = (n_prev + 1) * t
    lo = _lane_lo(t)
    work = []
    for u_tile in range(tiles_per_step):
        qi = step * tiles_per_step + u_tile
        q = q_ref[0, u_tile * t:(u_tile + 1) * t, :] * (C_DIM ** -0.5)
        zero = jnp.zeros_like(q)
        streams = [(qh, k_ref, full, vt_ref, _value_rows(0, C_DIM))
                   for qh in (jnp.where(lo, q, zero), jnp.where(lo, zero, q))]
        j0 = jnp.maximum(qi - n_prev, 0)
        work.append((qi, j0, streams, _flash_scores(streams, j0 * t, n_keys)))
    for u_tile, (qi, j0, streams, raw) in enumerate(work):
        def score_fn(h, qi=qi, j0=j0):
            def fn(_, s):
                return jnp.concatenate([s[u * t:(u + 1) * t] + tb_ref[h, qi - j0 - u + n_prev]
                                        for u in range(n_prev + 1)], axis=0)
            return fn

        state = _flash_update(streams, [score_fn(0), score_fn(1)], j0 * t, n_keys, 0, raw, _flash_init(streams))
        o_ref[0, u_tile * t:(u_tile + 1) * t, :] = jnp.concatenate(_flash_finish(state), axis=0).T


def _slc_attention(hb3, tb, k_blk, v_blk, sel):
    b, s, _ = hb3.shape
    t = T_ATT
    n_d = s // t
    n_p = n_d // 2
    hg = C_HEADS // C_KV_GROUPS
    gw = hg * C_DIM
    o_a, o_b = pl.pallas_call(
        functools.partial(_slc_kernel, n_d=n_d),
        grid=(C_KV_GROUPS, b, n_p),
        in_specs=[pl.BlockSpec((1, t, gw), lambda g, bi, pi: (bi, pi, g)),
                  pl.BlockSpec((1, t, gw), lambda g, bi, pi: (bi, n_d - 1 - pi, g)),
                  pl.BlockSpec((1, s, LANES), lambda g, bi, pi: (bi, 0, k_blk + g)),
                  pl.BlockSpec((1, s, LANES), lambda g, bi, pi: (bi, 0, v_blk + g)),
                  pl.BlockSpec((hg, n_d + 1, t, t), lambda g, bi, pi: (g, 0, 0, 0), pipeline_mode=pl.Buffered(1)),
                  pl.BlockSpec((1, 1, t, LANES), lambda g, bi, pi: (bi, g, pi, 0)),
                  pl.BlockSpec((1, 1, t, LANES), lambda g, bi, pi: (bi, g, n_d - 1 - pi, 0))],
        out_specs=[pl.BlockSpec((1, t, gw), lambda g, bi, pi: (bi, pi, g)),
                   pl.BlockSpec((1, t, gw), lambda g, bi, pi: (bi, n_p - 1 - pi, g))],
        out_shape=[jax.ShapeDtypeStruct((b, s // 2, C_HEADS * C_DIM), F32)] * 2,
        scratch_shapes=[pltpu.VMEM((C_DIM + ONES_ROWS, s), BF16)],
        compiler_params=_cparams(("arbitrary", "arbitrary", "arbitrary")),
        name="nsa_selected",
    )(hb3, hb3, hb3, hb3, tb, sel, sel)
    return _join_halves(o_a, o_b)


def _win_attention(hb3, tb, k_blk, v_blk, n_prev):
    b, s, _ = hb3.shape
    t = T_ATT
    assert s >= (n_prev + 1) * t
    tiles = 8
    return pl.pallas_call(
        functools.partial(_win_kernel, n_prev=n_prev, tiles_per_step=tiles),
        grid=(C_HEADS // 2, b, s // (tiles * t)),
        in_specs=[pl.BlockSpec((1, tiles * t, LANES), lambda hp, bi, qi: (bi, qi, hp)),
                  pl.BlockSpec((1, s, LANES), lambda hp, bi, qi: (bi, 0, k_blk + hp // 2)),
                  pl.BlockSpec((1, s, LANES), lambda hp, bi, qi: (bi, 0, v_blk + hp // 2)),
                  pl.BlockSpec((2, 2 * n_prev + 1, t, t), lambda hp, bi, qi: (hp, 0, 0, 0))],
        out_specs=pl.BlockSpec((1, tiles * t, LANES), lambda hp, bi, qi: (bi, qi, hp)),
        out_shape=jax.ShapeDtypeStruct((b, s, C_HEADS * C_DIM), F32),
        scratch_shapes=[pltpu.VMEM((C_DIM + ONES_ROWS, s), BF16)],
        compiler_params=_cparams(("arbitrary", "arbitrary", "arbitrary")),
        name="nsa_window",
    )(hb3, hb3, hb3, tb)


def _rms(x, g):
    return x * lax.rsqrt(jnp.mean(x * x, axis=-1, keepdims=True) + RMS_EPS) * g


def _mla_prep_kernel(cq_ref, ckv_ref, kr1_ref, kr2_ref, cos_ref, sin_ref, gq_ref, gkv_ref,
                     wq1_ref, wq2_ref, wk_ref, wv_ref, q_ref, k_ref, v_ref):
    cqn = _rms(cq_ref[...], gq_ref[...]).astype(BF16)
    c = _rms(ckv_ref[...], gkv_ref[...]).astype(BF16)
    cos, sin = cos_ref[...], sin_ref[...]
    k_rope = kr1_ref[...] * cos + kr2_ref[...] * sin
    scale = (D_NOPE + D_ROPE) ** -0.5 * LOG2E
    for h in range(D_HEADS):
        cols = slice(h * LANES, (h + 1) * LANES)
        q = _dot(cqn, wq1_ref[:, cols]) * cos + _dot(cqn, wq2_ref[:, cols]) * sin
        q_ref[:, cols] = (q * scale).astype(q_ref.dtype)
        k_ref[:, cols] = (_dot(c, wk_ref[:, cols]) + k_rope).astype(k_ref.dtype)
    v_ref[...] = _dot(c, wv_ref[...]).astype(v_ref.dtype)


def _mla_prep(hf, cos, sin, gq, gkv, wq1, wq2, wk, wv, seq):
    m = hf.shape[0]
    n_seq = seq // TM
    const = lambda shape: pl.BlockSpec(shape, lambda i: (0, 0))
    hq = D_HEADS * LANES
    return pl.pallas_call(
        _mla_prep_kernel,
        grid=(m // TM,),
        in_specs=[pl.BlockSpec((TM, D_Q_LORA), lambda i: (i, 1)),
                  pl.BlockSpec((TM, D_KV_LORA), lambda i: (i, 3)),
                  pl.BlockSpec((TM, LANES), lambda i: (i, 8)),
                  pl.BlockSpec((TM, LANES), lambda i: (i, 9)),
                  pl.BlockSpec((TM, LANES), lambda i: (i % n_seq, 0)),
                  pl.BlockSpec((TM, LANES), lambda i: (i % n_seq, 0)),
                  const((1, D_Q_LORA)), const((1, D_KV_LORA)),
                  const(wq1.shape), const(wq2.shape), const(wk.shape), const(wv.shape)],
        out_specs=[pl.BlockSpec((TM, hq), lambda i: (i, 0)), pl.BlockSpec((TM, hq), lambda i: (i, 0)),
                   pl.BlockSpec((TM, D_HEADS * D_V), lambda i: (i, 0))],
        out_shape=[jax.ShapeDtypeStruct((m, hq), BF16), jax.ShapeDtypeStruct((m, hq), BF16),
                   jax.ShapeDtypeStruct((m, D_HEADS * D_V), BF16)],
        compiler_params=_cparams(("arbitrary",)),
        name="mla_prep",
    )(hf, hf, hf, hf, cos, sin, gq.reshape(1, -1), gkv.reshape(1, -1), wq1, wq2, wk, wv)


def _mla_kernel(qa_ref, qb_ref, k_ref, v_ref, oa_ref, ob_ref, vt_ref, *, n_d, n_heads):
    t = T_ATT
    pi = pl.program_id(2)

    @pl.when(pi == 0)
    def _():
        _fill_transposed(vt_ref, v_ref, D_V)

    head_cols = [slice(h * LANES, (h + 1) * LANES) for h in range(n_heads)]
    key_minus_query = (lax.broadcasted_iota(jnp.int32, (2 * t, t), 0)
                       - lax.broadcasted_iota(jnp.int32, (2 * t, t), 1))

    def score(n, q_tile, i, s, diag):
        return jnp.where(key_minus_query <= (q_tile - 2 * i) * t, s, NEG_INF) if diag else s

    out_a, out_b = _flash_balanced(pi, n_d, [qa_ref[0, :, c] for c in head_cols], [qb_ref[0, :, c] for c in head_cols],
                                   [k_ref] * n_heads, head_cols, vt_ref,
                                   [_value_rows(h, D_V) for h in range(n_heads)], score)
    oa_ref[0] = jnp.concatenate(out_a, axis=0).T.astype(oa_ref.dtype)
    ob_ref[0] = jnp.concatenate(out_b, axis=0).T.astype(ob_ref.dtype)


def _mla_attention(q3, k3, v3):
    b, s, _ = q3.shape
    t = T_ATT
    n_d = s // t
    n_p = n_d // 2
    nh = MLA_HEADS_PER_STEP
    o_a, o_b = pl.pallas_call(
        functools.partial(_mla_kernel, n_d=n_d, n_heads=nh),
        grid=(D_HEADS // nh, b, n_p),
        in_specs=[pl.BlockSpec((1, t, nh * LANES), lambda hp, bi, pi: (bi, pi, hp)),
                  pl.BlockSpec((1, t, nh * LANES), lambda hp, bi, pi: (bi, n_d - 1 - pi, hp)),
                  pl.BlockSpec((1, s, nh * LANES), lambda hp, bi, pi: (bi, 0, hp)),
                  pl.BlockSpec((1, s, nh * D_V), lambda hp, bi, pi: (bi, 0, hp))],
        out_specs=[pl.BlockSpec((1, t, nh * D_V), lambda hp, bi, pi: (bi, pi, hp)),
                   pl.BlockSpec((1, t, nh * D_V), lambda hp, bi, pi: (bi, n_p - 1 - pi, hp))],
        out_shape=[jax.ShapeDtypeStruct((b, s // 2, D_HEADS * D_V), BF16)] * 2,
        scratch_shapes=[pltpu.VMEM((nh * (D_V + ONES_ROWS), s), BF16)],
        compiler_params=_cparams(("arbitrary", "arbitrary", "arbitrary")),
        name="mla_attention",
    )(q3, q3, k3, v3)
    return _join_halves(o_a, o_b)


def _odd_out_kernel(gate_ref, oc_ref, os_ref, ow_ref, od_ref, x_ref, w_ref, g_ref, b_ref, out_ref):
    gates = 1.0 / (1.0 + jnp.exp(-gate_ref[...]))
    lo = _lane_lo(TM)
    n_c = C_HEADS * C_DIM
    y = _dot(od_ref[...], w_ref[n_c:, :])
    for blk in range(n_c // LANES):
        cols = slice(blk * LANES, (blk + 1) * LANES)
        acc = jnp.zeros((TM, LANES), F32)
        for br, o_ref in enumerate((oc_ref, os_ref, ow_ref)):
            c0 = br * C_HEADS + 2 * blk
            gexp = jnp.where(lo, gates[:, c0:c0 + 1], gates[:, c0 + 1:c0 + 2])
            acc = acc + gexp * o_ref[:, cols]
        y = y + _dot(acc.astype(BF16), w_ref[cols, :])
    out_ref[...] = _residual_ln(x_ref[...], y, g_ref[...], b_ref[...])


def _odd_out(hf, o_cmp, o_slc, o_win, o_d, x, w_out, g, beta):
    m = x.shape[0]
    n_c = C_HEADS * C_DIM
    row = lambda width: pl.BlockSpec((TM, width), lambda i: (i, 0))
    const = lambda shape: pl.BlockSpec(shape, lambda i: (0, 0))
    return pl.pallas_call(
        _odd_out_kernel,
        grid=(m // TM,),
        in_specs=[pl.BlockSpec((TM, LANES), lambda i: (i, 2)),
                  row(n_c), row(n_c), row(n_c), row(D_HEADS * D_V), row(D_MODEL),
                  const(w_out.shape), const((1, D_MODEL)), const((1, D_MODEL))],
        out_specs=row(D_MODEL),
        out_shape=jax.ShapeDtypeStruct((m, D_MODEL), F32),
        compiler_params=_cparams(("arbitrary",)),
        name="odd_out",
    )(hf, o_cmp, o_slc, o_win, o_d, x, w_out, g.reshape(1, -1), beta.reshape(1, -1))


ODD_BF16_W = 1536
ODD_F32_W = 1280


def _odd_in_columns():
    q_c = C_HEADS * C_DIM
    kv = lambda br, which, g: q_c + ((br * 2 + which) * C_KV_GROUPS + g) * C_DIM + np.arange(C_DIM)
    gate0 = q_c + 3 * 2 * C_KV_GROUPS * C_DIM
    cq0 = gate0 + 3 * C_HEADS
    ckv0 = cq0 + D_Q_LORA
    kr0 = ckv0 + D_KV_LORA
    zeros = lambda n: np.full(n, -1)
    cols = [np.arange(q_c)]
    for br in (1, 2):
        for which in (0, 1):
            for g in range(C_KV_GROUPS):
                cols += [kv(br, which, g), kv(br, which, g)]
    assert sum(len(c) for c in cols) == ODD_BF16_W
    for which in (0, 1):
        cols += [kv(0, which, 0), kv(0, which, 1)]
    cols += [gate0 + np.arange(3 * C_HEADS), zeros(LANES - 3 * C_HEADS)]
    cols += [cq0 + np.arange(D_Q_LORA), ckv0 + np.arange(D_KV_LORA)]
    half = D_ROPE // 2
    kr = kr0 + np.arange(D_ROPE)
    cols += [zeros(D_NOPE), kr, zeros(LANES - D_NOPE - D_ROPE)]
    cols += [zeros(D_NOPE), kr[half:], kr[:half], zeros(LANES - D_NOPE - D_ROPE)]
    cols = np.concatenate(cols)
    assert len(cols) == ODD_BF16_W + ODD_F32_W
    return cols


def _gather_cols(w, cols):
    return jnp.where(jnp.asarray(cols >= 0)[None, :], w[:, np.maximum(cols, 0)], 0.0)


def _mla_weight_columns():
    dq = D_NOPE + D_ROPE
    half = D_ROPE // 2
    zeros = lambda n: np.full(n, -1)
    q1, q2, k1 = [], [], []
    for h in range(D_HEADS):
        rope0 = h * dq + D_NOPE
        q1 += [h * dq + np.arange(dq), zeros(LANES - dq)]
        q2 += [zeros(D_NOPE), rope0 + half + np.arange(half), rope0 + np.arange(half), zeros(LANES - dq)]
        k1 += [h * D_NOPE + np.arange(D_NOPE), zeros(LANES - D_NOPE)]
    return np.concatenate(q1), np.concatenate(q2), np.concatenate(k1)


def _rope_tables(seq):
    half = D_ROPE // 2
    inv = ROPE_THETA ** (-jnp.arange(half, dtype=F32) / half)
    ang = jnp.arange(seq).astype(F32)[:, None] * inv
    cos, sin = jnp.cos(ang), jnp.sin(ang)
    pad = jnp.zeros((seq, LANES - D_NOPE - D_ROPE), F32)
    cos_t = jnp.concatenate([jnp.ones((seq, D_NOPE), F32), cos, cos, pad], axis=1)
    sin_t = jnp.concatenate([jnp.zeros((seq, D_NOPE), F32), -sin, sin, pad], axis=1)
    return cos_t, sin_t


def kernel(x, rel_bias, ev_w_in, ev_w_out, ev_lambda, ev_subln, od_w_in, od_w_out, od_cmp_pe, od_cmp_w1, od_cmp_w2, od_q_norm, od_kv_norm, od_w_uq, od_w_uk, od_w_uv, ffn_w_up, ffn_conv_w, ffn_conv_b, ffn_w_down, ln_g, ln_b):
    b, s, _ = x.shape
    m = b * s
    assert s % 1024 == 0 and s // B_PAIRS[-1][1] >= 2 * T_DIL and s // SLC_BLOCK <= LANES
    n_d = s // T_ATT

    tb_t = rel_bias.T.astype(F32)
    tb_log2 = tb_t[:C_HEADS] * LOG2E
    n_prev_win = -(-(WIN_SIZE - 1) // T_ATT)
    tb_causal = _bias_table(tb_log2, _toeplitz_idx(1, n_d, T_ATT, s))
    tb_win = _bias_table(tb_log2, _toeplitz_idx(n_prev_win, n_prev_win + 1, T_ATT, WIN_SIZE - 1))
    tb_cmp = _bias_table(tb_log2, _cmp_idx(s)).reshape(C_HEADS, s, s // CMP_STRIDE)
    tb_dil = [_bias_table(tb_t[A_HEADS + i * B_HEADS:A_HEADS + (i + 1) * B_HEADS], _dilated_idx(d))
              for i, (_, d) in enumerate(B_PAIRS)]
    cos_t, sin_t = _rope_tables(s)
    odd_cols = _odd_in_columns()
    q1_cols, q2_cols, k1_cols = _mla_weight_columns()

    xf = x.reshape(m, D_MODEL)
    for l in range(DEPTH):
        i = l // 2
        if l % 2 == 0:
            n_a = EVEN_IN - EVEN_DILATED
            w_in = ev_w_in[i] * jnp.where(jnp.arange(EVEN_IN) < 2 * A_HEADS * A_QK, LOG2E, 1.0)
            h_a, h_b = _inproj(xf, w_in.astype(BF16), (n_a, EVEN_DILATED), (BF16, F32), 768)
            h3 = h_a.reshape(b, s, n_a)
            hd3 = h_b.reshape(b, s, EVEN_DILATED)
            lam_init = 0.8 - 0.6 * math.exp(-0.3 * l)
            o_a = _diff_attention(h3, tb_causal, ev_lambda[i].astype(F32), ev_subln[i], lam_init)
            obs, lses = [], []
            for p_idx, (_, d) in enumerate(B_PAIRS):
                o, lse = _dilated_attention(hd3, tb_dil[p_idx], p_idx, d)
                obs.append(o)
                lses.append(lse)
            xf = _even_out(o_a.reshape(m, -1), obs, lses, xf, ev_w_out[i].astype(BF16), ln_g[l, 0], ln_b[l, 0])
        else:
            w_in = _gather_cols(od_w_in[i], odd_cols) * jnp.where(jnp.arange(len(odd_cols)) < C_HEADS * C_DIM, LOG2E, 1.0)
            hb, hf = _inproj(xf, w_in.astype(BF16), (ODD_BF16_W, ODD_F32_W), (BF16, F32), 256)
            hb3 = hb.reshape(b, s, ODD_BF16_W)
            n_ch = s // CMP_STRIDE
            chunks = (hf[:, :4 * C_DIM].reshape(b, n_ch, CMP_STRIDE, 4, C_DIM)
                      .transpose(0, 3, 1, 2, 4).reshape(b, 4, n_ch, CMP_STRIDE * C_DIM))
            pe = od_cmp_pe[i].reshape(2, 2, CMP_STRIDE * C_DIM)
            w2d = jnp.concatenate([od_cmp_w2[i], od_cmp_w2[i]], axis=-1).astype(BF16)
            kvc = _compress(chunks, pe, od_cmp_w1[i].astype(BF16), w2d)
            o_cmp, sel = _cmp_attention(hb3, kvc, tb_cmp)
            o_slc = _slc_attention(hb3, tb_causal, 4, 6, sel)
            o_win = _win_attention(hb3, tb_win, 8, 10, n_prev_win)
            q_d, k_d, v_d = _mla_prep(hf, cos_t, sin_t, od_q_norm[i], od_kv_norm[i],
                                      _gather_cols(od_w_uq[i], q1_cols).astype(BF16),
                                      _gather_cols(od_w_uq[i], q2_cols).astype(BF16),
                                      _gather_cols(od_w_uk[i], k1_cols).astype(BF16),
                                      od_w_uv[i].astype(BF16), s)
            o_d = _mla_attention(q_d.reshape(b, s, -1), k_d.reshape(b, s, -1), v_d.reshape(b, s, -1))
            xf = _odd_out(hf, o_cmp.reshape(m, -1), o_slc.reshape(m, -1), o_win.reshape(m, -1),
                          o_d.reshape(m, -1), xf, od_w_out[i].astype(BF16), ln_g[l, 0], ln_b[l, 0])
        xf = _ffn(xf, ffn_w_up[l].astype(BF16), ffn_conv_w[l], ffn_conv_b[l], ffn_w_down[l].astype(BF16),
                  ln_g[l, 1], ln_b[l, 1], s)
    return xf.reshape(b, s, D_MODEL)
```

```python
import functools
import math

import numpy as np
import jax
import jax.numpy as jnp
from jax import lax
from jax.experimental import pallas as pl
from jax.experimental.pallas import tpu as pltpu

F32 = jnp.float32
BF16 = jnp.bfloat16

D_MODEL = 1024
DEPTH = 4
LN_EPS = 1e-5
RMS_EPS = 1e-5
N_BUCKETS = 32
MAX_DISTANCE = 2048
A_HEADS, A_QK, A_V = 4, 64, 128
B_PAIRS = ((128, 1), (512, 4), (2048, 16))
B_HEADS, B_DIM = 4, 64
C_HEADS, C_KV_GROUPS, C_DIM = 8, 2, 64
CMP_LEN, CMP_STRIDE, CMP_HIDDEN = 32, 16, 256
SLC_BLOCK, SLC_TOP_N = 64, 16
SLC_SHIFT = 6
WIN_SIZE = 512
D_HEADS, D_Q_LORA, D_KV_LORA, D_NOPE, D_ROPE, D_V = 8, 384, 256, 64, 32, 64
ROPE_THETA = 10000.0
D_FF = 2816
EVEN_DILATED = len(B_PAIRS) * 3 * B_HEADS * B_DIM
EVEN_IN = 2 * A_HEADS * A_QK * 2 + A_HEADS * A_V + EVEN_DILATED
ALPHA = (2 * DEPTH) ** 0.25

LANES = 128
HALF_LANE_SHIFT = 6
VMEM_LIMIT = 56 * 1024 * 1024
T_ATT = 256
MLA_HEADS_PER_STEP = 4
T_DIL = 128
TM = 512
FF_CHUNK = 256
NEG_INF = float("-inf")
M_INIT = -1e30
SEL_PENALTY = 2.0 ** 101
LOG2E = math.log2(math.e)
ONES_ROWS = 16


def _cparams(sem, flags=None):
    return pltpu.CompilerParams(dimension_semantics=sem, vmem_limit_bytes=VMEM_LIMIT, flags=flags)


def _dot(a, b):
    return jnp.dot(a, b, preferred_element_type=F32)


def _dot_nt(a, b):
    return lax.dot_general(a, b, (((1,), (1,)), ((), ())), preferred_element_type=F32)


def _bucket_np(dist):
    n = np.maximum(dist, 0)
    nf = np.maximum(n, 1).astype(np.float32)
    max_exact = N_BUCKETS // 2
    large = max_exact + (np.log(nf / max_exact) / math.log(MAX_DISTANCE / max_exact)
                         * (N_BUCKETS - max_exact)).astype(np.int32)
    return np.where(n < max_exact, n, np.minimum(large, N_BUCKETS - 1)).astype(np.int32)


def _toeplitz_idx(n_masked, n_delta, t, max_dist):
    key = np.arange(t)[:, None]
    query = np.arange(t)[None, :]
    out = []
    for delta in range(-n_masked, n_delta):
        dist = t * delta + query - key
        ok = (dist >= 0) & (dist <= max_dist)
        out.append(np.where(ok, _bucket_np(dist), -1))
    return np.stack(out).astype(np.int32)


def _dilated_idx(d):
    t = T_DIL
    r = np.arange(t)[:, None]
    c = np.arange(2 * t)[None, :]
    dist0 = r - c
    ok0 = (dist0 >= 0) & (c < t)
    dist1 = t + r - c
    ok1 = (dist1 >= 0) & (dist1 <= t)
    return np.stack([np.where(ok0, _bucket_np(dist0 * d), -1),
                     np.where(ok1, _bucket_np(dist1 * d), -1)]).astype(np.int32)


def _cmp_idx(seq):
    ncb = seq // CMP_STRIDE - 1
    q = np.arange(seq)[:, None]
    c = np.arange(seq // CMP_STRIDE)[None, :]
    dist = q - (c * CMP_STRIDE + CMP_LEN - 1)
    ok = (dist >= 0) & (c < ncb)
    return np.where(ok, _bucket_np(dist), -1).astype(np.int32).reshape(seq // T_ATT, T_ATT, seq // CMP_STRIDE)


def _bias_kernel(tbl_ref, idx_ref, o_ref):
    idx = idx_ref[0]
    rows = idx.shape[0]
    row = jnp.broadcast_to(tbl_ref[0], (rows, LANES))
    for c0 in range(0, idx.shape[1], LANES):
        ix = idx[:, c0:c0 + LANES]
        val = jnp.take_along_axis(row, jnp.maximum(ix, 0), axis=1)
        o_ref[0, 0, :, c0:c0 + LANES] = jnp.where(ix < 0, NEG_INF, val)


def _bias_table(tbl, idx):
    n_h = tbl.shape[0]
    n, r, c = idx.shape
    tbl_rows = jnp.pad(tbl, ((0, 0), (0, LANES - N_BUCKETS))).reshape(n_h, 1, LANES)
    return pl.pallas_call(
        _bias_kernel,
        grid=(n_h, n),
        in_specs=[pl.BlockSpec((1, 1, LANES), lambda h, i: (h, 0, 0)),
                  pl.BlockSpec((1, r, c), lambda h, i: (i, 0, 0))],
        out_specs=pl.BlockSpec((1, 1, r, c), lambda h, i: (h, i, 0, 0)),
        out_shape=jax.ShapeDtypeStruct((n_h, n, r, c), F32),
        compiler_params=_cparams(("arbitrary", "arbitrary")),
        name="bias_table",
    )(tbl_rows, jnp.asarray(idx))


def _inproj_kernel(x_ref, w_ref, *o_refs, widths, chunk):
    xb = x_ref[...].astype(BF16)
    off = 0
    for o_ref, width in zip(o_refs, widths):
        for c0 in range(0, width, chunk):
            o_ref[:, c0:c0 + chunk] = _dot(xb, w_ref[:, off + c0:off + c0 + chunk]).astype(o_ref.dtype)
        off += width


def _inproj(x, w, widths, dtypes, chunk):
    m = x.shape[0]
    n = w.shape[1]
    assert sum(widths) == n and all(wd % chunk == 0 for wd in widths)
    return pl.pallas_call(
        functools.partial(_inproj_kernel, widths=widths, chunk=chunk),
        grid=(m // TM,),
        in_specs=[pl.BlockSpec((TM, D_MODEL), lambda i: (i, 0)),
                  pl.BlockSpec((D_MODEL, n), lambda i: (0, 0))],
        out_specs=[pl.BlockSpec((TM, wd), lambda i: (i, 0)) for wd in widths],
        out_shape=[jax.ShapeDtypeStruct((m, wd), dt) for wd, dt in zip(widths, dtypes)],
        compiler_params=_cparams(("arbitrary",)),
        name="in_proj",
    )(x, w)


def _tree_reduce_rows(x, combine, reduce_fn):
    parts = [x[r:r + 8] for r in range(0, x.shape[0], 8)]
    while len(parts) > 1:
        parts = [combine(parts[k], parts[k + 1]) for k in range(0, len(parts) - 1, 2)] + \
                ([parts[-1]] if len(parts) % 2 else [])
    return reduce_fn(parts[0], axis=0, keepdims=True)


def _flash_init(streams):
    return tuple((jnp.full((1, q.shape[0]), M_INIT, F32), jnp.zeros((vt_rows.stop - vt_rows.start, q.shape[0]), F32))
                 for q, _, _, _, vt_rows in streams)


def _normalise(acc):
    d_v = acc.shape[0] - ONES_ROWS
    return acc[:d_v] / acc[d_v:d_v + 1]


def _flash_scores(streams, key_start, n_keys):
    keys = pl.ds(pl.multiple_of(key_start, T_ATT), n_keys)
    return tuple(_dot_nt(k_ref[0, keys, k_cols], q) for q, k_ref, k_cols, _, _ in streams)


def _flash_update(streams, score_fns, key_start, n_keys, i, raw, state):
    keys = pl.ds(pl.multiple_of(key_start, T_ATT), n_keys)
    scores = [fn(i, s) for fn, s in zip(score_fns, raw)]
    m_new = [jnp.maximum(m, _tree_reduce_rows(s, jnp.maximum, jnp.max)) for s, (m, _) in zip(scores, state)]
    probs = [jnp.exp2(s - mn) for s, mn in zip(scores, m_new)]
    out = []
    for (_, _, _, vt_ref, vt_rows), p, mn, (m, acc) in zip(streams, probs, m_new, state):
        acc = jnp.exp2(m - mn) * acc + _dot(vt_ref[vt_rows, keys], p.astype(BF16))
        out.append((mn, acc))
    return tuple(out)


def _flash_finish(state):
    return [_normalise(acc) for _, acc in state]


def _flash_balanced(pi, n_d, qa, qb, k_refs, k_cols, vt_ref, vt_rows, score, key_extra=None):
    kt = 2 * T_ATT
    t_q = qa[0].shape[0]
    n_streams = len(qa)
    n_slots = n_d // 2 + 1
    max_a = n_d // 4
    n_a = pi // 2 + 1
    tile_a, tile_b = pi, n_d - 1 - pi

    def slot(s):
        if s == 0:
            return True, tile_a, n_a - 1
        if s >= max_a:
            return False, tile_b, s - n_a
        is_a = s < n_a
        return is_a, jnp.where(is_a, tile_a, tile_b), jnp.where(is_a, n_a - 1 - s, s - n_a)

    def pick(is_a, a, b):
        return (a if is_a else b) if isinstance(is_a, bool) else jnp.where(is_a, a, b)

    def keys_of(i):
        return pl.ds(pl.multiple_of(i * kt, kt), kt)

    def raw_scores(s):
        is_a, _, i = slot(s)
        extra = None if key_extra is None else key_extra(i)
        out = []
        for n in range(n_streams):
            keys = k_refs[n][0, keys_of(i), k_cols[n]]
            if extra is not None:
                keys = jnp.concatenate([keys, extra], axis=1)
            out.append(_dot_nt(keys, pick(is_a, qa[n], qb[n])))
        return out

    state = [(jnp.full((1, t_q), M_INIT, F32), jnp.zeros((vt_rows[n].stop - vt_rows[n].start, t_q), F32))
             for n in range(n_streams)]
    out_a = [jnp.zeros((acc.shape[0] - ONES_ROWS, t_q), F32) for _, acc in state]
    pending = None
    raw = raw_scores(0)

    def fold(state, pending):
        alphas, probs, i_prev = pending
        return [(m, alphas[n] * acc + _dot(vt_ref[vt_rows[n], keys_of(i_prev)], probs[n]))
                for n, (m, acc) in enumerate(state)]

    for s in range(n_slots):
        raw_next = raw_scores(s + 1) if s + 1 < n_slots else None
        is_a, q_tile, i = slot(s)
        if pending is not None:
            state = fold(state, pending)
        if 1 <= s <= max_a:
            switch = s == n_a
            out_a = [jnp.where(switch, _normalise(acc), o) for (_, acc), o in zip(state, out_a)]
            state = [(jnp.where(switch, M_INIT, m), jnp.where(switch, 0.0, acc)) for m, acc in state]
        diag = s == 0 or s == n_slots - 1
        scores = [score(n, q_tile, i, raw[n], diag) for n in range(n_streams)]
        m_new = [jnp.maximum(m, _tree_reduce_rows(sc, jnp.maximum, jnp.max)) for sc, (m, _) in zip(scores, state)]
        probs = [jnp.exp2(sc - mn).astype(BF16) for sc, mn in zip(scores, m_new)]
        alphas = [jnp.exp2(m - mn) for (m, _), mn in zip(state, m_new)]
        state = [(mn, acc) for mn, (_, acc) in zip(m_new, state)]
        pending = (alphas, probs, i)
        raw = raw_next
    state = fold(state, pending)
    return out_a, [_normalise(acc) for _, acc in state]


def _pair_bias(tb_ref, h, q_tile, i, s):
    t = T_ATT
    d0 = q_tile - 2 * i
    return jnp.concatenate([s[:t] + tb_ref[h, d0 + 1], s[t:] + tb_ref[h, d0]], axis=0)


def _fill_transposed(vt_ref, v_ref, d_v):
    n = v_ref.shape[1]
    n_heads = vt_ref.shape[0] // (d_v + ONES_ROWS)
    step = 2 * T_ATT
    for c0 in range(0, n, step):
        v_t = v_ref[0, c0:c0 + step, :].astype(F32).T.astype(BF16)
        for h in range(n_heads):
            r0 = h * (d_v + ONES_ROWS)
            vt_ref[r0:r0 + d_v, c0:c0 + step] = v_t[h * d_v:(h + 1) * d_v]
            vt_ref[r0 + d_v:r0 + d_v + ONES_ROWS, c0:c0 + step] = jnp.ones((ONES_ROWS, step), BF16)


def _value_rows(h, d_v):
    return slice(h * (d_v + ONES_ROWS), (h + 1) * (d_v + ONES_ROWS))


def _lane_lo(rows):
    return lax.broadcasted_iota(jnp.int32, (rows, LANES), 1) < (LANES // 2)


def _diff_kernel(lam_ref, g_ref, q0a_ref, q0b_ref, q1a_ref, q1b_ref, k0_ref, k1_ref, v_ref, tb_ref,
                 oa_ref, ob_ref, vt_ref, *, lam_init, n_d):
    pi = pl.program_id(2)

    @pl.when(pi == 0)
    def _():
        _fill_transposed(vt_ref, v_ref, A_V)

    lp = lam_ref[...]
    lam = (jnp.exp(jnp.sum(lp[0:1] * lp[1:2], axis=-1, keepdims=True))
           - jnp.exp(jnp.sum(lp[2:3] * lp[3:4], axis=-1, keepdims=True)) + lam_init)
    lo = _lane_lo(T_ATT)

    def prep(q_ref):
        q = q_ref[0] * (A_QK ** -0.5)
        zero = jnp.zeros_like(q)
        return [jnp.where(lo, q, zero), jnp.where(lo, zero, q)]

    full = slice(0, LANES)

    def score(n, q_tile, i, s, diag):
        return _pair_bias(tb_ref, n % 2, q_tile, i, s)

    out_a, out_b = _flash_balanced(pi, n_d, prep(q0a_ref) + prep(q1a_ref), prep(q0b_ref) + prep(q1b_ref),
                                   [k0_ref, k0_ref, k1_ref, k1_ref], [full] * 4, vt_ref,
                                   [_value_rows(0, A_V), _value_rows(1, A_V)] * 2, score)
    for outs, o_ref in ((out_a, oa_ref), (out_b, ob_ref)):
        for head in range(2):
            d = (outs[head] - lam * outs[2 + head]).T
            ms = jnp.mean(d * d, axis=-1, keepdims=True)
            o_ref[0, :, head * A_V:(head + 1) * A_V] = (d * lax.rsqrt(ms + RMS_EPS) * g_ref[...]
                                                        * (1.0 - lam_init)).astype(o_ref.dtype)


def _join_halves(o_a, o_b):
    return o_a.reshape(-1, o_a.shape[-1]), o_b.reshape(-1, o_b.shape[-1])


def _diff_attention(h3, tb, lam_p, subln_g, lam_init):
    b, s, _ = h3.shape
    t = T_ATT
    n_d = s // t
    n_p = n_d // 2
    q_spec = lambda col0, second: pl.BlockSpec(
        (1, t, LANES), lambda hp, bi, pi: (bi, (n_d - 1 - pi) if second else pi, col0 + hp))
    k_spec = lambda col0: pl.BlockSpec((1, s, LANES), lambda hp, bi, pi: (bi, 0, col0 + hp))
    o_a, o_b = pl.pallas_call(
        functools.partial(_diff_kernel, lam_init=lam_init, n_d=n_d),
        grid=(A_HEADS // 2, b, n_p),
        in_specs=[pl.BlockSpec((4, A_QK), lambda hp, bi, pi: (0, 0)),
                  pl.BlockSpec((1, A_V), lambda hp, bi, pi: (0, 0)),
                  q_spec(0, False), q_spec(0, True), q_spec(2, False), q_spec(2, True),
                  k_spec(4), k_spec(6),
                  pl.BlockSpec((1, s, 2 * A_V), lambda hp, bi, pi: (bi, 0, 4 + hp)),
                  pl.BlockSpec((2, n_d + 1, t, t), lambda hp, bi, pi: (hp, 0, 0, 0))],
        out_specs=[pl.BlockSpec((1, t, 2 * A_V), lambda hp, bi, pi: (bi, pi, hp)),
                   pl.BlockSpec((1, t, 2 * A_V), lambda hp, bi, pi: (bi, n_p - 1 - pi, hp))],
        out_shape=[jax.ShapeDtypeStruct((b, s // 2, A_HEADS * A_V), BF16)] * 2,
        scratch_shapes=[pltpu.VMEM((2 * (A_V + ONES_ROWS), s), BF16)],
        compiler_params=_cparams(("arbitrary", "arbitrary", "arbitrary")),
        name="diff_attention",
    )(lam_p, subln_g.reshape(1, A_V), h3, h3, h3, h3, h3, h3, h3, tb)
    return _join_halves(o_a, o_b)


def _dilated_kernel(q_ref, k_ref, v_ref, tb_ref, o_ref, lse_ref, *, d, residues):
    t = T_DIL
    n_blocks = q_ref.shape[1] // (d * t)
    lo = _lane_lo(t)
    scale = B_DIM ** -0.5

    def block(idx, carry):
        r = pl.program_id(2) * residues + idx // n_blocks
        bi = idx % n_blocks
        var = jnp.minimum(bi, 1)
        qrows = pl.ds(r + d * t * bi, t, stride=d)
        krows = pl.ds(r + d * t * jnp.maximum(bi - 1, 0), 2 * t, stride=d)
        q = (q_ref[0, qrows, :] * scale).astype(BF16)
        kk = k_ref[0, krows, :].astype(BF16)
        vv = v_ref[0, krows, :].astype(BF16)
        o_half, l_half = [], []
        for half in range(2):
            keep = lo if half == 0 else jnp.logical_not(lo)
            qm = jnp.where(keep, q, jnp.zeros_like(q))
            s = _dot_nt(qm, kk) + tb_ref[half, var]
            m = jnp.max(s, axis=-1, keepdims=True)
            p = jnp.exp(s - m)
            l = jnp.sum(p, axis=-1, keepdims=True)
            o_half.append(_dot(p.astype(BF16), vv) / l)
            l_half.append(m + jnp.log(l))
        o_ref[0, qrows, :] = jnp.where(lo, o_half[0], o_half[1])
        lse_ref[0, qrows, :] = jnp.where(lo, l_half[0], l_half[1])
        return carry

    lax.fori_loop(0, residues * n_blocks, block, 0, unroll=8)


def _dilated_attention(hd3, tb, pair_idx, d):
    b, s, _ = hd3.shape
    hd = B_HEADS * B_DIM
    base = pair_idx * 3 * (hd // LANES)
    seq = lambda which: pl.BlockSpec((1, s, LANES), lambda bi, hp, r: (bi, 0, base + which * (hd // LANES) + hp))
    out = pl.BlockSpec((1, s, LANES), lambda bi, hp, r: (bi, 0, hp))
    n_blocks = s // (d * T_DIL)
    residues = min(d, max(1, 8 // n_blocks))
    assert d % residues == 0 and (residues * n_blocks) % 4 == 0
    o, lse = pl.pallas_call(
        functools.partial(_dilated_kernel, d=d, residues=residues),
        grid=(b, hd // LANES, d // residues),
        in_specs=[seq(0), seq(1), seq(2),
                  pl.BlockSpec((2, 2, T_DIL, 2 * T_DIL), lambda bi, hp, r: (hp, 0, 0, 0))],
        out_specs=[out, out],
        out_shape=[jax.ShapeDtypeStruct((b, s, hd), F32)] * 2,
        compiler_params=_cparams(("arbitrary", "arbitrary", "arbitrary")),
        name="dilated_attention",
    )(hd3, hd3, hd3, tb)
    return o.reshape(b * s, hd), lse.reshape(b * s, hd)


def _residual_ln(x, y, g, beta):
    z = ALPHA * x + y
    mu = jnp.mean(z, axis=-1, keepdims=True)
    zc = z - mu
    var = jnp.mean(zc * zc, axis=-1, keepdims=True)
    return zc * lax.rsqrt(var + LN_EPS) * g + beta


def _half_specs(width, tiles_per_seq):
    half = tiles_per_seq // 2
    first = pl.BlockSpec((TM, width), lambda i: ((i // tiles_per_seq) * half + jnp.minimum(i % tiles_per_seq, half - 1), 0))
    second = pl.BlockSpec((TM, width), lambda i: ((i // tiles_per_seq) * half + jnp.maximum(i % tiles_per_seq - half, 0), 0))
    return [first, second]


def _pick_half(first_ref, second_ref, tiles_per_seq):
    in_first = pl.program_id(0) % tiles_per_seq < tiles_per_seq // 2
    return jnp.where(in_first, first_ref[...], second_ref[...])


def _even_out_kernel(oa1_ref, oa2_ref, o0_ref, o1_ref, o2_ref, l0_ref, l1_ref, l2_ref, x_ref, w_ref, g_ref, b_ref,
                     out_ref, *, tiles_per_seq):
    l0, l1, l2 = l0_ref[...], l1_ref[...], l2_ref[...]
    mx = jnp.maximum(jnp.maximum(l0, l1), l2)
    e0, e1, e2 = jnp.exp(l0 - mx), jnp.exp(l1 - mx), jnp.exp(l2 - mx)
    den = e0 + e1 + e2
    ob = (e0 / den) * o0_ref[...] + (e1 / den) * o1_ref[...] + (e2 / den) * o2_ref[...]
    n_a = A_HEADS * A_V
    oa = _pick_half(oa1_ref, oa2_ref, tiles_per_seq)
    y = _dot(oa, w_ref[0:n_a, :]) + _dot(ob.astype(BF16), w_ref[n_a:, :])
    out_ref[...] = _residual_ln(x_ref[...], y, g_ref[...], b_ref[...])


def _even_out(oa_halves, obs, lses, x, w_out, g, beta, seq):
    m = x.shape[0]
    hd = B_HEADS * B_DIM
    tiles_per_seq = seq // TM
    row = lambda width: pl.BlockSpec((TM, width), lambda i: (i, 0))
    const = lambda shape: pl.BlockSpec(shape, lambda i: (0, 0))
    return pl.pallas_call(
        functools.partial(_even_out_kernel, tiles_per_seq=tiles_per_seq),
        grid=(m // TM,),
        in_specs=_half_specs(A_HEADS * A_V, tiles_per_seq) + [row(hd)] * 6 + [
            row(D_MODEL), const(w_out.shape), const((1, D_MODEL)), const((1, D_MODEL))],
        out_specs=row(D_MODEL),
        out_shape=jax.ShapeDtypeStruct((m, D_MODEL), F32),
        compiler_params=_cparams(("arbitrary",)),
        name="even_out",
    )(*oa_halves, *obs, *lses, x, w_out, g.reshape(1, -1), beta.reshape(1, -1))


def _gelu(x):
    return 0.5 * x * (1.0 + jnp.tanh(math.sqrt(2.0 / math.pi) * (x + 0.044715 * (x * x * x))))


def _ffn_kernel(x_ref, wu_ref, cw_ref, cb_ref, wd_ref, g_ref, b_ref, out_ref, u_ref, gs_ref, tail_ref,
                *, tiles_per_seq):
    halo = 8
    n_chunks = D_FF // FF_CHUNK

    @pl.when(pl.program_id(0) % tiles_per_seq == 0)
    def _():
        tail_ref[...] = jnp.zeros(tail_ref.shape, F32)

    x = x_ref[...]
    xb = x.astype(BF16)

    def up(c):
        cols = slice(c * FF_CHUNK, (c + 1) * FF_CHUNK)
        return _dot(xb, wu_ref[:, cols]), _dot(xb, wu_ref[:, D_FF + c * FF_CHUNK:D_FF + (c + 1) * FF_CHUNK])

    def activate(c, a, gate):
        cols = slice(c * FF_CHUNK, (c + 1) * FF_CHUNK)
        gs = gs_ref.at[c % 2]
        gs[0:halo, :] = tail_ref[:, cols]
        gs[halo:, :] = gate
        tail_ref[:, cols] = gate[TM - halo:, :]
        conv = (gs[pl.ds(halo - 2, TM), :] * cw_ref[0:1, cols] + gs[pl.ds(halo - 1, TM), :] * cw_ref[1:2, cols]
                + gate * cw_ref[2:3, cols] + cb_ref[:, cols])
        u_ref[:, cols] = (_gelu(conv) * a).astype(BF16)

    nxt = up(0)
    for c in range(n_chunks):
        cur = nxt
        if c + 1 < n_chunks:
            nxt = up(c + 1)
        activate(c, *cur)
    out_ref[...] = _residual_ln(x, _dot(u_ref[...], wd_ref[...]), g_ref[...], b_ref[...])


def _ffn(x, w_up, conv_w, conv_b, w_down, g, beta, seq):
    m = x.shape[0]
    const = lambda shape: pl.BlockSpec(shape, lambda i: (0, 0))
    return pl.pallas_call(
        functools.partial(_ffn_kernel, tiles_per_seq=seq // TM),
        grid=(m // TM,),
        in_specs=[pl.BlockSpec((TM, D_MODEL), lambda i: (i, 0)), const(w_up.shape), const(conv_w.shape),
                  const((1, D_FF)), const(w_down.shape), const((1, D_MODEL)), const((1, D_MODEL))],
        out_specs=pl.BlockSpec((TM, D_MODEL), lambda i: (i, 0)),
        out_shape=jax.ShapeDtypeStruct((m, D_MODEL), F32),
        scratch_shapes=[pltpu.VMEM((TM, D_FF), BF16), pltpu.VMEM((2, TM + 8, FF_CHUNK), F32),
                        pltpu.VMEM((8, D_FF), F32)],
        compiler_params=_cparams(("arbitrary",)),
        name="conv_ffn",
    )(x, w_up, conv_w, conv_b.reshape(1, -1), w_down, g.reshape(1, -1), beta.reshape(1, -1))


def _compress_kernel(ch_ref, pe_ref, w1_ref, w2_ref, o_ref):
    half = CMP_STRIDE * C_DIM
    ch = ch_ref[0, 0]
    a = _dot((ch + pe_ref[0, 0:1, :]).astype(BF16), w1_ref[0, 0:half, :])
    b = _dot((ch + pe_ref[0, 1:2, :]).astype(BF16), w1_ref[0, half:, :])
    n = ch.shape[0]
    hid = _gelu(a + pltpu.roll(b, n - 1, 0))
    o_ref[0, 0] = _dot(hid.astype(BF16), w2_ref[0]).astype(o_ref.dtype)


def _compress(chunks, pe, w1, w2d):
    b, _, n, width = chunks.shape
    return pl.pallas_call(
        _compress_kernel,
        grid=(b, 4),
        in_specs=[pl.BlockSpec((1, 1, n, width), lambda bi, j: (bi, j, 0, 0)),
                  pl.BlockSpec((1, 2, width), lambda bi, j: (j // 2, 0, 0)),
                  pl.BlockSpec((1, 2 * width, CMP_HIDDEN), lambda bi, j: (j // 2, 0, 0)),
                  pl.BlockSpec((1, CMP_HIDDEN, LANES), lambda bi, j: (j // 2, 0, 0))],
        out_specs=pl.BlockSpec((1, 1, n, LANES), lambda bi, j: (bi, j, 0, 0)),
        out_shape=jax.ShapeDtypeStruct((b, 4, n, LANES), BF16),
        compiler_params=_cparams(("arbitrary", "arbitrary")),
        name="nsa_compress",
    )(chunks, pe, w1, w2d)


def _cmp_kernel(q_ref, kc_ref, vc_ref, tb_ref, o_ref, sel_ref, pg_ref, sc_ref, pen_ref, *, tiles):
    t = T_ATT
    n_c = kc_ref.shape[2]
    n_sel = n_c // (SLC_BLOCK // CMP_STRIDE)
    lo = _lane_lo(t)
    kc = kc_ref[0, 0]
    vc = vc_ref[0, 0]
    raw = []
    for u in range(tiles):
        for pair in range(2):
            q = q_ref[0, u * t:(u + 1) * t, pair * LANES:(pair + 1) * LANES] * (C_DIM ** -0.5)
            zero = jnp.zeros_like(q)
            raw += [_dot_nt(jnp.where(lo, q, zero), kc), _dot_nt(jnp.where(lo, zero, q), kc)]
    probs = []
    for n, s in enumerate(raw):
        u, head = divmod(n, 4)
        s = s + tb_ref[head, u * t:(u + 1) * t, :]
        m = jnp.maximum(jnp.max(s, axis=-1, keepdims=True), M_INIT)
        p = jnp.exp2(s - m)
        den = jnp.sum(p, axis=-1, keepdims=True)
        probs.append(p / jnp.where(den > 0, den, 1.0))
    outs = [_dot(p.astype(BF16), vc) for p in probs]
    for u in range(tiles):
        for pair in range(2):
            o_ref[0, u * t:(u + 1) * t, pair * LANES:(pair + 1) * LANES] = jnp.where(
                lo, outs[4 * u + 2 * pair], outs[4 * u + 2 * pair + 1])
    for u in range(tiles):
        _select_blocks((probs[4 * u] + probs[4 * u + 1]) + (probs[4 * u + 2] + probs[4 * u + 3]),
                       pl.program_id(1) * tiles + u, sel_ref.at[0, 0, u * t:(u + 1) * t, :],
                       pg_ref.at[u], sc_ref.at[u], pen_ref.at[u], n_c, n_sel)


def _select_blocks(pg, qi, sel_out, pg_ref, sc_ref, pen_ref, n_c, n_sel):
    t = T_ATT
    pad = 8
    pg_t = pg.T
    for c in range(t // LANES):
        pg_ref[c, 0:pad, :] = jnp.zeros((pad, LANES), F32)
        pg_ref[c, pad:pad + n_c, :] = pg_t[:, c * LANES:(c + 1) * LANES]
        pg_ref[c, pad + n_c:, :] = jnp.zeros((pad, LANES), F32)
    r = SLC_BLOCK // CMP_STRIDE
    tap = lambda k: jnp.concatenate(
        [pg_ref[c, pl.ds(pad + k, n_sel, stride=r), :] for c in range(t // LANES)], axis=1)
    score = (0.5 * tap(-1) + ((tap(0) + tap(1)) + tap(2))) + 0.5 * tap(3)
    jb = lax.broadcasted_iota(jnp.int32, (n_sel, t), 0)
    qblk = jnp.right_shift(qi * t + lax.broadcasted_iota(jnp.int32, (n_sel, t), 1), SLC_SHIFT)
    forced = (jb == 0) | (jb == qblk) | (jb == qblk - 1)
    sc_ref[...] = jnp.where(forced, jnp.inf, jnp.where(jb <= qblk, score, NEG_INF))
    pen_ref[...] = jnp.zeros(pen_ref.shape, F32)
    groups = range(0, n_sel, 8)
    sc_g = [sc_ref[g0:g0 + 8, :] for g0 in groups]
    sub = lax.broadcasted_iota(jnp.int32, (8, t), 0)
    cnt_g = [jnp.zeros((8, t), jnp.int32) for _ in groups]
    for i in range(n_sel):
        row = jnp.broadcast_to(sc_ref[pl.ds(i, 1), :], (8, t))
        for k, g0 in enumerate(groups):
            if i < g0:
                beats = row >= sc_g[k]
            elif i >= g0 + 8:
                beats = row > sc_g[k]
            else:
                beats = (row > sc_g[k]) | ((row == sc_g[k]) & (i - g0 < sub))
            cnt_g[k] = cnt_g[k] + beats.astype(jnp.int32)
    for k, g0 in enumerate(groups):
        pen_ref[g0:g0 + 8, :] = jnp.where(cnt_g[k] < SLC_TOP_N, 0.0, -SEL_PENALTY)
    sel_out[...] = pen_ref[...].T.astype(sel_out.dtype)


def _cmp_attention(hb3, kvc, tb):
    b, s, _ = hb3.shape
    t = T_ATT
    n_c = s // CMP_STRIDE
    n_sel = s // SLC_BLOCK
    assert n_sel <= LANES
    gw = (C_HEADS // C_KV_GROUPS) * C_DIM
    tiles = 2
    rows = tiles * t
    return pl.pallas_call(
        functools.partial(_cmp_kernel, tiles=tiles),
        grid=(C_KV_GROUPS, s // rows, b),
        in_specs=[pl.BlockSpec((1, rows, gw), lambda g, qi, bi: (bi, qi, g)),
                  pl.BlockSpec((1, 1, n_c, LANES), lambda g, qi, bi: (bi, g, 0, 0)),
                  pl.BlockSpec((1, 1, n_c, LANES), lambda g, qi, bi: (bi, 2 + g, 0, 0)),
                  pl.BlockSpec((4, rows, n_c), lambda g, qi, bi: (g, qi, 0))],
        out_specs=[pl.BlockSpec((1, rows, gw), lambda g, qi, bi: (bi, qi, g)),
                   pl.BlockSpec((1, 1, rows, LANES), lambda g, qi, bi: (bi, g, qi, 0))],
        out_shape=[jax.ShapeDtypeStruct((b, s, C_HEADS * C_DIM), F32),
                   jax.ShapeDtypeStruct((b, C_KV_GROUPS, s, LANES), BF16)],
        scratch_shapes=[pltpu.VMEM((tiles, t // LANES, n_c + 16, LANES), F32), pltpu.VMEM((tiles, n_sel, t), F32),
                        pltpu.VMEM((tiles, LANES, t), F32)],
        compiler_params=_cparams(("arbitrary", "arbitrary", "arbitrary")),
        name="nsa_cmp_attention",
    )(hb3, kvc, kvc, tb)


def _head_pair_queries(q_ref):
    lo = _lane_lo(T_ATT)
    q = q_ref[0] * (C_DIM ** -0.5)
    zero = jnp.zeros_like(q)
    return [jnp.where(lo, q, zero), jnp.where(lo, zero, q)]


def _slc_kernel(qa_ref, qb_ref, k_ref, v_ref, tb_ref, sela_ref, selb_ref, oa_ref, ob_ref, vt_ref, *, n_d):
    t = T_ATT
    pi = pl.program_id(2)
    hg = C_HEADS // C_KV_GROUPS

    @pl.when(pi == 0)
    def _():
        _fill_transposed(vt_ref, v_ref, C_DIM)

    key_blk = jnp.right_shift(lax.broadcasted_iota(jnp.int32, (2 * t, LANES), 0), SLC_SHIFT)
    blk_slot = lax.broadcasted_iota(jnp.int32, (2 * t, LANES), 1)

    def one_hot_block(i):
        return jnp.where(blk_slot == key_blk + i * (2 * t // SLC_BLOCK), 1.0, 0.0).astype(BF16)

    lo = _lane_lo(t)

    def with_penalty(q_ref, sel_ref):
        out = []
        for pair in range(hg // 2):
            q = q_ref[0, :, pair * LANES:(pair + 1) * LANES] * (C_DIM ** -0.5)
            zero = jnp.zeros_like(q)
            out += [jnp.concatenate([qh, sel_ref[0, 0]], axis=1) for qh in (jnp.where(lo, q, zero), jnp.where(lo, zero, q))]
        return out

    def score(n, q_tile, i, s, diag):
        return _pair_bias(tb_ref, n, q_tile, i, s)

    full = slice(0, LANES)
    out_a, out_b = _flash_balanced(pi, n_d, with_penalty(qa_ref, sela_ref), with_penalty(qb_ref, selb_ref),
                                   [k_ref] * hg, [full] * hg, vt_ref, [_value_rows(0, C_DIM)] * hg, score,
                                   key_extra=one_hot_block)
    oa_ref[0] = jnp.concatenate(out_a, axis=0).T
    ob_ref[0] = jnp.concatenate(out_b, axis=0).T


def _win_kernel(q_ref, k_ref, v_ref, tb_ref, o_ref, vt_ref, *, n_prev, tiles_per_step):
    t = T_ATT
    step = pl.program_id(2)

    @pl.when(step == 0)
    def _():
        _fill_transposed(vt_ref, v_ref, C_DIM)

    full = slice(0, LANES)
    n_keys = (n_prev + 1) * t
    lo = _lane_lo(t)
    work = []
    for u_tile in range(tiles_per_step):
        qi = step * tiles_per_step + u_tile
        q = q_ref[0, u_tile * t:(u_tile + 1) * t, :] * (C_DIM ** -0.5)
        zero = jnp.zeros_like(q)
        streams = [(qh, k_ref, full, vt_ref, _value_rows(0, C_DIM))
                   for qh in (jnp.where(lo, q, zero), jnp.where(lo, zero, q))]
        j0 = jnp.maximum(qi - n_prev, 0)
        work.append((qi, j0, streams, _flash_scores(streams, j0 * t, n_keys)))
    for u_tile, (qi, j0, streams, raw) in enumerate(work):
        def score_fn(h, qi=qi, j0=j0):
            def fn(_, s):
                return jnp.concatenate([s[u * t:(u + 1) * t] + tb_ref[h, qi - j0 - u + n_prev]
                                        for u in range(n_prev + 1)], axis=0)
            return fn

        state = _flash_update(streams, [score_fn(0), score_fn(1)], j0 * t, n_keys, 0, raw, _flash_init(streams))
        o_ref[0, u_tile * t:(u_tile + 1) * t, :] = jnp.concatenate(_flash_finish(state), axis=0).T


def _slc_attention(hb3, tb, k_blk, v_blk, sel):
    b, s, _ = hb3.shape
    t = T_ATT
    n_d = s // t
    n_p = n_d // 2
    hg = C_HEADS // C_KV_GROUPS
    gw = hg * C_DIM
    o_a, o_b = pl.pallas_call(
        functools.partial(_slc_kernel, n_d=n_d),
        grid=(C_KV_GROUPS, b, n_p),
        in_specs=[pl.BlockSpec((1, t, gw), lambda g, bi, pi: (bi, pi, g)),
                  pl.BlockSpec((1, t, gw), lambda g, bi, pi: (bi, n_d - 1 - pi, g)),
                  pl.BlockSpec((1, s, LANES), lambda g, bi, pi: (bi, 0, k_blk + g)),
                  pl.BlockSpec((1, s, LANES), lambda g, bi, pi: (bi, 0, v_blk + g)),
                  pl.BlockSpec((hg, n_d + 1, t, t), lambda g, bi, pi: (g, 0, 0, 0), pipeline_mode=pl.Buffered(1)),
                  pl.BlockSpec((1, 1, t, LANES), lambda g, bi, pi: (bi, g, pi, 0)),
                  pl.BlockSpec((1, 1, t, LANES), lambda g, bi, pi: (bi, g, n_d - 1 - pi, 0))],
        out_specs=[pl.BlockSpec((1, t, gw), lambda g, bi, pi: (bi, pi, g)),
                   pl.BlockSpec((1, t, gw), lambda g, bi, pi: (bi, n_p - 1 - pi, g))],
        out_shape=[jax.ShapeDtypeStruct((b, s // 2, C_HEADS * C_DIM), F32)] * 2,
        scratch_shapes=[pltpu.VMEM((C_DIM + ONES_ROWS, s), BF16)],
        compiler_params=_cparams(("arbitrary", "arbitrary", "arbitrary")),
        name="nsa_selected",
    )(hb3, hb3, hb3, hb3, tb, sel, sel)
    return _join_halves(o_a, o_b)


def _win_attention(hb3, tb, k_blk, v_blk, n_prev):
    b, s, _ = hb3.shape
    t = T_ATT
    assert s >= (n_prev + 1) * t
    tiles = 8
    return pl.pallas_call(
        functools.partial(_win_kernel, n_prev=n_prev, tiles_per_step=tiles),
        grid=(C_HEADS // 2, b, s // (tiles * t)),
        in_specs=[pl.BlockSpec((1, tiles * t, LANES), lambda hp, bi, qi: (bi, qi, hp)),
                  pl.BlockSpec((1, s, LANES), lambda hp, bi, qi: (bi, 0, k_blk + hp // 2)),
                  pl.BlockSpec((1, s, LANES), lambda hp, bi, qi: (bi, 0, v_blk + hp // 2)),
                  pl.BlockSpec((2, 2 * n_prev + 1, t, t), lambda hp, bi, qi: (hp, 0, 0, 0))],
        out_specs=pl.BlockSpec((1, tiles * t, LANES), lambda hp, bi, qi: (bi, qi, hp)),
        out_shape=jax.ShapeDtypeStruct((b, s, C_HEADS * C_DIM), F32),
        scratch_shapes=[pltpu.VMEM((C_DIM + ONES_ROWS, s), BF16)],
        compiler_params=_cparams(("arbitrary", "arbitrary", "arbitrary")),
        name="nsa_window",
    )(hb3, hb3, hb3, tb)


def _rms(x, g):
    return x * lax.rsqrt(jnp.mean(x * x, axis=-1, keepdims=True) + RMS_EPS) * g


def _mla_prep_kernel(cq_ref, ckv_ref, kr1_ref, kr2_ref, cos_ref, sin_ref, gq_ref, gkv_ref,
                     wq1_ref, wq2_ref, wk_ref, wv_ref, q_ref, k_ref, v_ref):
    cqn = _rms(cq_ref[...], gq_ref[...]).astype(BF16)
    c = _rms(ckv_ref[...], gkv_ref[...]).astype(BF16)
    cos, sin = cos_ref[...], sin_ref[...]
    k_rope = kr1_ref[...] * cos + kr2_ref[...] * sin
    scale = (D_NOPE + D_ROPE) ** -0.5 * LOG2E
    for h in range(D_HEADS):
        cols = slice(h * LANES, (h + 1) * LANES)
        q = _dot(cqn, wq1_ref[:, cols]) * cos + _dot(cqn, wq2_ref[:, cols]) * sin
        q_ref[:, cols] = (q * scale).astype(q_ref.dtype)
        k_ref[:, cols] = (_dot(c, wk_ref[:, cols]) + k_rope).astype(k_ref.dtype)
    v_ref[...] = _dot(c, wv_ref[...]).astype(v_ref.dtype)


def _mla_prep(hf, cos, sin, gq, gkv, wq1, wq2, wk, wv, seq):
    m = hf.shape[0]
    n_seq = seq // TM
    const = lambda shape: pl.BlockSpec(shape, lambda i: (0, 0))
    hq = D_HEADS * LANES
    return pl.pallas_call(
        _mla_prep_kernel,
        grid=(m // TM,),
        in_specs=[pl.BlockSpec((TM, D_Q_LORA), lambda i: (i, 1)),
                  pl.BlockSpec((TM, D_KV_LORA), lambda i: (i, 3)),
                  pl.BlockSpec((TM, LANES), lambda i: (i, 8)),
                  pl.BlockSpec((TM, LANES), lambda i: (i, 9)),
                  pl.BlockSpec((TM, LANES), lambda i: (i % n_seq, 0)),
                  pl.BlockSpec((TM, LANES), lambda i: (i % n_seq, 0)),
                  const((1, D_Q_LORA)), const((1, D_KV_LORA)),
                  const(wq1.shape), const(wq2.shape), const(wk.shape), const(wv.shape)],
        out_specs=[pl.BlockSpec((TM, hq), lambda i: (i, 0)), pl.BlockSpec((TM, hq), lambda i: (i, 0)),
                   pl.BlockSpec((TM, D_HEADS * D_V), lambda i: (i, 0))],
        out_shape=[jax.ShapeDtypeStruct((m, hq), BF16), jax.ShapeDtypeStruct((m, hq), BF16),
                   jax.ShapeDtypeStruct((m, D_HEADS * D_V), BF16)],
        compiler_params=_cparams(("arbitrary",)),
        name="mla_prep",
    )(hf, hf, hf, hf, cos, sin, gq.reshape(1, -1), gkv.reshape(1, -1), wq1, wq2, wk, wv)


def _mla_kernel(qa_ref, qb_ref, k_ref, v_ref, oa_ref, ob_ref, vt_ref, *, n_d, n_heads):
    t = T_ATT
    pi = pl.program_id(2)

    @pl.when(pi == 0)
    def _():
        _fill_transposed(vt_ref, v_ref, D_V)

    head_cols = [slice(h * LANES, (h + 1) * LANES) for h in range(n_heads)]
    key_minus_query = (lax.broadcasted_iota(jnp.int32, (2 * t, t), 0)
                       - lax.broadcasted_iota(jnp.int32, (2 * t, t), 1))

    def score(n, q_tile, i, s, diag):
        return jnp.where(key_minus_query <= (q_tile - 2 * i) * t, s, NEG_INF) if diag else s

    out_a, out_b = _flash_balanced(pi, n_d, [qa_ref[0, :, c] for c in head_cols], [qb_ref[0, :, c] for c in head_cols],
                                   [k_ref] * n_heads, head_cols, vt_ref,
                                   [_value_rows(h, D_V) for h in range(n_heads)], score)
    oa_ref[0] = jnp.concatenate(out_a, axis=0).T.astype(oa_ref.dtype)
    ob_ref[0] = jnp.concatenate(out_b, axis=0).T.astype(ob_ref.dtype)


def _mla_attention(q3, k3, v3):
    b, s, _ = q3.shape
    t = T_ATT
    n_d = s // t
    n_p = n_d // 2
    nh = MLA_HEADS_PER_STEP
    o_a, o_b = pl.pallas_call(
        functools.partial(_mla_kernel, n_d=n_d, n_heads=nh),
        grid=(D_HEADS // nh, b, n_p),
        in_specs=[pl.BlockSpec((1, t, nh * LANES), lambda hp, bi, pi: (bi, pi, hp)),
                  pl.BlockSpec((1, t, nh * LANES), lambda hp, bi, pi: (bi, n_d - 1 - pi, hp)),
                  pl.BlockSpec((1, s, nh * LANES), lambda hp, bi, pi: (bi, 0, hp)),
                  pl.BlockSpec((1, s, nh * D_V), lambda hp, bi, pi: (bi, 0, hp))],
        out_specs=[pl.BlockSpec((1, t, nh * D_V), lambda hp, bi, pi: (bi, pi, hp)),
                   pl.BlockSpec((1, t, nh * D_V), lambda hp, bi, pi: (bi, n_p - 1 - pi, hp))],
        out_shape=[jax.ShapeDtypeStruct((b, s // 2, D_HEADS * D_V), BF16)] * 2,
        scratch_shapes=[pltpu.VMEM((nh * (D_V + ONES_ROWS), s), BF16)],
        compiler_params=_cparams(("arbitrary", "arbitrary", "arbitrary")),
        name="mla_attention",
    )(q3, q3, k3, v3)
    return _join_halves(o_a, o_b)


def _odd_out_kernel(gate_ref, oc_ref, os1_ref, os2_ref, ow_ref, od1_ref, od2_ref, x_ref, w_ref, g_ref, b_ref,
                    out_ref, *, tiles_per_seq):
    gates = 1.0 / (1.0 + jnp.exp(-gate_ref[...]))
    lo = _lane_lo(TM)
    n_c = C_HEADS * C_DIM
    o_slc = _pick_half(os1_ref, os2_ref, tiles_per_seq)
    y = _dot(_pick_half(od1_ref, od2_ref, tiles_per_seq), w_ref[n_c:, :])
    for blk in range(n_c // LANES):
        cols = slice(blk * LANES, (blk + 1) * LANES)
        acc = jnp.zeros((TM, LANES), F32)
        for br, o_br in enumerate((oc_ref[:, cols], o_slc[:, cols], ow_ref[:, cols])):
            c0 = br * C_HEADS + 2 * blk
            gexp = jnp.where(lo, gates[:, c0:c0 + 1], gates[:, c0 + 1:c0 + 2])
            acc = acc + gexp * o_br
        y = y + _dot(acc.astype(BF16), w_ref[cols, :])
    out_ref[...] = _residual_ln(x_ref[...], y, g_ref[...], b_ref[...])


def _odd_out(hf, o_cmp, o_slc_halves, o_win, o_d_halves, x, w_out, g, beta, seq):
    m = x.shape[0]
    n_c = C_HEADS * C_DIM
    tiles_per_seq = seq // TM
    row = lambda width: pl.BlockSpec((TM, width), lambda i: (i, 0))
    const = lambda shape: pl.BlockSpec(shape, lambda i: (0, 0))
    return pl.pallas_call(
        functools.partial(_odd_out_kernel, tiles_per_seq=tiles_per_seq),
        grid=(m // TM,),
        in_specs=[pl.BlockSpec((TM, LANES), lambda i: (i, 2)),
                  row(n_c)] + _half_specs(n_c, tiles_per_seq) + [row(n_c)] + _half_specs(D_HEADS * D_V, tiles_per_seq)
                 + [row(D_MODEL), const(w_out.shape), const((1, D_MODEL)), const((1, D_MODEL))],
        out_specs=row(D_MODEL),
        out_shape=jax.ShapeDtypeStruct((m, D_MODEL), F32),
        compiler_params=_cparams(("arbitrary",)),
        name="odd_out",
    )(hf, o_cmp, *o_slc_halves, o_win, *o_d_halves, x, w_out, g.reshape(1, -1), beta.reshape(1, -1))


ODD_BF16_W = 1536
ODD_F32_W = 1280


def _odd_in_columns():
    q_c = C_HEADS * C_DIM
    kv = lambda br, which, g: q_c + ((br * 2 + which) * C_KV_GROUPS + g) * C_DIM + np.arange(C_DIM)
    gate0 = q_c + 3 * 2 * C_KV_GROUPS * C_DIM
    cq0 = gate0 + 3 * C_HEADS
    ckv0 = cq0 + D_Q_LORA
    kr0 = ckv0 + D_KV_LORA
    zeros = lambda n: np.full(n, -1)
    cols = [np.arange(q_c)]
    for br in (1, 2):
        for which in (0, 1):
            for g in range(C_KV_GROUPS):
                cols += [kv(br, which, g), kv(br, which, g)]
    assert sum(len(c) for c in cols) == ODD_BF16_W
    for which in (0, 1):
        cols += [kv(0, which, 0), kv(0, which, 1)]
    cols += [gate0 + np.arange(3 * C_HEADS), zeros(LANES - 3 * C_HEADS)]
    cols += [cq0 + np.arange(D_Q_LORA), ckv0 + np.arange(D_KV_LORA)]
    half = D_ROPE // 2
    kr = kr0 + np.arange(D_ROPE)
    cols += [zeros(D_NOPE), kr, zeros(LANES - D_NOPE - D_ROPE)]
    cols += [zeros(D_NOPE), kr[half:], kr[:half], zeros(LANES - D_NOPE - D_ROPE)]
    cols = np.concatenate(cols)
    assert len(cols) == ODD_BF16_W + ODD_F32_W
    return cols


def _gather_cols(w, cols):
    return jnp.where(jnp.asarray(cols >= 0)[None, :], w[:, np.maximum(cols, 0)], 0.0)


def _mla_weight_columns():
    dq = D_NOPE + D_ROPE
    half = D_ROPE // 2
    zeros = lambda n: np.full(n, -1)
    q1, q2, k1 = [], [], []
    for h in range(D_HEADS):
        rope0 = h * dq + D_NOPE
        q1 += [h * dq + np.arange(dq), zeros(LANES - dq)]
        q2 += [zeros(D_NOPE), rope0 + half + np.arange(half), rope0 + np.arange(half), zeros(LANES - dq)]
        k1 += [h * D_NOPE + np.arange(D_NOPE), zeros(LANES - D_NOPE)]
    return np.concatenate(q1), np.concatenate(q2), np.concatenate(k1)


def _rope_tables(seq):
    half = D_ROPE // 2
    inv = ROPE_THETA ** (-jnp.arange(half, dtype=F32) / half)
    ang = jnp.arange(seq).astype(F32)[:, None] * inv
    cos, sin = jnp.cos(ang), jnp.sin(ang)
    pad = jnp.zeros((seq, LANES - D_NOPE - D_ROPE), F32)
    cos_t = jnp.concatenate([jnp.ones((seq, D_NOPE), F32), cos, cos, pad], axis=1)
    sin_t = jnp.concatenate([jnp.zeros((seq, D_NOPE), F32), -sin, sin, pad], axis=1)
    return cos_t, sin_t


def kernel(x, rel_bias, ev_w_in, ev_w_out, ev_lambda, ev_subln, od_w_in, od_w_out, od_cmp_pe, od_cmp_w1, od_cmp_w2, od_q_norm, od_kv_norm, od_w_uq, od_w_uk, od_w_uv, ffn_w_up, ffn_conv_w, ffn_conv_b, ffn_w_down, ln_g, ln_b):
    b, s, _ = x.shape
    m = b * s
    assert s % 1024 == 0 and s // B_PAIRS[-1][1] >= 2 * T_DIL and s // SLC_BLOCK <= LANES
    n_d = s // T_ATT

    tb_t = rel_bias.T.astype(F32)
    tb_log2 = tb_t[:C_HEADS] * LOG2E
    n_prev_win = -(-(WIN_SIZE - 1) // T_ATT)
    tb_causal = _bias_table(tb_log2, _toeplitz_idx(1, n_d, T_ATT, s))
    tb_win = _bias_table(tb_log2, _toeplitz_idx(n_prev_win, n_prev_win + 1, T_ATT, WIN_SIZE - 1))
    tb_cmp = _bias_table(tb_log2, _cmp_idx(s)).reshape(C_HEADS, s, s // CMP_STRIDE)
    tb_dil = [_bias_table(tb_t[A_HEADS + i * B_HEADS:A_HEADS + (i + 1) * B_HEADS], _dilated_idx(d))
              for i, (_, d) in enumerate(B_PAIRS)]
    cos_t, sin_t = _rope_tables(s)
    odd_cols = _odd_in_columns()
    q1_cols, q2_cols, k1_cols = _mla_weight_columns()

    xf = x.reshape(m, D_MODEL)
    for l in range(DEPTH):
        i = l // 2
        if l % 2 == 0:
            n_a = EVEN_IN - EVEN_DILATED
            w_in = ev_w_in[i] * jnp.where(jnp.arange(EVEN_IN) < 2 * A_HEADS * A_QK, LOG2E, 1.0)
            h_a, h_b = _inproj(xf, w_in.astype(BF16), (n_a, EVEN_DILATED), (BF16, F32), 768)
            h3 = h_a.reshape(b, s, n_a)
            hd3 = h_b.reshape(b, s, EVEN_DILATED)
            lam_init = 0.8 - 0.6 * math.exp(-0.3 * l)
            o_a = _diff_attention(h3, tb_causal, ev_lambda[i].astype(F32), ev_subln[i], lam_init)
            obs, lses = [], []
            for p_idx, (_, d) in enumerate(B_PAIRS):
                o, lse = _dilated_attention(hd3, tb_dil[p_idx], p_idx, d)
                obs.append(o)
                lses.append(lse)
            xf = _even_out(o_a, obs, lses, xf, ev_w_out[i].astype(BF16), ln_g[l, 0], ln_b[l, 0], s)
        else:
            w_in = _gather_cols(od_w_in[i], odd_cols) * jnp.where(jnp.arange(len(odd_cols)) < C_HEADS * C_DIM, LOG2E, 1.0)
            hb, hf = _inproj(xf, w_in.astype(BF16), (ODD_BF16_W, ODD_F32_W), (BF16, F32), 256)
            hb3 = hb.reshape(b, s, ODD_BF16_W)
            n_ch = s // CMP_STRIDE
            chunks = (hf[:, :4 * C_DIM].reshape(b, n_ch, CMP_STRIDE, 4, C_DIM)
                      .transpose(0, 3, 1, 2, 4).reshape(b, 4, n_ch, CMP_STRIDE * C_DIM))
            pe = od_cmp_pe[i].reshape(2, 2, CMP_STRIDE * C_DIM)
            w2d = jnp.concatenate([od_cmp_w2[i], od_cmp_w2[i]], axis=-1).astype(BF16)
            kvc = _compress(chunks, pe, od_cmp_w1[i].astype(BF16), w2d)
            o_cmp, sel = _cmp_attention(hb3, kvc, tb_cmp)
            o_slc = _slc_attention(hb3, tb_causal, 4, 6, sel)
            o_win = _win_attention(hb3, tb_win, 8, 10, n_prev_win)
            q_d, k_d, v_d = _mla_prep(hf, cos_t, sin_t, od_q_norm[i], od_kv_norm[i],
                                      _gather_cols(od_w_uq[i], q1_cols).astype(BF16),
                                      _gather_cols(od_w_uq[i], q2_cols).astype(BF16),
                                      _gather_cols(od_w_uk[i], k1_cols).astype(BF16),
                                      od_w_uv[i].astype(BF16), s)
            o_d = _mla_attention(q_d.reshape(b, s, -1), k_d.reshape(b, s, -1), v_d.reshape(b, s, -1))
            xf = _odd_out(hf, o_cmp.reshape(m, -1), o_slc, o_win.reshape(m, -1), o_d, xf,
                          od_w_out[i].astype(BF16), ln_g[l, 0], ln_b[l, 0], s)
        xf = _ffn(xf, ffn_w_up[l].astype(BF16), ffn_conv_w[l], ffn_conv_b[l], ffn_w_down[l].astype(BF16),
                  ln_g[l, 1], ln_b[l, 1], s)
    return xf.reshape(b, s, D_MODEL)
```

```python
import functools
import math

import numpy as np
import jax
import jax.numpy as jnp
from jax import lax
from jax.experimental import pallas as pl
from jax.experimental.pallas import tpu as pltpu

F32 = jnp.float32
BF16 = jnp.bfloat16

D_MODEL = 1024
DEPTH = 4
LN_EPS = 1e-5
RMS_EPS = 1e-5
N_BUCKETS = 32
MAX_DISTANCE = 2048
A_HEADS, A_QK, A_V = 4, 64, 128
B_PAIRS = ((128, 1), (512, 4), (2048, 16))
B_HEADS, B_DIM = 4, 64
C_HEADS, C_KV_GROUPS, C_DIM = 8, 2, 64
CMP_LEN, CMP_STRIDE, CMP_HIDDEN = 32, 16, 256
SLC_BLOCK, SLC_TOP_N = 64, 16
SLC_SHIFT = 6
WIN_SIZE = 512
D_HEADS, D_Q_LORA, D_KV_LORA, D_NOPE, D_ROPE, D_V = 8, 384, 256, 64, 32, 64
ROPE_THETA = 10000.0
D_FF = 2816
EVEN_DILATED = len(B_PAIRS) * 3 * B_HEADS * B_DIM
EVEN_IN = 2 * A_HEADS * A_QK * 2 + A_HEADS * A_V + EVEN_DILATED
ALPHA = (2 * DEPTH) ** 0.25

LANES = 128
HALF_LANE_SHIFT = 6
VMEM_LIMIT = 56 * 1024 * 1024
T_ATT = 256
MLA_HEADS_PER_STEP = 4
T_DIL = 128
TM = 512
FF_CHUNK = 256
NEG_INF = float("-inf")
M_INIT = -1e30
SEL_PENALTY = 2.0 ** 101
LOG2E = math.log2(math.e)
ONES_ROWS = 16


def _cparams(sem, flags=None):
    return pltpu.CompilerParams(dimension_semantics=sem, vmem_limit_bytes=VMEM_LIMIT, flags=flags)


def _dot(a, b):
    return jnp.dot(a, b, preferred_element_type=F32)


def _dot_nt(a, b):
    return lax.dot_general(a, b, (((1,), (1,)), ((), ())), preferred_element_type=F32)


def _bucket_np(dist):
    n = np.maximum(dist, 0)
    nf = np.maximum(n, 1).astype(np.float32)
    max_exact = N_BUCKETS // 2
    large = max_exact + (np.log(nf / max_exact) / math.log(MAX_DISTANCE / max_exact)
                         * (N_BUCKETS - max_exact)).astype(np.int32)
    return np.where(n < max_exact, n, np.minimum(large, N_BUCKETS - 1)).astype(np.int32)


def _toeplitz_idx(n_masked, n_delta, t, max_dist):
    key = np.arange(t)[:, None]
    query = np.arange(t)[None, :]
    out = []
    for delta in range(-n_masked, n_delta):
        dist = t * delta + query - key
        ok = (dist >= 0) & (dist <= max_dist)
        out.append(np.where(ok, _bucket_np(dist), -1))
    return np.stack(out).astype(np.int32)


def _dilated_idx(d):
    t = T_DIL
    r = np.arange(t)[:, None]
    c = np.arange(2 * t)[None, :]
    dist0 = r - c
    ok0 = (dist0 >= 0) & (c < t)
    dist1 = t + r - c
    ok1 = (dist1 >= 0) & (dist1 <= t)
    return np.stack([np.where(ok0, _bucket_np(dist0 * d), -1),
                     np.where(ok1, _bucket_np(dist1 * d), -1)]).astype(np.int32)


def _cmp_idx(seq):
    ncb = seq // CMP_STRIDE - 1
    q = np.arange(seq)[:, None]
    c = np.arange(seq // CMP_STRIDE)[None, :]
    dist = q - (c * CMP_STRIDE + CMP_LEN - 1)
    ok = (dist >= 0) & (c < ncb)
    return np.where(ok, _bucket_np(dist), -1).astype(np.int32).reshape(seq // T_ATT, T_ATT, seq // CMP_STRIDE)


def _bias_kernel(tbl_ref, idx_ref, o_ref):
    idx = idx_ref[0]
    rows = idx.shape[0]
    row = jnp.broadcast_to(tbl_ref[0], (rows, LANES))
    for c0 in range(0, idx.shape[1], LANES):
        ix = idx[:, c0:c0 + LANES]
        val = jnp.take_along_axis(row, jnp.maximum(ix, 0), axis=1)
        o_ref[0, 0, :, c0:c0 + LANES] = jnp.where(ix < 0, NEG_INF, val)


def _bias_table(tbl, idx):
    n_h = tbl.shape[0]
    n, r, c = idx.shape
    tbl_rows = jnp.pad(tbl, ((0, 0), (0, LANES - N_BUCKETS))).reshape(n_h, 1, LANES)
    return pl.pallas_call(
        _bias_kernel,
        grid=(n_h, n),
        in_specs=[pl.BlockSpec((1, 1, LANES), lambda h, i: (h, 0, 0)),
                  pl.BlockSpec((1, r, c), lambda h, i: (i, 0, 0))],
        out_specs=pl.BlockSpec((1, 1, r, c), lambda h, i: (h, i, 0, 0)),
        out_shape=jax.ShapeDtypeStruct((n_h, n, r, c), F32),
        compiler_params=_cparams(("arbitrary", "arbitrary")),
        name="bias_table",
    )(tbl_rows, jnp.asarray(idx))


def _inproj_kernel(x_ref, w_ref, *o_refs, widths, chunk):
    xb = x_ref[...].astype(BF16)
    off = 0
    for o_ref, width in zip(o_refs, widths):
        for c0 in range(0, width, chunk):
            o_ref[:, c0:c0 + chunk] = _dot(xb, w_ref[:, off + c0:off + c0 + chunk]).astype(o_ref.dtype)
        off += width


def _inproj(x, w, widths, dtypes, chunk):
    m = x.shape[0]
    n = w.shape[1]
    assert sum(widths) == n and all(wd % chunk == 0 for wd in widths)
    return pl.pallas_call(
        functools.partial(_inproj_kernel, widths=widths, chunk=chunk),
        grid=(m // TM,),
        in_specs=[pl.BlockSpec((TM, D_MODEL), lambda i: (i, 0)),
                  pl.BlockSpec((D_MODEL, n), lambda i: (0, 0))],
        out_specs=[pl.BlockSpec((TM, wd), lambda i: (i, 0)) for wd in widths],
        out_shape=[jax.ShapeDtypeStruct((m, wd), dt) for wd, dt in zip(widths, dtypes)],
        compiler_params=_cparams(("arbitrary",)),
        name="in_proj",
    )(x, w)


def _tree_reduce_rows(x, combine, reduce_fn):
    parts = [x[r:r + 8] for r in range(0, x.shape[0], 8)]
    while len(parts) > 1:
        parts = [combine(parts[k], parts[k + 1]) for k in range(0, len(parts) - 1, 2)] + \
                ([parts[-1]] if len(parts) % 2 else [])
    return reduce_fn(parts[0], axis=0, keepdims=True)


def _flash_init(streams):
    return tuple((jnp.full((1, q.shape[0]), M_INIT, F32), jnp.zeros((vt_rows.stop - vt_rows.start, q.shape[0]), F32))
                 for q, _, _, _, vt_rows in streams)


def _normalise(acc):
    d_v = acc.shape[0] - ONES_ROWS
    return acc[:d_v] / acc[d_v:d_v + 1]


def _flash_scores(streams, key_start, n_keys):
    keys = pl.ds(pl.multiple_of(key_start, T_ATT), n_keys)
    return tuple(_dot_nt(k_ref[0, keys, k_cols], q) for q, k_ref, k_cols, _, _ in streams)


def _flash_update(streams, score_fns, key_start, n_keys, i, raw, state):
    keys = pl.ds(pl.multiple_of(key_start, T_ATT), n_keys)
    scores = [fn(i, s) for fn, s in zip(score_fns, raw)]
    m_new = [jnp.maximum(m, _tree_reduce_rows(s, jnp.maximum, jnp.max)) for s, (m, _) in zip(scores, state)]
    probs = [jnp.exp2(s - mn) for s, mn in zip(scores, m_new)]
    out = []
    for (_, _, _, vt_ref, vt_rows), p, mn, (m, acc) in zip(streams, probs, m_new, state):
        acc = jnp.exp2(m - mn) * acc + _dot(vt_ref[vt_rows, keys], p.astype(BF16))
        out.append((mn, acc))
    return tuple(out)


def _flash_finish(state):
    return [_normalise(acc) for _, acc in state]


def _flash_balanced(pi, n_d, qa, qb, k_refs, k_cols, vt_ref, vt_rows, score, key_extra=None):
    kt = 2 * T_ATT
    t_q = qa[0].shape[0]
    n_streams = len(qa)
    n_slots = n_d // 2 + 1
    max_a = n_d // 4
    n_a = pi // 2 + 1
    tile_a, tile_b = pi, n_d - 1 - pi

    def slot(s):
        if s == 0:
            return True, tile_a, n_a - 1
        if s >= max_a:
            return False, tile_b, s - n_a
        is_a = s < n_a
        return is_a, jnp.where(is_a, tile_a, tile_b), jnp.where(is_a, n_a - 1 - s, s - n_a)

    def pick(is_a, a, b):
        return (a if is_a else b) if isinstance(is_a, bool) else jnp.where(is_a, a, b)

    def keys_of(i):
        return pl.ds(pl.multiple_of(i * kt, kt), kt)

    def raw_scores(s):
        is_a, _, i = slot(s)
        extra = None if key_extra is None else key_extra(i)
        out = []
        for n in range(n_streams):
            keys = k_refs[n][0, keys_of(i), k_cols[n]]
            if extra is not None:
                keys = jnp.concatenate([keys, extra], axis=1)
            out.append(_dot_nt(keys, pick(is_a, qa[n], qb[n])))
        return out

    state = [(jnp.full((1, t_q), M_INIT, F32), jnp.zeros((vt_rows[n].stop - vt_rows[n].start, t_q), F32))
             for n in range(n_streams)]
    out_a = [jnp.zeros((acc.shape[0] - ONES_ROWS, t_q), F32) for _, acc in state]
    pending = None
    raw = raw_scores(0)

    def fold(state, pending):
        alphas, probs, i_prev = pending
        return [(m, alphas[n] * acc + _dot(vt_ref[vt_rows[n], keys_of(i_prev)], probs[n]))
                for n, (m, acc) in enumerate(state)]

    for s in range(n_slots):
        raw_next = raw_scores(s + 1) if s + 1 < n_slots else None
        is_a, q_tile, i = slot(s)
        if pending is not None:
            state = fold(state, pending)
        if 1 <= s <= max_a:
            switch = s == n_a
            out_a = [jnp.where(switch, _normalise(acc), o) for (_, acc), o in zip(state, out_a)]
            state = [(jnp.where(switch, M_INIT, m), jnp.where(switch, 0.0, acc)) for m, acc in state]
        diag = s == 0 or s == n_slots - 1
        scores = [score(n, q_tile, i, raw[n], diag) for n in range(n_streams)]
        m_new = [jnp.maximum(m, _tree_reduce_rows(sc, jnp.maximum, jnp.max)) for sc, (m, _) in zip(scores, state)]
        probs = [jnp.exp2(sc - mn).astype(BF16) for sc, mn in zip(scores, m_new)]
        alphas = [jnp.exp2(m - mn) for (m, _), mn in zip(state, m_new)]
        state = [(mn, acc) for mn, (_, acc) in zip(m_new, state)]
        pending = (alphas, probs, i)
        raw = raw_next
    state = fold(state, pending)
    return out_a, [_normalise(acc) for _, acc in state]


def _pair_bias(tb_ref, h, q_tile, i, s):
    t = T_ATT
    d0 = q_tile - 2 * i
    return jnp.concatenate([s[:t] + tb_ref[h, d0 + 1], s[t:] + tb_ref[h, d0]], axis=0)


def _fill_transposed(vt_ref, v_ref, d_v):
    n = v_ref.shape[1]
    n_heads = vt_ref.shape[0] // (d_v + ONES_ROWS)
    step = 2 * T_ATT
    for c0 in range(0, n, step):
        v_t = v_ref[0, c0:c0 + step, :].astype(F32).T.astype(BF16)
        for h in range(n_heads):
            r0 = h * (d_v + ONES_ROWS)
            vt_ref[r0:r0 + d_v, c0:c0 + step] = v_t[h * d_v:(h + 1) * d_v]
            vt_ref[r0 + d_v:r0 + d_v + ONES_ROWS, c0:c0 + step] = jnp.ones((ONES_ROWS, step), BF16)


def _value_rows(h, d_v):
    return slice(h * (d_v + ONES_ROWS), (h + 1) * (d_v + ONES_ROWS))


def _lane_lo(rows):
    return lax.broadcasted_iota(jnp.int32, (rows, LANES), 1) < (LANES // 2)


def _diff_kernel(lam_ref, g_ref, q0a_ref, q0b_ref, q1a_ref, q1b_ref, k0_ref, k1_ref, v_ref, tb_ref,
                 oa_ref, ob_ref, vt_ref, *, lam_init, n_d):
    pi = pl.program_id(2)

    @pl.when(pi == 0)
    def _():
        _fill_transposed(vt_ref, v_ref, A_V)

    lp = lam_ref[...]
    lam = (jnp.exp(jnp.sum(lp[0:1] * lp[1:2], axis=-1, keepdims=True))
           - jnp.exp(jnp.sum(lp[2:3] * lp[3:4], axis=-1, keepdims=True)) + lam_init)
    lo = _lane_lo(T_ATT)

    def prep(q_ref):
        q = q_ref[0] * (A_QK ** -0.5)
        zero = jnp.zeros_like(q)
        return [jnp.where(lo, q, zero), jnp.where(lo, zero, q)]

    full = slice(0, LANES)

    def score(n, q_tile, i, s, diag):
        return _pair_bias(tb_ref, n % 2, q_tile, i, s)

    out_a, out_b = _flash_balanced(pi, n_d, prep(q0a_ref) + prep(q1a_ref), prep(q0b_ref) + prep(q1b_ref),
                                   [k0_ref, k0_ref, k1_ref, k1_ref], [full] * 4, vt_ref,
                                   [_value_rows(0, A_V), _value_rows(1, A_V)] * 2, score)
    for outs, o_ref in ((out_a, oa_ref), (out_b, ob_ref)):
        for head in range(2):
            d = (outs[head] - lam * outs[2 + head]).T
            ms = jnp.mean(d * d, axis=-1, keepdims=True)
            o_ref[0, :, head * A_V:(head + 1) * A_V] = (d * lax.rsqrt(ms + RMS_EPS) * g_ref[...]
                                                        * (1.0 - lam_init)).astype(o_ref.dtype)


def _join_halves(o_a, o_b):
    return o_a.reshape(-1, o_a.shape[-1]), o_b.reshape(-1, o_b.shape[-1])


def _diff_attention(h3, tb, lam_p, subln_g, lam_init):
    b, s, _ = h3.shape
    t = T_ATT
    n_d = s // t
    n_p = n_d // 2
    q_spec = lambda col0, second: pl.BlockSpec(
        (1, t, LANES), lambda hp, bi, pi: (bi, (n_d - 1 - pi) if second else pi, col0 + hp))
    k_spec = lambda col0: pl.BlockSpec((1, s, LANES), lambda hp, bi, pi: (bi, 0, col0 + hp))
    o_a, o_b = pl.pallas_call(
        functools.partial(_diff_kernel, lam_init=lam_init, n_d=n_d),
        grid=(A_HEADS // 2, b, n_p),
        in_specs=[pl.BlockSpec((4, A_QK), lambda hp, bi, pi: (0, 0)),
                  pl.BlockSpec((1, A_V), lambda hp, bi, pi: (0, 0)),
                  q_spec(0, False), q_spec(0, True), q_spec(2, False), q_spec(2, True),
                  k_spec(4), k_spec(6),
                  pl.BlockSpec((1, s, 2 * A_V), lambda hp, bi, pi: (bi, 0, 4 + hp)),
                  pl.BlockSpec((2, n_d + 1, t, t), lambda hp, bi, pi: (hp, 0, 0, 0))],
        out_specs=[pl.BlockSpec((1, t, 2 * A_V), lambda hp, bi, pi: (bi, pi, hp)),
                   pl.BlockSpec((1, t, 2 * A_V), lambda hp, bi, pi: (bi, n_p - 1 - pi, hp))],
        out_shape=[jax.ShapeDtypeStruct((b, s // 2, A_HEADS * A_V), BF16)] * 2,
        scratch_shapes=[pltpu.VMEM((2 * (A_V + ONES_ROWS), s), BF16)],
        compiler_params=_cparams(("arbitrary", "arbitrary", "arbitrary")),
        name="diff_attention",
    )(lam_p, subln_g.reshape(1, A_V), h3, h3, h3, h3, h3, h3, h3, tb)
    return _join_halves(o_a, o_b)


def _dilated_kernel(q_ref, k_ref, v_ref, tb_ref, o_ref, lse_ref, *, d, residues):
    t = T_DIL
    n_blocks = q_ref.shape[1] // (d * t)
    lo = _lane_lo(t)
    scale = B_DIM ** -0.5

    def block(idx, carry):
        r = pl.program_id(2) * residues + idx // n_blocks
        bi = idx % n_blocks
        var = jnp.minimum(bi, 1)
        qrows = pl.ds(r + d * t * bi, t, stride=d)
        krows = pl.ds(r + d * t * jnp.maximum(bi - 1, 0), 2 * t, stride=d)
        q = (q_ref[0, qrows, :] * scale).astype(BF16)
        kk = k_ref[0, krows, :].astype(BF16)
        vv = v_ref[0, krows, :].astype(BF16)
        o_half, l_half = [], []
        for half in range(2):
            keep = lo if half == 0 else jnp.logical_not(lo)
            qm = jnp.where(keep, q, jnp.zeros_like(q))
            s = _dot_nt(qm, kk) + tb_ref[half, var]
            m = jnp.max(s, axis=-1, keepdims=True)
            p = jnp.exp(s - m)
            l = jnp.sum(p, axis=-1, keepdims=True)
            o_half.append(_dot(p.astype(BF16), vv) / l)
            l_half.append(m + jnp.log(l))
        o_ref[0, qrows, :] = jnp.where(lo, o_half[0], o_half[1])
        lse_ref[0, qrows, :] = jnp.where(lo, l_half[0], l_half[1])
        return carry

    lax.fori_loop(0, residues * n_blocks, block, 0, unroll=8)


def _dilated_attention(hd3, tb, pair_idx, d):
    b, s, _ = hd3.shape
    hd = B_HEADS * B_DIM
    base = pair_idx * 3 * (hd // LANES)
    seq = lambda which: pl.BlockSpec((1, s, LANES), lambda bi, hp, r: (bi, 0, base + which * (hd // LANES) + hp))
    out = pl.BlockSpec((1, s, LANES), lambda bi, hp, r: (bi, 0, hp))
    n_blocks = s // (d * T_DIL)
    residues = min(d, max(1, 8 // n_blocks))
    assert d % residues == 0 and (residues * n_blocks) % 4 == 0
    o, lse = pl.pallas_call(
        functools.partial(_dilated_kernel, d=d, residues=residues),
        grid=(b, hd // LANES, d // residues),
        in_specs=[seq(0), seq(1), seq(2),
                  pl.BlockSpec((2, 2, T_DIL, 2 * T_DIL), lambda bi, hp, r: (hp, 0, 0, 0))],
        out_specs=[out, out],
        out_shape=[jax.ShapeDtypeStruct((b, s, hd), F32)] * 2,
        compiler_params=_cparams(("arbitrary", "arbitrary", "arbitrary")),
        name="dilated_attention",
    )(hd3, hd3, hd3, tb)
    return o.reshape(b * s, hd), lse.reshape(b * s, hd)


def _residual_ln(x, y, g, beta):
    z = ALPHA * x + y
    mu = jnp.mean(z, axis=-1, keepdims=True)
    zc = z - mu
    var = jnp.mean(zc * zc, axis=-1, keepdims=True)
    return zc * lax.rsqrt(var + LN_EPS) * g + beta


def _half_specs(width, tiles_per_seq):
    half = tiles_per_seq // 2
    first = pl.BlockSpec((TM, width), lambda i: ((i // tiles_per_seq) * half + jnp.minimum(i % tiles_per_seq, half - 1), 0))
    second = pl.BlockSpec((TM, width), lambda i: ((i // tiles_per_seq) * half + jnp.maximum(i % tiles_per_seq - half, 0), 0))
    return [first, second]


def _pick_half(first_ref, second_ref, tiles_per_seq):
    in_first = pl.program_id(0) % tiles_per_seq < tiles_per_seq // 2
    return jnp.where(in_first, first_ref[...], second_ref[...])


def _even_out_kernel(oa1_ref, oa2_ref, o0_ref, o1_ref, o2_ref, l0_ref, l1_ref, l2_ref, x_ref, w_ref, g_ref, b_ref,
                     out_ref, *, tiles_per_seq):
    l0, l1, l2 = l0_ref[...], l1_ref[...], l2_ref[...]
    mx = jnp.maximum(jnp.maximum(l0, l1), l2)
    e0, e1, e2 = jnp.exp(l0 - mx), jnp.exp(l1 - mx), jnp.exp(l2 - mx)
    inv = 1.0 / (e0 + e1 + e2)
    ob = (e0 * inv) * o0_ref[...] + (e1 * inv) * o1_ref[...] + (e2 * inv) * o2_ref[...]
    n_a = A_HEADS * A_V
    oa = _pick_half(oa1_ref, oa2_ref, tiles_per_seq)
    y = _dot(oa, w_ref[0:n_a, :]) + _dot(ob.astype(BF16), w_ref[n_a:, :])
    out_ref[...] = _residual_ln(x_ref[...], y, g_ref[...], b_ref[...])


def _even_out(oa_halves, obs, lses, x, w_out, g, beta, seq):
    m = x.shape[0]
    hd = B_HEADS * B_DIM
    tiles_per_seq = seq // TM
    row = lambda width: pl.BlockSpec((TM, width), lambda i: (i, 0))
    const = lambda shape: pl.BlockSpec(shape, lambda i: (0, 0))
    return pl.pallas_call(
        functools.partial(_even_out_kernel, tiles_per_seq=tiles_per_seq),
        grid=(m // TM,),
        in_specs=_half_specs(A_HEADS * A_V, tiles_per_seq) + [row(hd)] * 6 + [
            row(D_MODEL), const(w_out.shape), const((1, D_MODEL)), const((1, D_MODEL))],
        out_specs=row(D_MODEL),
        out_shape=jax.ShapeDtypeStruct((m, D_MODEL), F32),
        compiler_params=_cparams(("arbitrary",)),
        name="even_out",
    )(*oa_halves, *obs, *lses, x, w_out, g.reshape(1, -1), beta.reshape(1, -1))


def _gelu(x):
    return 0.5 * x * (1.0 + jnp.tanh(math.sqrt(2.0 / math.pi) * (x + 0.044715 * (x * x * x))))


def _ffn_kernel(x_ref, wu_ref, cw_ref, cb_ref, wd_ref, g_ref, b_ref, out_ref, u_ref, gs_ref, tail_ref,
                *, tiles_per_seq):
    halo = 8
    n_chunks = D_FF // FF_CHUNK

    @pl.when(pl.program_id(0) % tiles_per_seq == 0)
    def _():
        tail_ref[...] = jnp.zeros(tail_ref.shape, F32)

    x = x_ref[...]
    xb = x.astype(BF16)

    def up(c):
        cols = slice(c * FF_CHUNK, (c + 1) * FF_CHUNK)
        return _dot(xb, wu_ref[:, cols]), _dot(xb, wu_ref[:, D_FF + c * FF_CHUNK:D_FF + (c + 1) * FF_CHUNK])

    def activate(c, a, gate):
        cols = slice(c * FF_CHUNK, (c + 1) * FF_CHUNK)
        gs = gs_ref.at[c % 2]
        gs[0:halo, :] = tail_ref[:, cols]
        gs[halo:, :] = gate
        tail_ref[:, cols] = gate[TM - halo:, :]
        conv = (gs[pl.ds(halo - 2, TM), :] * cw_ref[0:1, cols] + gs[pl.ds(halo - 1, TM), :] * cw_ref[1:2, cols]
                + gate * cw_ref[2:3, cols] + cb_ref[:, cols])
        u_ref[:, cols] = (_gelu(conv) * a).astype(BF16)

    nxt = up(0)
    for c in range(n_chunks):
        cur = nxt
        if c + 1 < n_chunks:
            nxt = up(c + 1)
        activate(c, *cur)
    out_ref[...] = _residual_ln(x, _dot(u_ref[...], wd_ref[...]), g_ref[...], b_ref[...])


def _ffn(x, w_up, conv_w, conv_b, w_down, g, beta, seq):
    m = x.shape[0]
    const = lambda shape: pl.BlockSpec(shape, lambda i: (0, 0))
    return pl.pallas_call(
        functools.partial(_ffn_kernel, tiles_per_seq=seq // TM),
        grid=(m // TM,),
        in_specs=[pl.BlockSpec((TM, D_MODEL), lambda i: (i, 0)), const(w_up.shape), const(conv_w.shape),
                  const((1, D_FF)), const(w_down.shape), const((1, D_MODEL)), const((1, D_MODEL))],
        out_specs=pl.BlockSpec((TM, D_MODEL), lambda i: (i, 0)),
        out_shape=jax.ShapeDtypeStruct((m, D_MODEL), F32),
        scratch_shapes=[pltpu.VMEM((TM, D_FF), BF16), pltpu.VMEM((2, TM + 8, FF_CHUNK), F32),
                        pltpu.VMEM((8, D_FF), F32)],
        compiler_params=_cparams(("arbitrary",)),
        name="conv_ffn",
    )(x, w_up, conv_w, conv_b.reshape(1, -1), w_down, g.reshape(1, -1), beta.reshape(1, -1))


def _compress_kernel(ch_ref, pe_ref, w1_ref, w2_ref, o_ref):
    half = CMP_STRIDE * C_DIM
    ch = ch_ref[0, 0]
    a = _dot((ch + pe_ref[0, 0:1, :]).astype(BF16), w1_ref[0, 0:half, :])
    b = _dot((ch + pe_ref[0, 1:2, :]).astype(BF16), w1_ref[0, half:, :])
    n = ch.shape[0]
    hid = _gelu(a + pltpu.roll(b, n - 1, 0))
    o_ref[0, 0] = _dot(hid.astype(BF16), w2_ref[0]).astype(o_ref.dtype)


def _compress(chunks, pe, w1, w2d):
    b, _, n, width = chunks.shape
    return pl.pallas_call(
        _compress_kernel,
        grid=(b, 4),
        in_specs=[pl.BlockSpec((1, 1, n, width), lambda bi, j: (bi, j, 0, 0)),
                  pl.BlockSpec((1, 2, width), lambda bi, j: (j // 2, 0, 0)),
                  pl.BlockSpec((1, 2 * width, CMP_HIDDEN), lambda bi, j: (j // 2, 0, 0)),
                  pl.BlockSpec((1, CMP_HIDDEN, LANES), lambda bi, j: (j // 2, 0, 0))],
        out_specs=pl.BlockSpec((1, 1, n, LANES), lambda bi, j: (bi, j, 0, 0)),
        out_shape=jax.ShapeDtypeStruct((b, 4, n, LANES), BF16),
        compiler_params=_cparams(("arbitrary", "arbitrary")),
        name="nsa_compress",
    )(chunks, pe, w1, w2d)


def _cmp_kernel(q_ref, kc_ref, vc_ref, tb_ref, o_ref, sel_ref, pg_ref, sc_ref, pen_ref, *, tiles):
    t = T_ATT
    n_c = kc_ref.shape[2]
    n_sel = n_c // (SLC_BLOCK // CMP_STRIDE)
    lo = _lane_lo(t)
    kc = kc_ref[0, 0]
    vc = vc_ref[0, 0]
    raw = []
    for u in range(tiles):
        for pair in range(2):
            q = q_ref[0, u * t:(u + 1) * t, pair * LANES:(pair + 1) * LANES] * (C_DIM ** -0.5)
            zero = jnp.zeros_like(q)
            raw += [_dot_nt(jnp.where(lo, q, zero), kc), _dot_nt(jnp.where(lo, zero, q), kc)]
    probs = []
    for n, s in enumerate(raw):
        u, head = divmod(n, 4)
        s = s + tb_ref[head, u * t:(u + 1) * t, :]
        m = jnp.maximum(jnp.max(s, axis=-1, keepdims=True), M_INIT)
        p = jnp.exp2(s - m)
        den = jnp.sum(p, axis=-1, keepdims=True)
        probs.append(p / jnp.where(den > 0, den, 1.0))
    outs = [_dot(p.astype(BF16), vc) for p in probs]
    for u in range(tiles):
        for pair in range(2):
            o_ref[0, u * t:(u + 1) * t, pair * LANES:(pair + 1) * LANES] = jnp.where(
                lo, outs[4 * u + 2 * pair], outs[4 * u + 2 * pair + 1])
    for u in range(tiles):
        _select_blocks((probs[4 * u] + probs[4 * u + 1]) + (probs[4 * u + 2] + probs[4 * u + 3]),
                       pl.program_id(1) * tiles + u, sel_ref.at[0, 0, u * t:(u + 1) * t, :],
                       pg_ref.at[u], sc_ref.at[u], pen_ref.at[u], n_c, n_sel)


def _select_blocks(pg, qi, sel_out, pg_ref, sc_ref, pen_ref, n_c, n_sel):
    t = T_ATT
    pad = 8
    pg_t = pg.T
    for c in range(t // LANES):
        pg_ref[c, 0:pad, :] = jnp.zeros((pad, LANES), F32)
        pg_ref[c, pad:pad + n_c, :] = pg_t[:, c * LANES:(c + 1) * LANES]
        pg_ref[c, pad + n_c:, :] = jnp.zeros((pad, LANES), F32)
    r = SLC_BLOCK // CMP_STRIDE
    tap = lambda k: jnp.concatenate(
        [pg_ref[c, pl.ds(pad + k, n_sel, stride=r), :] for c in range(t // LANES)], axis=1)
    score = (0.5 * tap(-1) + ((tap(0) + tap(1)) + tap(2))) + 0.5 * tap(3)
    jb = lax.broadcasted_iota(jnp.int32, (n_sel, t), 0)
    qblk = jnp.right_shift(qi * t + lax.broadcasted_iota(jnp.int32, (n_sel, t), 1), SLC_SHIFT)
    forced = (jb == 0) | (jb == qblk) | (jb == qblk - 1)
    sc_ref[...] = jnp.where(forced, jnp.inf, jnp.where(jb <= qblk, score, NEG_INF))
    pen_ref[...] = jnp.zeros(pen_ref.shape, F32)
    groups = range(0, n_sel, 8)
    sc_g = [sc_ref[g0:g0 + 8, :] for g0 in groups]
    sub = lax.broadcasted_iota(jnp.int32, (8, t), 0)
    cnt_g = [jnp.zeros((8, t), jnp.int32) for _ in groups]
    for i in range(n_sel):
        row = jnp.broadcast_to(sc_ref[pl.ds(i, 1), :], (8, t))
        for k, g0 in enumerate(groups):
            if i < g0:
                beats = row >= sc_g[k]
            elif i >= g0 + 8:
                beats = row > sc_g[k]
            else:
                beats = (row > sc_g[k]) | ((row == sc_g[k]) & (i - g0 < sub))
            cnt_g[k] = cnt_g[k] + beats.astype(jnp.int32)
    for k, g0 in enumerate(groups):
        pen_ref[g0:g0 + 8, :] = jnp.where(cnt_g[k] < SLC_TOP_N, 0.0, -SEL_PENALTY)
    sel_out[...] = pen_ref[...].T.astype(sel_out.dtype)


def _cmp_attention(hb3, kvc, tb):
    b, s, _ = hb3.shape
    t = T_ATT
    n_c = s // CMP_STRIDE
    n_sel = s // SLC_BLOCK
    assert n_sel <= LANES
    gw = (C_HEADS // C_KV_GROUPS) * C_DIM
    tiles = 2
    rows = tiles * t
    return pl.pallas_call(
        functools.partial(_cmp_kernel, tiles=tiles),
        grid=(C_KV_GROUPS, s // rows, b),
        in_specs=[pl.BlockSpec((1, rows, gw), lambda g, qi, bi: (bi, qi, g)),
                  pl.BlockSpec((1, 1, n_c, LANES), lambda g, qi, bi: (bi, g, 0, 0)),
                  pl.BlockSpec((1, 1, n_c, LANES), lambda g, qi, bi: (bi, 2 + g, 0, 0)),
                  pl.BlockSpec((4, rows, n_c), lambda g, qi, bi: (g, qi, 0))],
        out_specs=[pl.BlockSpec((1, rows, gw), lambda g, qi, bi: (bi, qi, g)),
                   pl.BlockSpec((1, 1, rows, LANES), lambda g, qi, bi: (bi, g, qi, 0))],
        out_shape=[jax.ShapeDtypeStruct((b, s, C_HEADS * C_DIM), F32),
                   jax.ShapeDtypeStruct((b, C_KV_GROUPS, s, LANES), BF16)],
        scratch_shapes=[pltpu.VMEM((tiles, t // LANES, n_c + 16, LANES), F32), pltpu.VMEM((tiles, n_sel, t), F32),
                        pltpu.VMEM((tiles, LANES, t), F32)],
        compiler_params=_cparams(("arbitrary", "arbitrary", "arbitrary")),
        name="nsa_cmp_attention",
    )(hb3, kvc, kvc, tb)


def _head_pair_queries(q_ref):
    lo = _lane_lo(T_ATT)
    q = q_ref[0] * (C_DIM ** -0.5)
    zero = jnp.zeros_like(q)
    return [jnp.where(lo, q, zero), jnp.where(lo, zero, q)]


def _slc_kernel(qa_ref, qb_ref, k_ref, v_ref, tb_ref, sela_ref, selb_ref, oa_ref, ob_ref, vt_ref, *, n_d):
    t = T_ATT
    pi = pl.program_id(2)
    hg = C_HEADS // C_KV_GROUPS

    @pl.when(pi == 0)
    def _():
        _fill_transposed(vt_ref, v_ref, C_DIM)

    key_blk = jnp.right_shift(lax.broadcasted_iota(jnp.int32, (2 * t, LANES), 0), SLC_SHIFT)
    blk_slot = lax.broadcasted_iota(jnp.int32, (2 * t, LANES), 1)

    def one_hot_block(i):
        return jnp.where(blk_slot == key_blk + i * (2 * t // SLC_BLOCK), 1.0, 0.0).astype(BF16)

    lo = _lane_lo(t)

    def with_penalty(q_ref, sel_ref):
        out = []
        for pair in range(hg // 2):
            q = q_ref[0, :, pair * LANES:(pair + 1) * LANES] * (C_DIM ** -0.5)
            zero = jnp.zeros_like(q)
            out += [jnp.concatenate([qh, sel_ref[0, 0]], axis=1) for qh in (jnp.where(lo, q, zero), jnp.where(lo, zero, q))]
        return out

    def score(n, q_tile, i, s, diag):
        return _pair_bias(tb_ref, n, q_tile, i, s)

    full = slice(0, LANES)
    out_a, out_b = _flash_balanced(pi, n_d, with_penalty(qa_ref, sela_ref), with_penalty(qb_ref, selb_ref),
                                   [k_ref] * hg, [full] * hg, vt_ref, [_value_rows(0, C_DIM)] * hg, score,
                                   key_extra=one_hot_block)
    oa_ref[0] = jnp.concatenate(out_a, axis=0).T
    ob_ref[0] = jnp.concatenate(out_b, axis=0).T


def _win_kernel(q_ref, k_ref, v_ref, tb_ref, o_ref, vt_ref, *, n_prev, tiles_per_step):
    t = T_ATT
    step = pl.program_id(2)

    @pl.when(step == 0)
    def _():
        _fill_transposed(vt_ref, v_ref, C_DIM)

    full = slice(0, LANES)
    n_keys = (n_prev + 1) * t
    lo = _lane_lo(t)
    work = []
    for u_tile in range(tiles_per_step):
        qi = step * tiles_per_step + u_tile
        q = q_ref[0, u_tile * t:(u_tile + 1) * t, :] * (C_DIM ** -0.5)
        zero = jnp.zeros_like(q)
        streams = [(qh, k_ref, full, vt_ref, _value_rows(0, C_DIM))
                   for qh in (jnp.where(lo, q, zero), jnp.where(lo, zero, q))]
        j0 = jnp.maximum(qi - n_prev, 0)
        work.append((qi, j0, streams, _flash_scores(streams, j0 * t, n_keys)))
    for u_tile, (qi, j0, streams, raw) in enumerate(work):
        def score_fn(h, qi=qi, j0=j0):
            def fn(_, s):
                return jnp.concatenate([s[u * t:(u + 1) * t] + tb_ref[h, qi - j0 - u + n_prev]
                                        for u in range(n_prev + 1)], axis=0)
            return fn

        state = _flash_update(streams, [score_fn(0), score_fn(1)], j0 * t, n_keys, 0, raw, _flash_init(streams))
        o_ref[0, u_tile * t:(u_tile + 1) * t, :] = jnp.concatenate(_flash_finish(state), axis=0).T


def _slc_attention(hb3, tb, k_blk, v_blk, sel):
    b, s, _ = hb3.shape
    t = T_ATT
    n_d = s // t
    n_p = n_d // 2
    hg = C_HEADS // C_KV_GROUPS
    gw = hg * C_DIM
    o_a, o_b = pl.pallas_call(
        functools.partial(_slc_kernel, n_d=n_d),
        grid=(C_KV_GROUPS, b, n_p),
        in_specs=[pl.BlockSpec((1, t, gw), lambda g, bi, pi: (bi, pi, g)),
                  pl.BlockSpec((1, t, gw), lambda g, bi, pi: (bi, n_d - 1 - pi, g)),
                  pl.BlockSpec((1, s, LANES), lambda g, bi, pi: (bi, 0, k_blk + g)),
                  pl.BlockSpec((1, s, LANES), lambda g, bi, pi: (bi, 0, v_blk + g)),
                  pl.BlockSpec((hg, n_d + 1, t, t), lambda g, bi, pi: (g, 0, 0, 0), pipeline_mode=pl.Buffered(1)),
                  pl.BlockSpec((1, 1, t, LANES), lambda g, bi, pi: (bi, g, pi, 0)),
                  pl.BlockSpec((1, 1, t, LANES), lambda g, bi, pi: (bi, g, n_d - 1 - pi, 0))],
        out_specs=[pl.BlockSpec((1, t, gw), lambda g, bi, pi: (bi, pi, g)),
                   pl.BlockSpec((1, t, gw), lambda g, bi, pi: (bi, n_p - 1 - pi, g))],
        out_shape=[jax.ShapeDtypeStruct((b, s // 2, C_HEADS * C_DIM), F32)] * 2,
        scratch_shapes=[pltpu.VMEM((C_DIM + ONES_ROWS, s), BF16)],
        compiler_params=_cparams(("arbitrary", "arbitrary", "arbitrary")),
        name="nsa_selected",
    )(hb3, hb3, hb3, hb3, tb, sel, sel)
    return _join_halves(o_a, o_b)


def _win_attention(hb3, tb, k_blk, v_blk, n_prev):
    b, s, _ = hb3.shape
    t = T_ATT
    assert s >= (n_prev + 1) * t
    tiles = 8
    return pl.pallas_call(
        functools.partial(_win_kernel, n_prev=n_prev, tiles_per_step=tiles),
        grid=(C_HEADS // 2, b, s // (tiles * t)),
        in_specs=[pl.BlockSpec((1, tiles * t, LANES), lambda hp, bi, qi: (bi, qi, hp)),
                  pl.BlockSpec((1, s, LANES), lambda hp, bi, qi: (bi, 0, k_blk + hp // 2)),
                  pl.BlockSpec((1, s, LANES), lambda hp, bi, qi: (bi, 0, v_blk + hp // 2)),
                  pl.BlockSpec((2, 2 * n_prev + 1, t, t), lambda hp, bi, qi: (hp, 0, 0, 0))],
        out_specs=pl.BlockSpec((1, tiles * t, LANES), lambda hp, bi, qi: (bi, qi, hp)),
        out_shape=jax.ShapeDtypeStruct((b, s, C_HEADS * C_DIM), F32),
        scratch_shapes=[pltpu.VMEM((C_DIM + ONES_ROWS, s), BF16)],
        compiler_params=_cparams(("arbitrary", "arbitrary", "arbitrary")),
        name="nsa_window",
    )(hb3, hb3, hb3, tb)


def _rms(x, g):
    return x * lax.rsqrt(jnp.mean(x * x, axis=-1, keepdims=True) + RMS_EPS) * g


def _mla_prep_kernel(cq_ref, ckv_ref, kr1_ref, kr2_ref, cos_ref, sin_ref, gq_ref, gkv_ref,
                     wq1_ref, wq2_ref, wk_ref, wv_ref, q_ref, k_ref, v_ref):
    cqn = _rms(cq_ref[...], gq_ref[...]).astype(BF16)
    c = _rms(ckv_ref[...], gkv_ref[...]).astype(BF16)
    cos, sin = cos_ref[...], sin_ref[...]
    k_rope = kr1_ref[...] * cos + kr2_ref[...] * sin
    scale = (D_NOPE + D_ROPE) ** -0.5 * LOG2E
    for h in range(D_HEADS):
        cols = slice(h * LANES, (h + 1) * LANES)
        q = _dot(cqn, wq1_ref[:, cols]) * cos + _dot(cqn, wq2_ref[:, cols]) * sin
        q_ref[:, cols] = (q * scale).astype(q_ref.dtype)
        k_ref[:, cols] = (_dot(c, wk_ref[:, cols]) + k_rope).astype(k_ref.dtype)
    v_ref[...] = _dot(c, wv_ref[...]).astype(v_ref.dtype)


def _mla_prep(hf, cos, sin, gq, gkv, wq1, wq2, wk, wv, seq):
    m = hf.shape[0]
    n_seq = seq // TM
    const = lambda shape: pl.BlockSpec(shape, lambda i: (0, 0))
    hq = D_HEADS * LANES
    return pl.pallas_call(
        _mla_prep_kernel,
        grid=(m // TM,),
        in_specs=[pl.BlockSpec((TM, D_Q_LORA), lambda i: (i, 1)),
                  pl.BlockSpec((TM, D_KV_LORA), lambda i: (i, 3)),
                  pl.BlockSpec((TM, LANES), lambda i: (i, 8)),
                  pl.BlockSpec((TM, LANES), lambda i: (i, 9)),
                  pl.BlockSpec((TM, LANES), lambda i: (i % n_seq, 0)),
                  pl.BlockSpec((TM, LANES), lambda i: (i % n_seq, 0)),
                  const((1, D_Q_LORA)), const((1, D_KV_LORA)),
                  const(wq1.shape), const(wq2.shape), const(wk.shape), const(wv.shape)],
        out_specs=[pl.BlockSpec((TM, hq), lambda i: (i, 0)), pl.BlockSpec((TM, hq), lambda i: (i, 0)),
                   pl.BlockSpec((TM, D_HEADS * D_V), lambda i: (i, 0))],
        out_shape=[jax.ShapeDtypeStruct((m, hq), BF16), jax.ShapeDtypeStruct((m, hq), BF16),
                   jax.ShapeDtypeStruct((m, D_HEADS * D_V), BF16)],
        compiler_params=_cparams(("arbitrary",)),
        name="mla_prep",
    )(hf, hf, hf, hf, cos, sin, gq.reshape(1, -1), gkv.reshape(1, -1), wq1, wq2, wk, wv)


def _mla_kernel(qa_ref, qb_ref, k_ref, v_ref, oa_ref, ob_ref, vt_ref, *, n_d, n_heads):
    t = T_ATT
    pi = pl.program_id(2)

    @pl.when(pi == 0)
    def _():
        _fill_transposed(vt_ref, v_ref, D_V)

    head_cols = [slice(h * LANES, (h + 1) * LANES) for h in range(n_heads)]
    key_minus_query = (lax.broadcasted_iota(jnp.int32, (2 * t, t), 0)
                       - lax.broadcasted_iota(jnp.int32, (2 * t, t), 1))

    def score(n, q_tile, i, s, diag):
        return jnp.where(key_minus_query <= (q_tile - 2 * i) * t, s, NEG_INF) if diag else s

    out_a, out_b = _flash_balanced(pi, n_d, [qa_ref[0, :, c] for c in head_cols], [qb_ref[0, :, c] for c in head_cols],
                                   [k_ref] * n_heads, head_cols, vt_ref,
                                   [_value_rows(h, D_V) for h in range(n_heads)], score)
    oa_ref[0] = jnp.concatenate(out_a, axis=0).T.astype(oa_ref.dtype)
    ob_ref[0] = jnp.concatenate(out_b, axis=0).T.astype(ob_ref.dtype)


def _mla_attention(q3, k3, v3):
    b, s, _ = q3.shape
    t = T_ATT
    n_d = s // t
    n_p = n_d // 2
    nh = MLA_HEADS_PER_STEP
    o_a, o_b = pl.pallas_call(
        functools.partial(_mla_kernel, n_d=n_d, n_heads=nh),
        grid=(D_HEADS // nh, b, n_p),
        in_specs=[pl.BlockSpec((1, t, nh * LANES), lambda hp, bi, pi: (bi, pi, hp)),
                  pl.BlockSpec((1, t, nh * LANES), lambda hp, bi, pi: (bi, n_d - 1 - pi, hp)),
                  pl.BlockSpec((1, s, nh * LANES), lambda hp, bi, pi: (bi, 0, hp)),
                  pl.BlockSpec((1, s, nh * D_V), lambda hp, bi, pi: (bi, 0, hp))],
        out_specs=[pl.BlockSpec((1, t, nh * D_V), lambda hp, bi, pi: (bi, pi, hp)),
                   pl.BlockSpec((1, t, nh * D_V), lambda hp, bi, pi: (bi, n_p - 1 - pi, hp))],
        out_shape=[jax.ShapeDtypeStruct((b, s // 2, D_HEADS * D_V), BF16)] * 2,
        scratch_shapes=[pltpu.VMEM((nh * (D_V + ONES_ROWS), s), BF16)],
        compiler_params=_cparams(("arbitrary", "arbitrary", "arbitrary")),
        name="mla_attention",
    )(q3, q3, k3, v3)
    return _join_halves(o_a, o_b)


def _odd_out_kernel(gate_ref, oc_ref, os1_ref, os2_ref, ow_ref, od1_ref, od2_ref, x_ref, w_ref, g_ref, b_ref,
                    out_ref, *, tiles_per_seq):
    gates = 1.0 / (1.0 + jnp.exp(-gate_ref[...]))
    lo = _lane_lo(TM)
    n_c = C_HEADS * C_DIM
    o_slc = _pick_half(os1_ref, os2_ref, tiles_per_seq)
    y = _dot(_pick_half(od1_ref, od2_ref, tiles_per_seq), w_ref[n_c:, :])
    for blk in range(n_c // LANES):
        cols = slice(blk * LANES, (blk + 1) * LANES)
        acc = jnp.zeros((TM, LANES), F32)
        for br, o_br in enumerate((oc_ref[:, cols], o_slc[:, cols], ow_ref[:, cols])):
            c0 = br * C_HEADS + 2 * blk
            gexp = jnp.where(lo, gates[:, c0:c0 + 1], gates[:, c0 + 1:c0 + 2])
            acc = acc + gexp * o_br
        y = y + _dot(acc.astype(BF16), w_ref[cols, :])
    out_ref[...] = _residual_ln(x_ref[...], y, g_ref[...], b_ref[...])


def _odd_out(hf, o_cmp, o_slc_halves, o_win, o_d_halves, x, w_out, g, beta, seq):
    m = x.shape[0]
    n_c = C_HEADS * C_DIM
    tiles_per_seq = seq // TM
    row = lambda width: pl.BlockSpec((TM, width), lambda i: (i, 0))
    const = lambda shape: pl.BlockSpec(shape, lambda i: (0, 0))
    return pl.pallas_call(
        functools.partial(_odd_out_kernel, tiles_per_seq=tiles_per_seq),
        grid=(m // TM,),
        in_specs=[pl.BlockSpec((TM, LANES), lambda i: (i, 2)),
                  row(n_c)] + _half_specs(n_c, tiles_per_seq) + [row(n_c)] + _half_specs(D_HEADS * D_V, tiles_per_seq)
                 + [row(D_MODEL), const(w_out.shape), const((1, D_MODEL)), const((1, D_MODEL))],
        out_specs=row(D_MODEL),
        out_shape=jax.ShapeDtypeStruct((m, D_MODEL), F32),
        compiler_params=_cparams(("arbitrary",)),
        name="odd_out",
    )(hf, o_cmp, *o_slc_halves, o_win, *o_d_halves, x, w_out, g.reshape(1, -1), beta.reshape(1, -1))


ODD_BF16_W = 1536
ODD_F32_W = 1280


def _odd_in_columns():
    q_c = C_HEADS * C_DIM
    kv = lambda br, which, g: q_c + ((br * 2 + which) * C_KV_GROUPS + g) * C_DIM + np.arange(C_DIM)
    gate0 = q_c + 3 * 2 * C_KV_GROUPS * C_DIM
    cq0 = gate0 + 3 * C_HEADS
    ckv0 = cq0 + D_Q_LORA
    kr0 = ckv0 + D_KV_LORA
    zeros = lambda n: np.full(n, -1)
    cols = [np.arange(q_c)]
    for br in (1, 2):
        for which in (0, 1):
            for g in range(C_KV_GROUPS):
                cols += [kv(br, which, g), kv(br, which, g)]
    assert sum(len(c) for c in cols) == ODD_BF16_W
    for which in (0, 1):
        cols += [kv(0, which, 0), kv(0, which, 1)]
    cols += [gate0 + np.arange(3 * C_HEADS), zeros(LANES - 3 * C_HEADS)]
    cols += [cq0 + np.arange(D_Q_LORA), ckv0 + np.arange(D_KV_LORA)]
    half = D_ROPE // 2
    kr = kr0 + np.arange(D_ROPE)
    cols += [zeros(D_NOPE), kr, zeros(LANES - D_NOPE - D_ROPE)]
    cols += [zeros(D_NOPE), kr[half:], kr[:half], zeros(LANES - D_NOPE - D_ROPE)]
    cols = np.concatenate(cols)
    assert len(cols) == ODD_BF16_W + ODD_F32_W
    return cols


def _gather_cols(w, cols):
    return jnp.where(jnp.asarray(cols >= 0)[None, :], w[:, np.maximum(cols, 0)], 0.0)


def _mla_weight_columns():
    dq = D_NOPE + D_ROPE
    half = D_ROPE // 2
    zeros = lambda n: np.full(n, -1)
    q1, q2, k1 = [], [], []
    for h in range(D_HEADS):
        rope0 = h * dq + D_NOPE
        q1 += [h * dq + np.arange(dq), zeros(LANES - dq)]
        q2 += [zeros(D_NOPE), rope0 + half + np.arange(half), rope0 + np.arange(half), zeros(LANES - dq)]
        k1 += [h * D_NOPE + np.arange(D_NOPE), zeros(LANES - D_NOPE)]
    return np.concatenate(q1), np.concatenate(q2), np.concatenate(k1)


def _rope_tables(seq):
    half = D_ROPE // 2
    inv = ROPE_THETA ** (-jnp.arange(half, dtype=F32) / half)
    ang = jnp.arange(seq).astype(F32)[:, None] * inv
    cos, sin = jnp.cos(ang), jnp.sin(ang)
    pad = jnp.zeros((seq, LANES - D_NOPE - D_ROPE), F32)
    cos_t = jnp.concatenate([jnp.ones((seq, D_NOPE), F32), cos, cos, pad], axis=1)
    sin_t = jnp.concatenate([jnp.zeros((seq, D_NOPE), F32), -sin, sin, pad], axis=1)
    return cos_t, sin_t


def kernel(x, rel_bias, ev_w_in, ev_w_out, ev_lambda, ev_subln, od_w_in, od_w_out, od_cmp_pe, od_cmp_w1, od_cmp_w2, od_q_norm, od_kv_norm, od_w_uq, od_w_uk, od_w_uv, ffn_w_up, ffn_conv_w, ffn_conv_b, ffn_w_down, ln_g, ln_b):
    b, s, _ = x.shape
    m = b * s
    assert s % 1024 == 0 and s // B_PAIRS[-1][1] >= 2 * T_DIL and s // SLC_BLOCK <= LANES
    n_d = s // T_ATT

    tb_t = rel_bias.T.astype(F32)
    tb_log2 = tb_t[:C_HEADS] * LOG2E
    n_prev_win = -(-(WIN_SIZE - 1) // T_ATT)
    tb_causal = _bias_table(tb_log2, _toeplitz_idx(1, n_d, T_ATT, s))
    tb_win = _bias_table(tb_log2, _toeplitz_idx(n_prev_win, n_prev_win + 1, T_ATT, WIN_SIZE - 1))
    tb_cmp = _bias_table(tb_log2, _cmp_idx(s)).reshape(C_HEADS, s, s // CMP_STRIDE)
    tb_dil = [_bias_table(tb_t[A_HEADS + i * B_HEADS:A_HEADS + (i + 1) * B_HEADS], _dilated_idx(d))
              for i, (_, d) in enumerate(B_PAIRS)]
    cos_t, sin_t = _rope_tables(s)
    odd_cols = _odd_in_columns()
    q1_cols, q2_cols, k1_cols = _mla_weight_columns()

    xf = x.reshape(m, D_MODEL)
    for l in range(DEPTH):
        i = l // 2
        if l % 2 == 0:
            n_a = EVEN_IN - EVEN_DILATED
            w_in = ev_w_in[i] * jnp.where(jnp.arange(EVEN_IN) < 2 * A_HEADS * A_QK, LOG2E, 1.0)
            h_a, h_b = _inproj(xf, w_in.astype(BF16), (n_a, EVEN_DILATED), (BF16, F32), 768)
            h3 = h_a.reshape(b, s, n_a)
            hd3 = h_b.reshape(b, s, EVEN_DILATED)
            lam_init = 0.8 - 0.6 * math.exp(-0.3 * l)
            o_a = _diff_attention(h3, tb_causal, ev_lambda[i].astype(F32), ev_subln[i], lam_init)
            obs, lses = [], []
            for p_idx, (_, d) in enumerate(B_PAIRS):
                o, lse = _dilated_attention(hd3, tb_dil[p_idx], p_idx, d)
                obs.append(o)
                lses.append(lse)
            xf = _even_out(o_a, obs, lses, xf, ev_w_out[i].astype(BF16), ln_g[l, 0], ln_b[l, 0], s)
        else:
            w_in = _gather_cols(od_w_in[i], odd_cols) * jnp.where(jnp.arange(len(odd_cols)) < C_HEADS * C_DIM, LOG2E, 1.0)
            hb, hf = _inproj(xf, w_in.astype(BF16), (ODD_BF16_W, ODD_F32_W), (BF16, F32), 256)
            hb3 = hb.reshape(b, s, ODD_BF16_W)
            n_ch = s // CMP_STRIDE
            chunks = (hf[:, :4 * C_DIM].reshape(b, n_ch, CMP_STRIDE, 4, C_DIM)
                      .transpose(0, 3, 1, 2, 4).reshape(b, 4, n_ch, CMP_STRIDE * C_DIM))
            pe = od_cmp_pe[i].reshape(2, 2, CMP_STRIDE * C_DIM)
            w2d = jnp.concatenate([od_cmp_w2[i], od_cmp_w2[i]], axis=-1).astype(BF16)
            kvc = _compress(chunks, pe, od_cmp_w1[i].astype(BF16), w2d)
            o_cmp, sel = _cmp_attention(hb3, kvc, tb_cmp)
            o_slc = _slc_attention(hb3, tb_causal, 4, 6, sel)
            o_win = _win_attention(hb3, tb_win, 8, 10, n_prev_win)
            q_d, k_d, v_d = _mla_prep(hf, cos_t, sin_t, od_q_norm[i], od_kv_norm[i],
                                      _gather_cols(od_w_uq[i], q1_cols).astype(BF16),
                                      _gather_cols(od_w_uq[i], q2_cols).astype(BF16),
                                      _gather_cols(od_w_uk[i], k1_cols).astype(BF16),
                                      od_w_uv[i].astype(BF16), s)
            o_d = _mla_attention(q_d.reshape(b, s, -1), k_d.reshape(b, s, -1), v_d.reshape(b, s, -1))
            xf = _odd_out(hf, o_cmp.reshape(m, -1), o_slc, o_win.reshape(m, -1), o_d, xf,
                          od_w_out[i].astype(BF16), ln_g[l, 0], ln_b[l, 0], s)
        xf = _ffn(xf, ffn_w_up[l].astype(BF16), ffn_conv_w[l], ffn_conv_b[l], ffn_w_down[l].astype(BF16),
                  ln_g[l, 1], ln_b[l, 1], s)
    return xf.reshape(b, s, D_MODEL)
```
